```python
import math
import jax, jax.numpy as jnp
from jax import lax
import numpy as np

D_MODEL = 1024
BATCH = 4
SEQ = 4096
DEPTH = 1
DEC_BATCH = 1
DEC_SEQ = 16384
PAST_LEN = 128

HEAD_DIM = 64
N_META = 16
DIFF_HEADS = 4
DIFF_WIDTH = DIFF_HEADS * 2 * HEAD_DIM
SWA_Q_HEADS = 8
SWA_KV_HEADS = 2
SWA_GROUP = SWA_Q_HEADS // SWA_KV_HEADS
SWA_WIDTH = SWA_Q_HEADS * HEAD_DIM
SWA_KV_WIDTH = SWA_KV_HEADS * HEAD_DIM
MIX_WIDTH = DIFF_WIDTH + SWA_WIDTH
IN_COLS = 3 * DIFF_WIDTH + SWA_WIDTH + 2 * SWA_KV_WIDTH
WINDOW = 128
BLOCK = 128
N_BUCKETS = 32
MAX_DISTANCE = 128
N_BIAS_COLS = 2 * DIFF_HEADS + SWA_Q_HEADS
N_EXPERTS = 16
EC_FACTOR = 2
D_FF_EXPERT = 2752
RMS_EPS = 1e-6

kernel_name = "hymba_diff_swa_ec_encoder"


def rmsnorm(x, g):
    xf = x.astype(jnp.float32)
    y = xf * lax.rsqrt(jnp.mean(xf * xf, axis=-1, keepdims=True) + RMS_EPS)
    return (y * g.astype(jnp.float32)).astype(x.dtype)


def rel_bucket(rel):
    half = N_BUCKETS // 2
    max_exact = half // 2
    n = jnp.abs(rel)
    large = max_exact + (jnp.log(jnp.maximum(n, 1).astype(jnp.float32) / max_exact)
                         / math.log(MAX_DISTANCE / max_exact) * (half - max_exact)).astype(jnp.int32)
    large = jnp.minimum(large, half - 1)
    return jnp.where(rel > 0, half, 0) + jnp.where(n < max_exact, n, large)


def diff_attention(q, k, v, L, bias_table, lq1, lk1, lq2, lk2, subln_g, lam_init):
    B, Lp = q.shape[0], q.shape[1]
    nb = Lp // BLOCK
    f32 = jnp.float32
    lam = (jnp.exp(jnp.sum(lq1.astype(f32) * lk1.astype(f32)))
           - jnp.exp(jnp.sum(lq2.astype(f32) * lk2.astype(f32))) + lam_init)
    key_pos = jnp.arange(Lp)
    key_valid = key_pos < L
    table = bias_table.astype(f32)
    qb = jnp.moveaxis(q.reshape(B, nb, BLOCK, DIFF_HEADS, 2, HEAD_DIM), 1, 0)
    starts = jnp.arange(nb) * BLOCK

    def one_block(args):
        qblk, start = args
        s = jnp.einsum('bqhmd,bkhmd->bhmqk', qblk, k, preferred_element_type=f32)
        rel = key_pos[None, :] - (start + jnp.arange(BLOCK))[:, None]
        bias = table[rel_bucket(rel)].reshape(BLOCK, Lp, DIFF_HEADS, 2).transpose(2, 3, 0, 1)
        s = jnp.where(key_valid, s + bias, -jnp.inf)
        p = jax.nn.softmax(s, axis=-1)
        a = p[:, :, 0] - lam * p[:, :, 1]
        return jnp.einsum('bhqk,bkhe->bqhe', a.astype(v.dtype), v)

    o = lax.map(one_block, (qb, starts))
    o = jnp.moveaxis(o, 0, 1).reshape(B, Lp, DIFF_HEADS, 2 * HEAD_DIM)
    o = rmsnorm(o, subln_g) * (1.0 - lam_init)
    return o.reshape(B, Lp, DIFF_WIDTH)


def window_attention(q, k, v, L, bias_table, sink):
    B, Lp = q.shape[0], q.shape[1]
    nb = Lp // BLOCK
    f32 = jnp.float32
    qb = q.reshape(B, nb, BLOCK, SWA_KV_HEADS, SWA_GROUP, HEAD_DIM)

    def band(t):
        tb = jnp.pad(t, ((0, 0), (BLOCK, BLOCK), (0, 0), (0, 0))).reshape(B, nb + 2, BLOCK, SWA_KV_HEADS, HEAD_DIM)
        return jnp.concatenate([tb[:, :-2], tb[:, 1:-1], tb[:, 2:]], axis=2)

    kb, vb = band(k), band(v)
    s = jnp.einsum('bnqkgd,bnskd->bnkgqs', qb, kb, preferred_element_type=f32)
    rel = jnp.arange(3 * BLOCK)[None, :] - BLOCK - jnp.arange(BLOCK)[:, None]
    bias = bias_table.astype(f32)[rel_bucket(rel)].reshape(BLOCK, 3 * BLOCK, SWA_KV_HEADS, SWA_GROUP).transpose(2, 3, 0, 1)
    key_pos = (jnp.arange(nb) * BLOCK - BLOCK)[:, None] + jnp.arange(3 * BLOCK)[None, :]
    valid = ((key_pos >= 0) & (key_pos < L))[:, None, :] & (jnp.abs(rel) <= WINDOW)[None]
    s = jnp.where(valid[None, :, None, None], s + bias, -jnp.inf)
    sink_logit = jnp.broadcast_to(sink.astype(f32).reshape(SWA_KV_HEADS, SWA_GROUP, 1, 1), s.shape[:-1] + (1,))
    p = jax.nn.softmax(jnp.concatenate([s, sink_logit], axis=-1), axis=-1)[..., :-1]
    o = jnp.einsum('bnkgqs,bnskd->bnqkgd', p.astype(v.dtype), vb)
    return o.reshape(B, Lp, SWA_WIDTH)


def expert_choice_ffn(x, w_router, w_gate, w_up, w_down):
    N, D = x.shape
    C = EC_FACTOR * N // N_EXPERTS
    probs = jax.nn.softmax(jnp.dot(x, w_router, preferred_element_type=jnp.float32), axis=-1)
    gate, idx = lax.top_k(probs.T, C)
    xs = x[idx]
    h = jax.nn.silu(jnp.einsum('ecd,edf->ecf', xs, w_gate)) * jnp.einsum('ecd,edf->ecf', xs, w_up)
    out = jnp.einsum('ecf,efd->ecd', h, w_down) * gate[..., None].astype(x.dtype)
    return jnp.zeros_like(x).at[idx.reshape(-1)].add(out.reshape(-1, D))


def encoder_trunk(x, meta_tokens, rel_bias, attn_norm_g, w_in, diff_lambda_q1, diff_lambda_k1,
                  diff_lambda_q2, diff_lambda_k2, diff_subln_g, swa_sink, w_out, ffn_norm_g,
                  w_router, w_gate, w_up, w_down, final_norm_g):
    B, S, D = x.shape
    L = S + N_META
    Lp = -(-L // BLOCK) * BLOCK
    scale = HEAD_DIM ** -0.5
    splits = [int(c) for c in np.cumsum([DIFF_WIDTH, DIFF_WIDTH, DIFF_WIDTH, SWA_WIDTH, SWA_KV_WIDTH])]
    h = jnp.concatenate([jnp.broadcast_to(meta_tokens.astype(x.dtype)[None], (B, N_META, D)), x], axis=1)
    for layer in range(DEPTH):
        lam_init = 0.8 - 0.6 * math.exp(-0.3 * layer)
        a = jnp.pad(rmsnorm(h, attn_norm_g[layer]), ((0, 0), (0, Lp - L), (0, 0)))
        proj = a @ w_in[layer]
        dq, dk, dv, sq, sk, sv = jnp.split(proj, splits, axis=-1)
        diff_o = diff_attention(
            dq.reshape(B, Lp, DIFF_HEADS, 2, HEAD_DIM) * scale,
            dk.reshape(B, Lp, DIFF_HEADS, 2, HEAD_DIM),
            dv.reshape(B, Lp, DIFF_HEADS, 2 * HEAD_DIM),
            L, rel_bias[:, :2 * DIFF_HEADS], diff_lambda_q1[layer], diff_lambda_k1[layer],
            diff_lambda_q2[layer], diff_lambda_k2[layer], diff_subln_g[layer], lam_init)
        swa_o = window_attention(
            sq.reshape(B, Lp, SWA_KV_HEADS, SWA_GROUP, HEAD_DIM) * scale,
            sk.reshape(B, Lp, SWA_KV_HEADS, HEAD_DIM),
            sv.reshape(B, Lp, SWA_KV_HEADS, HEAD_DIM),
            L, rel_bias[:, 2 * DIFF_HEADS:], swa_sink[layer])
        mixed = jnp.concatenate([diff_o, swa_o], axis=-1)[:, :L]
        h = h + mixed @ w_out[layer]
        m = rmsnorm(h, ffn_norm_g[layer]).reshape(B * L, D)
        h = h + expert_choice_ffn(m, w_router[layer], w_gate[layer], w_up[layer], w_down[layer]).reshape(B, L, D)
    return rmsnorm(h, final_norm_g)[:, N_META:]


def setup_inputs(seed: int = 0) -> dict:
    key = jax.random.key(seed)
    ks = jax.random.split(key, 20)
    f32 = jnp.float32
    nrm = lambda k, shape, s: jax.random.normal(k, shape, f32) * s
    gain = lambda k, shape: 1.0 + 0.02 * jax.random.normal(k, shape, f32)
    return {
        "x_prompt": nrm(ks[0], (BATCH, SEQ, D_MODEL), 1.0),
        "x_sample": nrm(ks[1], (DEC_BATCH, DEC_SEQ, D_MODEL), 1.0),
        "meta_tokens": nrm(ks[2], (N_META, D_MODEL), 1.0),
        "rel_bias": nrm(ks[3], (N_BUCKETS, N_BIAS_COLS), 0.5),
        "attn_norm_g": gain(ks[4], (DEPTH, D_MODEL)),
        "w_in": nrm(ks[5], (DEPTH, D_MODEL, IN_COLS), D_MODEL ** -0.5),
        "diff_lambda_q1": nrm(ks[6], (DEPTH, HEAD_DIM), 0.1),
        "diff_lambda_k1": nrm(ks[7], (DEPTH, HEAD_DIM), 0.1),
        "diff_lambda_q2": nrm(ks[8], (DEPTH, HEAD_DIM), 0.1),
        "diff_lambda_k2": nrm(ks[9], (DEPTH, HEAD_DIM), 0.1),
        "diff_subln_g": gain(ks[10], (DEPTH, 2 * HEAD_DIM)),
        "swa_sink": nrm(ks[11], (DEPTH, SWA_Q_HEADS), 0.5),
        "w_out": nrm(ks[12], (DEPTH, MIX_WIDTH, D_MODEL), MIX_WIDTH ** -0.5),
        "ffn_norm_g": gain(ks[13], (DEPTH, D_MODEL)),
        "w_router": nrm(ks[14], (DEPTH, D_MODEL, N_EXPERTS), D_MODEL ** -0.5),
        "w_gate": nrm(ks[15], (DEPTH, N_EXPERTS, D_MODEL, D_FF_EXPERT), D_MODEL ** -0.5),
        "w_up": nrm(ks[16], (DEPTH, N_EXPERTS, D_MODEL, D_FF_EXPERT), D_MODEL ** -0.5),
        "w_down": nrm(ks[17], (DEPTH, N_EXPERTS, D_FF_EXPERT, D_MODEL), D_FF_EXPERT ** -0.5),
        "final_norm_g": gain(ks[18], (D_MODEL,)),
    }


def reference(x_prompt, x_sample, meta_tokens, rel_bias, attn_norm_g, w_in, diff_lambda_q1, diff_lambda_k1,
              diff_lambda_q2, diff_lambda_k2, diff_subln_g, swa_sink, w_out, ffn_norm_g, w_router,
              w_gate, w_up, w_down, final_norm_g):
    y_prompt = encoder_trunk(x_prompt, meta_tokens, rel_bias, attn_norm_g, w_in, diff_lambda_q1, diff_lambda_k1,
                             diff_lambda_q2, diff_lambda_k2, diff_subln_g, swa_sink, w_out, ffn_norm_g,
                             w_router, w_gate, w_up, w_down, final_norm_g)
    y_sample = encoder_trunk(x_sample, meta_tokens, rel_bias, attn_norm_g, w_in, diff_lambda_q1, diff_lambda_k1,
                             diff_lambda_q2, diff_lambda_k2, diff_subln_g, swa_sink, w_out, ffn_norm_g,
                             w_router, w_gate, w_up, w_down, final_norm_g)
    return (y_prompt, y_sample)
```

```python
import functools
import math

import jax
import jax.numpy as jnp
import numpy as np
from jax import lax
from jax.experimental import pallas as pl
from jax.experimental.pallas import tpu as pltpu

F32 = jnp.float32
BF16 = jnp.bfloat16

D_MODEL = 1024
HEAD_DIM = 64
N_META = 16
DIFF_HEADS = 4
DIFF_WIDTH = DIFF_HEADS * 2 * HEAD_DIM
SWA_Q_HEADS = 8
SWA_KV_HEADS = 2
SWA_GROUP = SWA_Q_HEADS // SWA_KV_HEADS
SWA_WIDTH = SWA_Q_HEADS * HEAD_DIM
SWA_KV_WIDTH = SWA_KV_HEADS * HEAD_DIM
MIX_WIDTH = DIFF_WIDTH + SWA_WIDTH
IN_COLS = 3 * DIFF_WIDTH + SWA_WIDTH + 2 * SWA_KV_WIDTH
WINDOW = 128
N_BUCKETS = 32
MAX_DISTANCE = 128
N_EXPERTS = 16
EC_FACTOR = 2
D_FF = 2752
RMS_EPS = 1e-6
LAM_INIT = 0.8 - 0.6 * math.exp(-0.3 * 0)

LANES = 128
BF16_ROWS = 16
VMEM_LIMIT = 56 * 1024 * 1024

SEQ_TILE = 256
SWA_KEYS = SEQ_TILE + 2 * WINDOW
FF_TILE = 256
CHUNK = LANES
WIN_ROWS = CHUNK + BF16_ROWS
NEG = -1e30

_NT = (((1,), (1,)), ((), ()))


def _cparams(sem):
    return pltpu.CompilerParams(dimension_semantics=sem, vmem_limit_bytes=VMEM_LIMIT)


def _largest_divisor(n, cap):
    return max(d for d in range(1, cap + 1) if n % d == 0)


def _rel_bucket(rel):
    half = N_BUCKETS // 2
    max_exact = half // 2
    n = jnp.abs(rel)
    large = max_exact + (jnp.log(jnp.maximum(n, 1).astype(F32) / max_exact)
                         / math.log(MAX_DISTANCE / max_exact) * (half - max_exact)).astype(jnp.int32)
    large = jnp.minimum(large, half - 1)
    return jnp.where(rel > 0, half, 0) + jnp.where(n < max_exact, n, large)


def _proj_kernel(h_ref, g_ref, w_ref, proj_ref, vt_ref):
    x = h_ref[...]
    ms = jnp.mean(x * x, axis=-1, keepdims=True)
    a = (x * lax.rsqrt(ms + RMS_EPS) * g_ref[...]).astype(BF16)
    proj = jnp.dot(a, w_ref[...], preferred_element_type=F32)
    proj_ref[...] = proj.astype(BF16)
    for h in range(DIFF_HEADS):
        v = proj[:, 2 * DIFF_WIDTH + h * LANES: 2 * DIFF_WIDTH + (h + 1) * LANES]
        vt_ref[h] = v.T.astype(BF16)


def _proj(h0, g, w_in_bf16, tm):
    rows = h0.shape[0]
    return pl.pallas_call(
        _proj_kernel,
        grid=(rows // tm,),
        in_specs=[
            pl.BlockSpec((tm, D_MODEL), lambda i: (i, 0)),
            pl.BlockSpec((1, D_MODEL), lambda i: (0, 0)),
            pl.BlockSpec((D_MODEL, IN_COLS), lambda i: (0, 0)),
        ],
        out_specs=[
            pl.BlockSpec((tm, IN_COLS), lambda i: (i, 0)),
            pl.BlockSpec((DIFF_HEADS, LANES, tm), lambda i: (0, 0, i)),
        ],
        out_shape=[
            jax.ShapeDtypeStruct((rows, IN_COLS), BF16),
            jax.ShapeDtypeStruct((DIFF_HEADS, LANES, rows), BF16),
        ],
        compiler_params=_cparams(("parallel",)),
        name="proj",
    )(h0, g, w_in_bf16)


def _diff_kernel(consts_ref, lamv_ref, q_ref, k_ref, vt_ref, bias_ref, g_ref, o_ref,
                 acc_ref, m_ref, l_ref, *, seq_len, nkv):
    T = SEQ_TILE
    h = pl.program_id(1)
    i = pl.program_id(2)
    q = q_ref[...]
    lane = lax.broadcasted_iota(jnp.int32, q.shape, 1)
    zero = jnp.zeros_like(q)
    q_maps = (jnp.where(lane < HEAD_DIM, q, zero), jnp.where(lane >= HEAD_DIM, q, zero))

    acc_ref[...] = jnp.zeros_like(acc_ref)
    m_ref[...] = jnp.full_like(m_ref, NEG)
    l_ref[...] = jnp.zeros_like(l_ref)

    def chunk(j, *, tile, side, mask):
        off = pl.multiple_of(j * T, T)
        kb = k_ref[pl.ds(off, T), :]
        vb = vt_ref[0, :, pl.ds(off, T)]
        if mask:
            kpos = off + lax.broadcasted_iota(jnp.int32, (T, 1), 0)
            kvalid = kpos < seq_len
        for mp in range(2):
            s = lax.dot_general(kb, q_maps[mp], _NT, preferred_element_type=F32)
            if tile:
                s = s + bias_ref[0, mp, j - i + 1]
                c = 0.0
            else:
                c = consts_ref[h, 2 * mp + side]
            if mask:
                s = jnp.where(kvalid, s, NEG)
            m_old = m_ref[mp]
            m_new = jnp.maximum(m_old, jnp.max(s, axis=0, keepdims=True) + c)
            alpha = jnp.exp(m_old - m_new)
            p = jnp.exp(s - (m_new - c))
            l_ref[mp] = alpha * l_ref[mp] + jnp.sum(p, axis=0, keepdims=True)
            acc_ref[mp] = alpha * acc_ref[mp] + jnp.dot(vb, p.astype(BF16), preferred_element_type=F32)
            m_ref[mp] = m_new

    def loop(lo, hi, **kw):
        def body(j, carry):
            chunk(j, **kw)
            return carry
        lax.fori_loop(lo, hi, body, 0)

    near_lo = jnp.maximum(i - 1, 0)
    near_hi = jnp.minimum(i + 2, nkv)
    loop(0, near_lo, tile=False, side=0, mask=False)
    loop(near_lo, near_hi, tile=True, side=0, mask=True)
    loop(near_hi, nkv - 1, tile=False, side=1, mask=False)
    loop(jnp.maximum(near_hi, nkv - 1), nkv, tile=False, side=1, mask=True)

    lamv = lamv_ref[...]
    lam = (jnp.exp(jnp.sum(lamv[0:1] * lamv[1:2], axis=-1, keepdims=True))
           - jnp.exp(jnp.sum(lamv[2:3] * lamv[3:4], axis=-1, keepdims=True)) + LAM_INIT)
    o = acc_ref[0] / l_ref[0] - lam * (acc_ref[1] / l_ref[1])
    ms = jnp.mean(o * o, axis=0, keepdims=True)
    y = o * lax.rsqrt(ms + RMS_EPS) * g_ref[...] * (1.0 - LAM_INIT)
    o_ref[...] = y.T.astype(BF16)


def _diff_attention(proj, vt, consts, lamv, bias_t, subln_g, batch, P, seq_len):
    T = SEQ_TILE
    nq = P // T
    kern = functools.partial(_diff_kernel, seq_len=seq_len, nkv=nq)
    return pl.pallas_call(
        kern,
        grid=(batch, DIFF_HEADS, nq),
        in_specs=[
            pl.BlockSpec(memory_space=pltpu.SMEM),
            pl.BlockSpec((4, HEAD_DIM), lambda b, h, i: (0, 0)),
            pl.BlockSpec((T, LANES), lambda b, h, i: (b * nq + i, h)),
            pl.BlockSpec((P, LANES), lambda b, h, i: (b, DIFF_HEADS + h)),
            pl.BlockSpec((1, LANES, P), lambda b, h, i: (h, 0, b)),
            pl.BlockSpec((1, 2, 3, T, T), lambda b, h, i: (h, 0, 0, 0, 0)),
            pl.BlockSpec((LANES, 1), lambda b, h, i: (0, 0)),
        ],
        out_specs=pl.BlockSpec((T, LANES), lambda b, h, i: (b * nq + i, h)),
        out_shape=jax.ShapeDtypeStruct((batch * P, DIFF_WIDTH), BF16),
        scratch_shapes=[
            pltpu.VMEM((2, LANES, T), F32),
            pltpu.VMEM((2, 1, T), F32),
            pltpu.VMEM((2, 1, T), F32),
        ],
        compiler_params=_cparams(("parallel", "parallel", "parallel")),
        name="diff_attn",
    )(consts, lamv, proj, proj, vt, bias_t, subln_g)


def _swa_kernel(sink_ref, q_ref, k_ref, v_ref, bias_ref, o_ref, *, seq_len, P):
    T = SEQ_TILE
    i = pl.program_id(1)
    start = pl.multiple_of(jnp.clip(i * T - WINDOW, 0, P - SWA_KEYS), LANES)
    boff = pl.multiple_of(start - i * T + T, LANES)
    kw = k_ref[pl.ds(start, SWA_KEYS), :]
    vw = v_ref[pl.ds(start, SWA_KEYS), :]
    kvalid = (start + lax.broadcasted_iota(jnp.int32, (1, SWA_KEYS), 1)) < seq_len
    lane = lax.broadcasted_iota(jnp.int32, (T, LANES), 1)
    for j in range(SWA_GROUP):
        qp = q_ref[:, j * LANES:(j + 1) * LANES]
        zero = jnp.zeros_like(qp)
        outs = []
        for kvh in range(SWA_KV_HEADS):
            head = kvh * SWA_GROUP + j
            in_half = (lane < HEAD_DIM) if kvh == 0 else (lane >= HEAD_DIM)
            qe = jnp.where(in_half, qp, zero)
            s = lax.dot_general(qe, kw, _NT, preferred_element_type=F32)
            s = s + bias_ref[head, :, pl.ds(boff, SWA_KEYS)]
            s = jnp.where(kvalid, s, NEG)
            sink = sink_ref[head]
            m = jnp.maximum(jnp.max(s, axis=-1, keepdims=True), sink)
            p = jnp.exp(s - m)
            l = jnp.sum(p, axis=-1, keepdims=True) + jnp.exp(sink - m)
            outs.append(jnp.dot(p.astype(BF16), vw, preferred_element_type=F32) / l)
        o_ref[:, j * LANES:(j + 1) * LANES] = jnp.where(lane < HEAD_DIM, outs[0], outs[1]).astype(BF16)


def _swa_attention(proj, sink, bias_w, batch, P, seq_len):
    T = SEQ_TILE
    nq = P // T
    q_blk = (3 * DIFF_WIDTH) // SWA_WIDTH
    k_blk = (3 * DIFF_WIDTH + SWA_WIDTH) // LANES
    kern = functools.partial(_swa_kernel, seq_len=seq_len, P=P)
    return pl.pallas_call(
        kern,
        grid=(batch, nq),
        in_specs=[
            pl.BlockSpec(memory_space=pltpu.SMEM),
            pl.BlockSpec((T, SWA_WIDTH), lambda b, i: (b * nq + i, q_blk)),
            pl.BlockSpec((P, LANES), lambda b, i: (b, k_blk)),
            pl.BlockSpec((P, LANES), lambda b, i: (b, k_blk + 1)),
            pl.BlockSpec((SWA_Q_HEADS, T, SWA_KEYS + T), lambda b, i: (0, 0, 0)),
        ],
        out_specs=pl.BlockSpec((T, SWA_WIDTH), lambda b, i: (b * nq + i, 0)),
        out_shape=jax.ShapeDtypeStruct((batch * P, SWA_WIDTH), BF16),
        compiler_params=_cparams(("parallel", "parallel")),
        name="swa_attn",
    )(sink, proj, proj, proj, bias_w)


def _out_kernel(do_ref, so_ref, h_ref, w_ref, g_ref, wr_ref, h2_ref, m_ref, p3_ref, *, seq_len, tm):
    i = pl.program_id(1)
    h2 = (h_ref[...]
          + jnp.dot(do_ref[...], w_ref[:DIFF_WIDTH, :], preferred_element_type=F32)
          + jnp.dot(so_ref[...], w_ref[DIFF_WIDTH:, :], preferred_element_type=F32))
    h2_ref[...] = h2
    ms = jnp.mean(h2 * h2, axis=-1, keepdims=True)
    mf = h2 * lax.rsqrt(ms + RMS_EPS) * g_ref[...]
    m_ref[...] = mf.astype(BF16)
    logits = lax.dot_general(wr_ref[...], mf, _NT, preferred_element_type=F32,
                             precision=lax.Precision.HIGHEST)
    e = jnp.exp(logits - jnp.max(logits, axis=0, keepdims=True))
    probs = e / jnp.sum(e, axis=0, keepdims=True)
    pos = i * tm + lax.broadcasted_iota(jnp.int32, (1, tm), 1)
    probs = jnp.where(pos < seq_len, probs, -1.0)
    for c in range(tm // CHUNK):
        p3_ref[c] = probs[:, c * CHUNK:(c + 1) * CHUNK]


def _out_router(diff_o, swa_o, h0, w_out_bf16, g, w_router_t, batch, P, seq_len):
    tm = SEQ_TILE
    nq = P // tm
    rows = batch * P
    kern = functools.partial(_out_kernel, seq_len=seq_len, tm=tm)
    return pl.pallas_call(
        kern,
        grid=(batch, nq),
        in_specs=[
            pl.BlockSpec((tm, DIFF_WIDTH), lambda b, i: (b * nq + i, 0)),
            pl.BlockSpec((tm, SWA_WIDTH), lambda b, i: (b * nq + i, 0)),
            pl.BlockSpec((tm, D_MODEL), lambda b, i: (b * nq + i, 0)),
            pl.BlockSpec((MIX_WIDTH, D_MODEL), lambda b, i: (0, 0)),
            pl.BlockSpec((1, D_MODEL), lambda b, i: (0, 0)),
            pl.BlockSpec((N_EXPERTS, D_MODEL), lambda b, i: (0, 0)),
        ],
        out_specs=[
            pl.BlockSpec((tm, D_MODEL), lambda b, i: (b * nq + i, 0)),
            pl.BlockSpec((tm, D_MODEL), lambda b, i: (b * nq + i, 0)),
            pl.BlockSpec((tm // CHUNK, N_EXPERTS, CHUNK), lambda b, i: (b * nq + i, 0, 0)),
        ],
        out_shape=[
            jax.ShapeDtypeStruct((rows, D_MODEL), F32),
            jax.ShapeDtypeStruct((rows, D_MODEL), BF16),
            jax.ShapeDtypeStruct((rows // CHUNK, N_EXPERTS, CHUNK), F32),
        ],
        compiler_params=_cparams(("parallel", "parallel")),
        name="out_router",
    )(diff_o, swa_o, h0, w_out_bf16, g, w_router_t)


def _topk_kernel(p3_ref, pos_ref, post_ref, gatet_ref, base_ref, incl_ref, flag_ref, *, capacity, nc):
    E = N_EXPERTS
    probs = p3_ref[...]
    bits = pltpu.bitcast(probs, jnp.int32)

    def count(pred):
        part = jnp.sum(jnp.where(pred, 1.0, 0.0), axis=0, keepdims=True)
        return jnp.broadcast_to(jnp.sum(part, axis=-1, keepdims=True), part.shape)

    def bisect(_, lohi):
        lo, hi = lohi
        mid = lo + ((hi - lo + 1) >> 1)
        ok = count(bits >= mid) >= capacity
        return jnp.where(ok, mid, lo), jnp.where(ok, hi, mid - 1)

    lo0 = jnp.zeros((1, E, CHUNK), jnp.int32)
    hi0 = jnp.full((1, E, CHUNK), 0x7F800000, jnp.int32)
    thr, _ = lax.fori_loop(0, 32, bisect, (lo0, hi0))
    gt = bits > thr
    eq = bits == thr
    need = capacity - count(gt)[0]

    tri = (lax.broadcasted_iota(jnp.int32, (CHUNK, CHUNK), 0)
           <= lax.broadcasted_iota(jnp.int32, (CHUNK, CHUNK), 1)).astype(BF16)

    def inclusive_prefix(flags):
        f2 = flags.astype(BF16).reshape(nc * E, CHUNK)
        return jnp.dot(f2, tri, preferred_element_type=F32).reshape(nc, E, CHUNK)

    eqf = jnp.where(eq, 1.0, 0.0)
    incl_ref[...] = inclusive_prefix(eqf)
    flag_ref[...] = eqf

    def tie_scan(c, run):
        inc = incl_ref[c]
        e_c = flag_ref[c]
        take = (e_c > 0.0) & ((run + inc - e_c) < need)
        flag_ref[c] = jnp.where(take, 1.0, 0.0)
        return run + jnp.broadcast_to(inc[:, CHUNK - 1:CHUNK], inc.shape)

    lax.fori_loop(0, nc, tie_scan, jnp.zeros((E, CHUNK), F32))
    self = jnp.where(gt, 1.0, flag_ref[...])
    flag_ref[...] = self
    incl_ref[...] = inclusive_prefix(self)

    zpad = jnp.zeros((CHUNK - E, CHUNK), F32)

    def to_token_major(x):
        return jnp.concatenate([x, zpad], axis=0).T[:, :E]

    def pos_scan(c, run):
        inc = incl_ref[c]
        s_c = flag_ref[c]
        sel = s_c > 0.0
        pos = jnp.where(sel, run + inc - s_c, -1.0)
        gate = jnp.where(sel, p3_ref[c], 0.0)
        off = pl.multiple_of(c * CHUNK, CHUNK)
        pos_ref[:, pl.ds(off, CHUNK)] = pos
        post_ref[pl.ds(off, CHUNK), :] = to_token_major(pos)
        gatet_ref[pl.ds(off, CHUNK), :] = to_token_major(gate)
        base_ref[c] = run.astype(jnp.int32)
        return run + jnp.broadcast_to(inc[:, CHUNK - 1:CHUNK], inc.shape)

    lax.fori_loop(0, nc, pos_scan, jnp.zeros((E, CHUNK), F32))


def _topk(probs3, capacity):
    nc = probs3.shape[0]
    nt = nc * CHUNK
    kern = functools.partial(_topk_kernel, capacity=capacity, nc=nc)
    return pl.pallas_call(
        kern,
        out_shape=[
            jax.ShapeDtypeStruct((N_EXPERTS, nt), F32),
            jax.ShapeDtypeStruct((nt, N_EXPERTS), F32),
            jax.ShapeDtypeStruct((nt, N_EXPERTS), F32),
            jax.ShapeDtypeStruct((nc, N_EXPERTS, CHUNK), jnp.int32),
        ],
        scratch_shapes=[
            pltpu.VMEM((nc, N_EXPERTS, CHUNK), F32),
            pltpu.VMEM((nc, N_EXPERTS, CHUNK), F32),
        ],
        compiler_params=pltpu.CompilerParams(vmem_limit_bytes=VMEM_LIMIT),
        name="topk",
    )(probs3)


def _gather_kernel(base_ref, pos_ref, m_ref, xc_ref, *, nc, sub):
    e = pl.program_id(0)
    sb = pl.program_id(1)

    @pl.when(sb == 0)
    def _():
        xc_ref[...] = jnp.zeros_like(xc_ref)

    row = lax.broadcasted_iota(jnp.int32, (WIN_ROWS, CHUNK), 0).astype(F32)
    erow = lax.broadcasted_iota(jnp.int32, (N_EXPERTS, CHUNK), 0)

    def body(u, carry):
        b0 = base_ref[e * nc + sb * sub + u]
        a = pl.multiple_of((b0 // BF16_ROWS) * BF16_ROWS, BF16_ROWS)
        off = pl.multiple_of(u * CHUNK, CHUNK)
        pblk = pos_ref[:, pl.ds(off, CHUNK)]
        prow = jnp.sum(jnp.where(erow == e, pblk, 0.0), axis=0, keepdims=True)
        onehot = jnp.where(row == prow - a.astype(F32), 1.0, 0.0).astype(BF16)
        x = m_ref[pl.ds(off, CHUNK), :]
        g = jnp.dot(onehot, x, preferred_element_type=F32)
        xc_ref[0, pl.ds(a, WIN_ROWS), :] += g.astype(BF16)
        return carry

    lax.fori_loop(0, sub, body, 0)


def _gather(base_flat, pos, m, cp):
    nt = pos.shape[1]
    nc = nt // CHUNK
    sub = _largest_divisor(nc, 17)
    ts = sub * CHUNK
    kern = functools.partial(_gather_kernel, nc=nc, sub=sub)
    return pl.pallas_call(
        kern,
        grid_spec=pltpu.PrefetchScalarGridSpec(
            num_scalar_prefetch=1,
            grid=(N_EXPERTS, nc // sub),
            in_specs=[
                pl.BlockSpec((N_EXPERTS, ts), lambda e, s, base: (0, s)),
                pl.BlockSpec((ts, D_MODEL), lambda e, s, base: (s, 0)),
            ],
            out_specs=pl.BlockSpec((1, cp, D_MODEL), lambda e, s, base: (e, 0, 0)),
        ),
        out_shape=jax.ShapeDtypeStruct((N_EXPERTS, cp, D_MODEL), BF16),
        compiler_params=_cparams(("parallel", "arbitrary")),
        name="gather",
    )(base_flat, pos, m)


def _ffn_kernel(x_ref, wg_ref, wu_ref, wd_ref, y_ref, acc_ref, *, rows):
    f = pl.program_id(1)
    nf = pl.num_programs(1)

    @pl.when(f == 0)
    def _():
        acc_ref[...] = jnp.zeros_like(acc_ref)

    x = x_ref[0, :rows, :]
    g = jnp.dot(x, wg_ref[0].astype(BF16), preferred_element_type=F32)
    u = jnp.dot(x, wu_ref[0].astype(BF16), preferred_element_type=F32)
    hid = g * jax.nn.sigmoid(g) * u
    valid = D_FF - f * FF_TILE
    col = lax.broadcasted_iota(jnp.int32, (1, FF_TILE), 1)
    hid = jnp.where(col < valid, hid, 0.0).astype(BF16)
    wrow = lax.broadcasted_iota(jnp.int32, (FF_TILE, 1), 0)
    wd = jnp.where(wrow < valid, wd_ref[0], 0.0).astype(BF16)
    acc_ref[...] += jnp.dot(hid, wd, preferred_element_type=F32)

    @pl.when(f == nf - 1)
    def _():
        y_ref[0, :rows, :] = acc_ref[...].astype(BF16)
        y_ref[0, rows:, :] = jnp.zeros((y_ref.shape[1] - rows, D_MODEL), BF16)


def _ffn(xc, w_gate, w_up, w_down, rows):
    cp = xc.shape[1]
    nf = pl.cdiv(D_FF, FF_TILE)
    kern = functools.partial(_ffn_kernel, rows=rows)
    return pl.pallas_call(
        kern,
        grid=(N_EXPERTS, nf),
        in_specs=[
            pl.BlockSpec((1, cp, D_MODEL), lambda e, f: (e, 0, 0)),
            pl.BlockSpec((1, D_MODEL, FF_TILE), lambda e, f: (e, 0, f)),
            pl.BlockSpec((1, D_MODEL, FF_TILE), lambda e, f: (e, 0, f)),
            pl.BlockSpec((1, FF_TILE, D_MODEL), lambda e, f: (e, f, 0)),
        ],
        out_specs=pl.BlockSpec((1, cp, D_MODEL), lambda e, f: (e, 0, 0)),
        out_shape=jax.ShapeDtypeStruct((N_EXPERTS, cp, D_MODEL), BF16),
        scratch_shapes=[pltpu.VMEM((rows, D_MODEL), F32)],
        compiler_params=_cparams(("parallel", "arbitrary")),
        name="ffn",
    )(xc, w_gate, w_up, w_down)


def _combine_kernel(base_ref, h2_ref, post_ref, gatet_ref, g_ref, yc_ref, o_ref,
                    win_ref, carry_ref, sem_ref, *, nc, cpb, out_blocks):
    E = N_EXPERTS
    b = pl.program_id(0)
    c = pl.program_id(1)
    step = b * cpb + c
    nsteps = pl.num_programs(0) * cpb
    slot = step % 2

    def window_copy(chunk, e, sl):
        b0 = base_ref[e * nc + chunk]
        a = pl.multiple_of((b0 // BF16_ROWS) * BF16_ROWS, BF16_ROWS)
        return pltpu.make_async_copy(yc_ref.at[e, pl.ds(a, WIN_ROWS), :], win_ref.at[sl, e], sem_ref.at[sl, e])

    @pl.when(step == 0)
    def _():
        for e in range(E):
            window_copy(step, e, slot).start()

    @pl.when(step + 1 < nsteps)
    def _():
        for e in range(E):
            window_copy(step + 1, e, 1 - slot).start()

    acc = h2_ref[...]
    col = lax.broadcasted_iota(jnp.int32, (CHUNK, WIN_ROWS), 1).astype(F32)
    for e in range(E):
        b0 = base_ref[e * nc + step]
        a = ((b0 // BF16_ROWS) * BF16_ROWS).astype(F32)
        window_copy(step, e, slot).wait()
        onehot = jnp.where(col == post_ref[:, e:e + 1] - a, 1.0, 0.0).astype(BF16)
        contrib = jnp.dot(onehot, win_ref[slot, e], preferred_element_type=F32)
        acc = acc + gatet_ref[:, e:e + 1] * contrib

    ms = jnp.mean(acc * acc, axis=-1, keepdims=True)
    y = acc * lax.rsqrt(ms + RMS_EPS) * g_ref[...]

    @pl.when((c >= 1) & (c <= out_blocks))
    def _():
        o_ref[0, :CHUNK - N_META, :] = carry_ref[N_META:, :]
        o_ref[0, CHUNK - N_META:, :] = y[:N_META, :]

    carry_ref[...] = y


def _combine(base_flat, h2, post, gatet, g, yc, batch, P, S):
    nt = h2.shape[0]
    nc = nt // CHUNK
    cpb = P // CHUNK
    out_blocks = S // CHUNK
    kern = functools.partial(_combine_kernel, nc=nc, cpb=cpb, out_blocks=out_blocks)
    return pl.pallas_call(
        kern,
        grid_spec=pltpu.PrefetchScalarGridSpec(
            num_scalar_prefetch=1,
            grid=(batch, cpb),
            in_specs=[
                pl.BlockSpec((CHUNK, D_MODEL), lambda b, c, base: (b * cpb + c, 0)),
                pl.BlockSpec((CHUNK, N_EXPERTS), lambda b, c, base: (b * cpb + c, 0)),
                pl.BlockSpec((CHUNK, N_EXPERTS), lambda b, c, base: (b * cpb + c, 0)),
                pl.BlockSpec((1, D_MODEL), lambda b, c, base: (0, 0)),
                pl.BlockSpec(memory_space=pl.ANY),
            ],
            out_specs=pl.BlockSpec(
                (1, CHUNK, D_MODEL), lambda b, c, base: (b, jnp.clip(c - 1, 0, out_blocks - 1), 0)),
            scratch_shapes=[
                pltpu.VMEM((2, N_EXPERTS, WIN_ROWS, D_MODEL), BF16),
                pltpu.VMEM((CHUNK, D_MODEL), F32),
                pltpu.SemaphoreType.DMA((2, N_EXPERTS)),
            ],
        ),
        out_shape=jax.ShapeDtypeStruct((batch, S, D_MODEL), F32),
        compiler_params=_cparams(("arbitrary", "arbitrary")),
        name="combine",
    )(base_flat, h2, post, gatet, g, yc)


def _swa_head_perm():
    perm = np.arange(SWA_WIDTH).reshape(SWA_KV_HEADS, SWA_GROUP, HEAD_DIM)
    return perm.transpose(1, 0, 2).reshape(-1)


def _prep_params(rel_bias, w_in, w_out):
    perm = _swa_head_perm()
    scale = np.ones((IN_COLS,), np.float32)
    scale[:DIFF_WIDTH] = HEAD_DIM ** -0.5
    scale[3 * DIFF_WIDTH:3 * DIFF_WIDTH + SWA_WIDTH] = HEAD_DIM ** -0.5
    cols = np.arange(IN_COLS)
    cols[3 * DIFF_WIDTH:3 * DIFF_WIDTH + SWA_WIDTH] = 3 * DIFF_WIDTH + perm
    w_in_p = (w_in * scale)[:, cols].astype(BF16)
    rows = np.arange(MIX_WIDTH)
    rows[DIFF_WIDTH:] = DIFF_WIDTH + perm
    w_out_p = w_out[rows, :].astype(BF16)

    T = SEQ_TILE
    table = rel_bias.astype(F32)
    kk = jnp.arange(T)[:, None]
    qq = jnp.arange(T)[None, :]
    idx = jnp.stack([_rel_bucket(d * T + kk - qq) for d in (-1, 0, 1)])
    diff_tiles = table[:, :2 * DIFF_HEADS][idx]
    diff_tiles = diff_tiles.transpose(3, 0, 1, 2).reshape(DIFF_HEADS, 2, 3, T, T)
    half = N_BUCKETS // 2
    diff_consts = jnp.stack([table[half - 1, 0:2 * DIFF_HEADS:2], table[N_BUCKETS - 1, 0:2 * DIFF_HEADS:2],
                             table[half - 1, 1:2 * DIFF_HEADS:2], table[N_BUCKETS - 1, 1:2 * DIFF_HEADS:2]],
                            axis=1)
    rel = jnp.arange(SWA_KEYS + T)[None, :] - T - jnp.arange(T)[:, None]
    swa_tiles = table[:, 2 * DIFF_HEADS:][_rel_bucket(rel)]
    swa_tiles = jnp.where((jnp.abs(rel) <= WINDOW)[:, :, None], swa_tiles, NEG).transpose(2, 0, 1)
    return w_in_p, w_out_p, diff_tiles, diff_consts, swa_tiles


def _trunk(x, meta_tokens, prep, attn_norm_g, lamv, subln_g, swa_sink, ffn_norm_g, w_router_t,
           w_gate, w_up, w_down, final_norm_g):
    w_in_p, w_out_p, diff_tiles, diff_consts, swa_tiles = prep
    B, S, _ = x.shape
    L = S + N_META
    P = -(-L // SEQ_TILE) * SEQ_TILE
    rows = B * P
    capacity = EC_FACTOR * (B * L) // N_EXPERTS
    ffn_rows = -(-capacity // BF16_ROWS) * BF16_ROWS
    cp = -(-(capacity + WIN_ROWS) // BF16_ROWS) * BF16_ROWS

    h0 = jnp.concatenate([jnp.broadcast_to(meta_tokens.astype(x.dtype)[None], (B, N_META, D_MODEL)), x,
                          jnp.zeros((B, P - L, D_MODEL), x.dtype)], axis=1).reshape(rows, D_MODEL)

    tm = SEQ_TILE * _largest_divisor(rows // SEQ_TILE, 2)
    proj, vt = _proj(h0, attn_norm_g.reshape(1, D_MODEL), w_in_p, tm)
    diff_o = _diff_attention(proj, vt, diff_consts, lamv, diff_tiles, subln_g.reshape(LANES, 1), B, P, L)
    swa_o = _swa_attention(proj, swa_sink, swa_tiles, B, P, L)
    h2, m, probs3 = _out_router(diff_o, swa_o, h0, w_out_p, ffn_norm_g.reshape(1, D_MODEL), w_router_t, B, P, L)
    pos, post, gatet, base3 = _topk(probs3, capacity)
    base_flat = base3[:, :, 0].T.reshape(-1)
    xc = _gather(base_flat, pos, m, cp)
    yc = _ffn(xc, w_gate, w_up, w_down, ffn_rows)
    return _combine(base_flat, h2, post, gatet, final_norm_g.reshape(1, D_MODEL), yc, B, P, S)


def kernel(x_prompt, x_sample, meta_tokens, rel_bias, attn_norm_g, w_in, diff_lambda_q1, diff_lambda_k1,
           diff_lambda_q2, diff_lambda_k2, diff_subln_g, swa_sink, w_out, ffn_norm_g, w_router, w_gate, w_up,
           w_down, final_norm_g):
    prep = _prep_params(rel_bias, w_in[0], w_out[0])
    lamv = jnp.stack([diff_lambda_q1[0], diff_lambda_k1[0], diff_lambda_q2[0], diff_lambda_k2[0]]).astype(F32)
    args = (meta_tokens, prep, attn_norm_g[0], lamv, diff_subln_g[0], swa_sink[0].astype(F32), ffn_norm_g[0],
            w_router[0].T, w_gate[0], w_up[0], w_down[0], final_norm_g)
    return (_trunk(x_prompt, *args), _trunk(x_sample, *args))
```

```python
import functools
import math

import jax
import jax.numpy as jnp
import numpy as np
from jax import lax
from jax.experimental import pallas as pl
from jax.experimental.pallas import tpu as pltpu

F32 = jnp.float32
BF16 = jnp.bfloat16

D_MODEL = 1024
HEAD_DIM = 64
N_META = 16
DIFF_HEADS = 4
DIFF_WIDTH = DIFF_HEADS * 2 * HEAD_DIM
SWA_Q_HEADS = 8
SWA_KV_HEADS = 2
SWA_GROUP = SWA_Q_HEADS // SWA_KV_HEADS
SWA_WIDTH = SWA_Q_HEADS * HEAD_DIM
SWA_KV_WIDTH = SWA_KV_HEADS * HEAD_DIM
MIX_WIDTH = DIFF_WIDTH + SWA_WIDTH
IN_COLS = 3 * DIFF_WIDTH + SWA_WIDTH + 2 * SWA_KV_WIDTH
WINDOW = 128
N_BUCKETS = 32
MAX_DISTANCE = 128
N_EXPERTS = 16
EC_FACTOR = 2
D_FF = 2752
RMS_EPS = 1e-6
LAM_INIT = 0.8 - 0.6 * math.exp(-0.3 * 0)

LANES = 128
BF16_ROWS = 16
VMEM_LIMIT = 56 * 1024 * 1024

SEQ_TILE = 256
SWA_KEYS = SEQ_TILE + 2 * WINDOW
FF_TILE = 256
CHUNK = LANES
WIN_ROWS = CHUNK + BF16_ROWS
NEG = -1e30

_NT = (((1,), (1,)), ((), ()))


def _cparams(sem):
    return pltpu.CompilerParams(dimension_semantics=sem, vmem_limit_bytes=VMEM_LIMIT)


def _largest_divisor(n, cap):
    return max(d for d in range(1, cap + 1) if n % d == 0)


def _rel_bucket(rel):
    half = N_BUCKETS // 2
    max_exact = half // 2
    n = jnp.abs(rel)
    large = max_exact + (jnp.log(jnp.maximum(n, 1).astype(F32) / max_exact)
                         / math.log(MAX_DISTANCE / max_exact) * (half - max_exact)).astype(jnp.int32)
    large = jnp.minimum(large, half - 1)
    return jnp.where(rel > 0, half, 0) + jnp.where(n < max_exact, n, large)


def _proj_kernel(h_ref, g_ref, w_ref, proj_ref, vt_ref):
    x = h_ref[...]
    ms = jnp.mean(x * x, axis=-1, keepdims=True)
    a = (x * lax.rsqrt(ms + RMS_EPS) * g_ref[...]).astype(BF16)
    proj = jnp.dot(a, w_ref[...], preferred_element_type=F32)
    proj_ref[...] = proj.astype(BF16)
    for h in range(DIFF_HEADS):
        v = proj[:, 2 * DIFF_WIDTH + h * LANES: 2 * DIFF_WIDTH + (h + 1) * LANES]
        vt_ref[h] = v.T.astype(BF16)


def _proj(h0, g, w_in_bf16, tm):
    rows = h0.shape[0]
    return pl.pallas_call(
        _proj_kernel,
        grid=(rows // tm,),
        in_specs=[
            pl.BlockSpec((tm, D_MODEL), lambda i: (i, 0)),
            pl.BlockSpec((1, D_MODEL), lambda i: (0, 0)),
            pl.BlockSpec((D_MODEL, IN_COLS), lambda i: (0, 0)),
        ],
        out_specs=[
            pl.BlockSpec((tm, IN_COLS), lambda i: (i, 0)),
            pl.BlockSpec((DIFF_HEADS, LANES, tm), lambda i: (0, 0, i)),
        ],
        out_shape=[
            jax.ShapeDtypeStruct((rows, IN_COLS), BF16),
            jax.ShapeDtypeStruct((DIFF_HEADS, LANES, rows), BF16),
        ],
        compiler_params=_cparams(("parallel",)),
        name="proj",
    )(h0, g, w_in_bf16)


def _diff_kernel(consts_ref, lamv_ref, q_ref, k_ref, vt_ref, bias_ref, g_ref, o_ref,
                 acc_ref, m_ref, l_ref, *, seq_len, nkv, nsub):
    T = SEQ_TILE
    h = pl.program_id(1)
    i = pl.program_id(2)
    q = q_ref[...]
    lane = lax.broadcasted_iota(jnp.int32, q.shape, 1)
    zero = jnp.zeros_like(q)
    q_maps = (jnp.where(lane < HEAD_DIM, q, zero), jnp.where(lane >= HEAD_DIM, q, zero))

    acc_ref[...] = jnp.zeros_like(acc_ref)
    m_ref[...] = jnp.full_like(m_ref, NEG)
    l_ref[...] = jnp.zeros_like(l_ref)

    def chunk(j, *, tile, side, mask):
        off = pl.multiple_of(j * T, T)
        kb = k_ref[pl.ds(off, T), :]
        vb = vt_ref[0, :, pl.ds(off, T)]
        if mask:
            kpos = off + lax.broadcasted_iota(jnp.int32, (T, 1), 0)
            kvalid = kpos < seq_len
        for mp in range(2):
            s = lax.dot_general(kb, q_maps[mp], _NT, preferred_element_type=F32)
            if tile:
                s = jnp.concatenate(
                    [s[:, u * T:(u + 1) * T] + bias_ref[0, mp, jnp.clip(j - (i * nsub + u), -2, 2) + 2]
                     for u in range(nsub)], axis=1)
                c = 0.0
            else:
                c = consts_ref[h, 2 * mp + side]
            if mask:
                s = jnp.where(kvalid, s, NEG)
            m_old = m_ref[mp]
            m_new = jnp.maximum(m_old, jnp.max(s, axis=0, keepdims=True) + c)
            alpha = jnp.exp(m_old - m_new)
            p = jnp.exp(s - (m_new - c))
            l_ref[mp] = alpha * l_ref[mp] + jnp.sum(p, axis=0, keepdims=True)
            acc_ref[mp] = alpha * acc_ref[mp] + jnp.dot(vb, p.astype(BF16), preferred_element_type=F32)
            m_ref[mp] = m_new

    def loop(lo, hi, **kw):
        def body(j, carry):
            chunk(j, **kw)
            return carry
        lax.fori_loop(lo, hi, body, 0)

    near_lo = jnp.maximum(i * nsub - 1, 0)
    near_hi = jnp.minimum((i + 1) * nsub + 1, nkv)
    loop(0, near_lo, tile=False, side=0, mask=False)
    loop(near_lo, near_hi, tile=True, side=0, mask=True)
    loop(near_hi, nkv - 1, tile=False, side=1, mask=False)
    loop(jnp.maximum(near_hi, nkv - 1), nkv, tile=False, side=1, mask=True)

    lamv = lamv_ref[...]
    lam = (jnp.exp(jnp.sum(lamv[0:1] * lamv[1:2], axis=-1, keepdims=True))
           - jnp.exp(jnp.sum(lamv[2:3] * lamv[3:4], axis=-1, keepdims=True)) + LAM_INIT)
    o = acc_ref[0] / l_ref[0] - lam * (acc_ref[1] / l_ref[1])
    ms = jnp.mean(o * o, axis=0, keepdims=True)
    y = o * lax.rsqrt(ms + RMS_EPS) * g_ref[...] * (1.0 - LAM_INIT)
    o_ref[...] = y.T.astype(BF16)


def _diff_attention(proj, vt, consts, lamv, bias_t, subln_g, batch, P, seq_len, nsub):
    T = SEQ_TILE
    tq = nsub * T
    nq = P // tq
    nkv = pl.cdiv(seq_len, T)
    kern = functools.partial(_diff_kernel, seq_len=seq_len, nkv=nkv, nsub=nsub)
    return pl.pallas_call(
        kern,
        grid=(batch, DIFF_HEADS, nq),
        in_specs=[
            pl.BlockSpec(memory_space=pltpu.SMEM),
            pl.BlockSpec((4, HEAD_DIM), lambda b, h, i: (0, 0)),
            pl.BlockSpec((tq, LANES), lambda b, h, i: (b * nq + i, h)),
            pl.BlockSpec((P, LANES), lambda b, h, i: (b, DIFF_HEADS + h)),
            pl.BlockSpec((1, LANES, P), lambda b, h, i: (h, 0, b)),
            pl.BlockSpec((1, 2, 5, T, T), lambda b, h, i: (h, 0, 0, 0, 0)),
            pl.BlockSpec((LANES, 1), lambda b, h, i: (0, 0)),
        ],
        out_specs=pl.BlockSpec((tq, LANES), lambda b, h, i: (b * nq + i, h)),
        out_shape=jax.ShapeDtypeStruct((batch * P, DIFF_WIDTH), BF16),
        scratch_shapes=[
            pltpu.VMEM((2, LANES, tq), F32),
            pltpu.VMEM((2, 1, tq), F32),
            pltpu.VMEM((2, 1, tq), F32),
        ],
        compiler_params=_cparams(("parallel", "parallel", "parallel")),
        name="diff_attn",
    )(consts, lamv, proj, proj, vt, bias_t, subln_g)


def _swa_kernel(sink_ref, q_ref, k_ref, v_ref, bias_ref, o_ref, *, seq_len, P):
    T = SEQ_TILE
    i = pl.program_id(1)
    start = pl.multiple_of(jnp.clip(i * T - WINDOW, 0, P - SWA_KEYS), LANES)
    boff = pl.multiple_of(start - i * T + T, LANES)
    kw = k_ref[pl.ds(start, SWA_KEYS), :]
    vw = v_ref[pl.ds(start, SWA_KEYS), :]
    kvalid = (start + lax.broadcasted_iota(jnp.int32, (1, SWA_KEYS), 1)) < seq_len
    lane = lax.broadcasted_iota(jnp.int32, (T, LANES), 1)
    for j in range(SWA_GROUP):
        qp = q_ref[:, j * LANES:(j + 1) * LANES]
        zero = jnp.zeros_like(qp)
        outs = []
        for kvh in range(SWA_KV_HEADS):
            head = kvh * SWA_GROUP + j
            in_half = (lane < HEAD_DIM) if kvh == 0 else (lane >= HEAD_DIM)
            qe = jnp.where(in_half, qp, zero)
            s = lax.dot_general(qe, kw, _NT, preferred_element_type=F32)
            s = s + bias_ref[head, :, pl.ds(boff, SWA_KEYS)]
            s = jnp.where(kvalid, s, NEG)
            sink = sink_ref[head]
            m = jnp.maximum(jnp.max(s, axis=-1, keepdims=True), sink)
            p = jnp.exp(s - m)
            l = jnp.sum(p, axis=-1, keepdims=True) + jnp.exp(sink - m)
            outs.append(jnp.dot(p.astype(BF16), vw, preferred_element_type=F32) / l)
        o_ref[:, j * LANES:(j + 1) * LANES] = jnp.where(lane < HEAD_DIM, outs[0], outs[1]).astype(BF16)


def _swa_attention(proj, sink, bias_w, batch, P, seq_len):
    T = SEQ_TILE
    nq = P // T
    q_blk = (3 * DIFF_WIDTH) // SWA_WIDTH
    k_blk = (3 * DIFF_WIDTH + SWA_WIDTH) // LANES
    kern = functools.partial(_swa_kernel, seq_len=seq_len, P=P)
    return pl.pallas_call(
        kern,
        grid=(batch, nq),
        in_specs=[
            pl.BlockSpec(memory_space=pltpu.SMEM),
            pl.BlockSpec((T, SWA_WIDTH), lambda b, i: (b * nq + i, q_blk)),
            pl.BlockSpec((P, LANES), lambda b, i: (b, k_blk)),
            pl.BlockSpec((P, LANES), lambda b, i: (b, k_blk + 1)),
            pl.BlockSpec((SWA_Q_HEADS, T, SWA_KEYS + T), lambda b, i: (0, 0, 0)),
        ],
        out_specs=pl.BlockSpec((T, SWA_WIDTH), lambda b, i: (b * nq + i, 0)),
        out_shape=jax.ShapeDtypeStruct((batch * P, SWA_WIDTH), BF16),
        compiler_params=_cparams(("parallel", "parallel")),
        name="swa_attn",
    )(sink, proj, proj, proj, bias_w)


def _out_kernel(do_ref, so_ref, h_ref, w_ref, g_ref, wr_ref, h2_ref, m_ref, p3_ref, *, seq_len, tm):
    i = pl.program_id(1)
    h2 = (h_ref[...]
          + jnp.dot(do_ref[...], w_ref[:DIFF_WIDTH, :], preferred_element_type=F32)
          + jnp.dot(so_ref[...], w_ref[DIFF_WIDTH:, :], preferred_element_type=F32))
    h2_ref[...] = h2
    ms = jnp.mean(h2 * h2, axis=-1, keepdims=True)
    mf = h2 * lax.rsqrt(ms + RMS_EPS) * g_ref[...]
    m_ref[...] = mf.astype(BF16)
    logits = lax.dot_general(wr_ref[...], mf, _NT, preferred_element_type=F32,
                             precision=lax.Precision.HIGHEST)
    e = jnp.exp(logits - jnp.max(logits, axis=0, keepdims=True))
    probs = e / jnp.sum(e, axis=0, keepdims=True)
    pos = i * tm + lax.broadcasted_iota(jnp.int32, (1, tm), 1)
    probs = jnp.where(pos < seq_len, probs, -1.0)
    for c in range(tm // CHUNK):
        p3_ref[c] = probs[:, c * CHUNK:(c + 1) * CHUNK]


def _out_router(diff_o, swa_o, h0, w_out_bf16, g, w_router_t, batch, P, seq_len):
    tm = SEQ_TILE
    nq = P // tm
    rows = batch * P
    kern = functools.partial(_out_kernel, seq_len=seq_len, tm=tm)
    return pl.pallas_call(
        kern,
        grid=(batch, nq),
        in_specs=[
            pl.BlockSpec((tm, DIFF_WIDTH), lambda b, i: (b * nq + i, 0)),
            pl.BlockSpec((tm, SWA_WIDTH), lambda b, i: (b * nq + i, 0)),
            pl.BlockSpec((tm, D_MODEL), lambda b, i: (b * nq + i, 0)),
            pl.BlockSpec((MIX_WIDTH, D_MODEL), lambda b, i: (0, 0)),
            pl.BlockSpec((1, D_MODEL), lambda b, i: (0, 0)),
            pl.BlockSpec((N_EXPERTS, D_MODEL), lambda b, i: (0, 0)),
        ],
        out_specs=[
            pl.BlockSpec((tm, D_MODEL), lambda b, i: (b * nq + i, 0)),
            pl.BlockSpec((tm, D_MODEL), lambda b, i: (b * nq + i, 0)),
            pl.BlockSpec((tm // CHUNK, N_EXPERTS, CHUNK), lambda b, i: (b * nq + i, 0, 0)),
        ],
        out_shape=[
            jax.ShapeDtypeStruct((rows, D_MODEL), F32),
            jax.ShapeDtypeStruct((rows, D_MODEL), BF16),
            jax.ShapeDtypeStruct((rows // CHUNK, N_EXPERTS, CHUNK), F32),
        ],
        compiler_params=_cparams(("parallel", "parallel")),
        name="out_router",
    )(diff_o, swa_o, h0, w_out_bf16, g, w_router_t)


def _topk_kernel(p3_ref, pos_ref, post_ref, gatet_ref, base_ref, incl_ref, flag_ref, *, capacity, nc):
    E = N_EXPERTS
    probs = p3_ref[...]
    bits = pltpu.bitcast(probs, jnp.int32)

    def count(pred):
        part = jnp.sum(jnp.where(pred, 1.0, 0.0), axis=0, keepdims=True)
        return jnp.broadcast_to(jnp.sum(part, axis=-1, keepdims=True), part.shape)

    def bisect(_, lohi):
        lo, hi = lohi
        mid = lo + ((hi - lo + 1) >> 1)
        ok = count(bits >= mid) >= capacity
        return jnp.where(ok, mid, lo), jnp.where(ok, hi, mid - 1)

    lo0 = jnp.zeros((1, E, CHUNK), jnp.int32)
    hi0 = jnp.full((1, E, CHUNK), 0x7F800000, jnp.int32)
    thr, _ = lax.fori_loop(0, 32, bisect, (lo0, hi0))
    gt = bits > thr
    eq = bits == thr
    need = capacity - count(gt)[0]

    tri = (lax.broadcasted_iota(jnp.int32, (CHUNK, CHUNK), 0)
           <= lax.broadcasted_iota(jnp.int32, (CHUNK, CHUNK), 1)).astype(BF16)

    def inclusive_prefix(flags):
        f2 = flags.astype(BF16).reshape(nc * E, CHUNK)
        return jnp.dot(f2, tri, preferred_element_type=F32).reshape(nc, E, CHUNK)

    eqf = jnp.where(eq, 1.0, 0.0)
    incl_ref[...] = inclusive_prefix(eqf)
    flag_ref[...] = eqf

    def tie_scan(c, run):
        inc = incl_ref[c]
        e_c = flag_ref[c]
        take = (e_c > 0.0) & ((run + inc - e_c) < need)
        flag_ref[c] = jnp.where(take, 1.0, 0.0)
        return run + jnp.broadcast_to(inc[:, CHUNK - 1:CHUNK], inc.shape)

    lax.fori_loop(0, nc, tie_scan, jnp.zeros((E, CHUNK), F32))
    self = jnp.where(gt, 1.0, flag_ref[...])
    flag_ref[...] = self
    incl_ref[...] = inclusive_prefix(self)

    zpad = jnp.zeros((CHUNK - E, CHUNK), F32)

    def to_token_major(x):
        return jnp.concatenate([x, zpad], axis=0).T[:, :E]

    def pos_scan(c, run):
        inc = incl_ref[c]
        s_c = flag_ref[c]
        sel = s_c > 0.0
        pos = jnp.where(sel, run + inc - s_c, -1.0)
        gate = jnp.where(sel, p3_ref[c], 0.0)
        off = pl.multiple_of(c * CHUNK, CHUNK)
        pos_ref[:, pl.ds(off, CHUNK)] = pos
        post_ref[pl.ds(off, CHUNK), :] = to_token_major(pos)
        gatet_ref[pl.ds(off, CHUNK), :] = to_token_major(gate)
        base_ref[c] = run.astype(jnp.int32)
        return run + jnp.broadcast_to(inc[:, CHUNK - 1:CHUNK], inc.shape)

    lax.fori_loop(0, nc, pos_scan, jnp.zeros((E, CHUNK), F32))


def _topk(probs3, capacity):
    nc = probs3.shape[0]
    nt = nc * CHUNK
    kern = functools.partial(_topk_kernel, capacity=capacity, nc=nc)
    return pl.pallas_call(
        kern,
        out_shape=[
            jax.ShapeDtypeStruct((N_EXPERTS, nt), F32),
            jax.ShapeDtypeStruct((nt, N_EXPERTS), F32),
            jax.ShapeDtypeStruct((nt, N_EXPERTS), F32),
            jax.ShapeDtypeStruct((nc, N_EXPERTS, CHUNK), jnp.int32),
        ],
        scratch_shapes=[
            pltpu.VMEM((nc, N_EXPERTS, CHUNK), F32),
            pltpu.VMEM((nc, N_EXPERTS, CHUNK), F32),
        ],
        compiler_params=pltpu.CompilerParams(vmem_limit_bytes=VMEM_LIMIT),
        name="topk",
    )(probs3)


def _gather_kernel(base_ref, pos_ref, m_ref, xc_ref, *, nc, sub):
    e = pl.program_id(0)
    sb = pl.program_id(1)

    @pl.when(sb == 0)
    def _():
        xc_ref[...] = jnp.zeros_like(xc_ref)

    row = lax.broadcasted_iota(jnp.int32, (WIN_ROWS, CHUNK), 0).astype(F32)
    erow = lax.broadcasted_iota(jnp.int32, (N_EXPERTS, CHUNK), 0)

    def body(u, carry):
        b0 = base_ref[e * nc + sb * sub + u]
        a = pl.multiple_of((b0 // BF16_ROWS) * BF16_ROWS, BF16_ROWS)
        off = pl.multiple_of(u * CHUNK, CHUNK)
        pblk = pos_ref[:, pl.ds(off, CHUNK)]
        prow = jnp.sum(jnp.where(erow == e, pblk, 0.0), axis=0, keepdims=True)
        onehot = jnp.where(row == prow - a.astype(F32), 1.0, 0.0).astype(BF16)
        x = m_ref[pl.ds(off, CHUNK), :]
        g = jnp.dot(onehot, x, preferred_element_type=F32)
        xc_ref[0, pl.ds(a, WIN_ROWS), :] += g.astype(BF16)
        return carry

    lax.fori_loop(0, sub, body, 0)


def _gather(base_flat, pos, m, cp):
    nt = pos.shape[1]
    nc = nt // CHUNK
    sub = _largest_divisor(nc, 17)
    ts = sub * CHUNK
    kern = functools.partial(_gather_kernel, nc=nc, sub=sub)
    return pl.pallas_call(
        kern,
        grid_spec=pltpu.PrefetchScalarGridSpec(
            num_scalar_prefetch=1,
            grid=(N_EXPERTS, nc // sub),
            in_specs=[
                pl.BlockSpec((N_EXPERTS, ts), lambda e, s, base: (0, s)),
                pl.BlockSpec((ts, D_MODEL), lambda e, s, base: (s, 0)),
            ],
            out_specs=pl.BlockSpec((1, cp, D_MODEL), lambda e, s, base: (e, 0, 0)),
        ),
        out_shape=jax.ShapeDtypeStruct((N_EXPERTS, cp, D_MODEL), BF16),
        compiler_params=_cparams(("parallel", "arbitrary")),
        name="gather",
    )(base_flat, pos, m)


def _ffn_kernel(x_ref, wg_ref, wu_ref, wd_ref, y_ref, acc_ref, *, rows):
    f = pl.program_id(1)
    nf = pl.num_programs(1)

    @pl.when(f == 0)
    def _():
        acc_ref[...] = jnp.zeros_like(acc_ref)

    x = x_ref[0, :rows, :]
    g = jnp.dot(x, wg_ref[0].astype(BF16), preferred_element_type=F32)
    u = jnp.dot(x, wu_ref[0].astype(BF16), preferred_element_type=F32)
    hid = g * jax.nn.sigmoid(g) * u
    valid = D_FF - f * FF_TILE
    col = lax.broadcasted_iota(jnp.int32, (1, FF_TILE), 1)
    hid = jnp.where(col < valid, hid, 0.0).astype(BF16)
    wrow = lax.broadcasted_iota(jnp.int32, (FF_TILE, 1), 0)
    wd = jnp.where(wrow < valid, wd_ref[0], 0.0).astype(BF16)
    acc_ref[...] += jnp.dot(hid, wd, preferred_element_type=F32)

    @pl.when(f == nf - 1)
    def _():
        y_ref[0, :rows, :] = acc_ref[...].astype(BF16)
        y_ref[0, rows:, :] = jnp.zeros((y_ref.shape[1] - rows, D_MODEL), BF16)


def _ffn(xc, w_gate, w_up, w_down, rows):
    cp = xc.shape[1]
    nf = pl.cdiv(D_FF, FF_TILE)
    kern = functools.partial(_ffn_kernel, rows=rows)
    return pl.pallas_call(
        kern,
        grid=(N_EXPERTS, nf),
        in_specs=[
            pl.BlockSpec((1, cp, D_MODEL), lambda e, f: (e, 0, 0)),
            pl.BlockSpec((1, D_MODEL, FF_TILE), lambda e, f: (e, 0, f)),
            pl.BlockSpec((1, D_MODEL, FF_TILE), lambda e, f: (e, 0, f)),
            pl.BlockSpec((1, FF_TILE, D_MODEL), lambda e, f: (e, f, 0)),
        ],
        out_specs=pl.BlockSpec((1, cp, D_MODEL), lambda e, f: (e, 0, 0)),
        out_shape=jax.ShapeDtypeStruct((N_EXPERTS, cp, D_MODEL), BF16),
        scratch_shapes=[pltpu.VMEM((rows, D_MODEL), F32)],
        compiler_params=_cparams(("parallel", "arbitrary")),
        name="ffn",
    )(xc, w_gate, w_up, w_down)


def _combine_kernel(base_ref, h2_ref, post_ref, gatet_ref, g_ref, yc_ref, o_ref,
                    win_ref, carry_ref, sem_ref, *, nc, cpb, out_blocks):
    E = N_EXPERTS
    b = pl.program_id(0)
    c = pl.program_id(1)
    step = b * cpb + c
    nsteps = pl.num_programs(0) * cpb
    slot = step % 2

    def window_copy(chunk, e, sl):
        b0 = base_ref[e * nc + chunk]
        a = pl.multiple_of((b0 // BF16_ROWS) * BF16_ROWS, BF16_ROWS)
        return pltpu.make_async_copy(yc_ref.at[e, pl.ds(a, WIN_ROWS), :], win_ref.at[sl, e], sem_ref.at[sl, e])

    @pl.when(step == 0)
    def _():
        for e in range(E):
            window_copy(step, e, slot).start()

    @pl.when(step + 1 < nsteps)
    def _():
        for e in range(E):
            window_copy(step + 1, e, 1 - slot).start()

    acc = h2_ref[...]
    col = lax.broadcasted_iota(jnp.int32, (CHUNK, WIN_ROWS), 1).astype(F32)
    for e in range(E):
        b0 = base_ref[e * nc + step]
        a = ((b0 // BF16_ROWS) * BF16_ROWS).astype(F32)
        window_copy(step, e, slot).wait()
        onehot = jnp.where(col == post_ref[:, e:e + 1] - a, 1.0, 0.0).astype(BF16)
        contrib = jnp.dot(onehot, win_ref[slot, e], preferred_element_type=F32)
        acc = acc + gatet_ref[:, e:e + 1] * contrib

    ms = jnp.mean(acc * acc, axis=-1, keepdims=True)
    y = acc * lax.rsqrt(ms + RMS_EPS) * g_ref[...]

    @pl.when((c >= 1) & (c <= out_blocks))
    def _():
        o_ref[0, :CHUNK - N_META, :] = carry_ref[N_META:, :]
        o_ref[0, CHUNK - N_META:, :] = y[:N_META, :]

    carry_ref[...] = y


def _combine(base_flat, h2, post, gatet, g, yc, batch, P, S):
    nt = h2.shape[0]
    nc = nt // CHUNK
    cpb = P // CHUNK
    out_blocks = S // CHUNK
    kern = functools.partial(_combine_kernel, nc=nc, cpb=cpb, out_blocks=out_blocks)
    return pl.pallas_call(
        kern,
        grid_spec=pltpu.PrefetchScalarGridSpec(
            num_scalar_prefetch=1,
            grid=(batch, cpb),
            in_specs=[
                pl.BlockSpec((CHUNK, D_MODEL), lambda b, c, base: (b * cpb + c, 0)),
                pl.BlockSpec((CHUNK, N_EXPERTS), lambda b, c, base: (b * cpb + c, 0)),
                pl.BlockSpec((CHUNK, N_EXPERTS), lambda b, c, base: (b * cpb + c, 0)),
                pl.BlockSpec((1, D_MODEL), lambda b, c, base: (0, 0)),
                pl.BlockSpec(memory_space=pl.ANY),
            ],
            out_specs=pl.BlockSpec(
                (1, CHUNK, D_MODEL), lambda b, c, base: (b, jnp.clip(c - 1, 0, out_blocks - 1), 0)),
            scratch_shapes=[
                pltpu.VMEM((2, N_EXPERTS, WIN_ROWS, D_MODEL), BF16),
                pltpu.VMEM((CHUNK, D_MODEL), F32),
                pltpu.SemaphoreType.DMA((2, N_EXPERTS)),
            ],
        ),
        out_shape=jax.ShapeDtypeStruct((batch, S, D_MODEL), F32),
        compiler_params=_cparams(("arbitrary", "arbitrary")),
        name="combine",
    )(base_flat, h2, post, gatet, g, yc)


def _swa_head_perm():
    perm = np.arange(SWA_WIDTH).reshape(SWA_KV_HEADS, SWA_GROUP, HEAD_DIM)
    return perm.transpose(1, 0, 2).reshape(-1)


def _prep_params(rel_bias, w_in, w_out):
    perm = _swa_head_perm()
    scale = np.ones((IN_COLS,), np.float32)
    scale[:DIFF_WIDTH] = HEAD_DIM ** -0.5
    scale[3 * DIFF_WIDTH:3 * DIFF_WIDTH + SWA_WIDTH] = HEAD_DIM ** -0.5
    cols = np.arange(IN_COLS)
    cols[3 * DIFF_WIDTH:3 * DIFF_WIDTH + SWA_WIDTH] = 3 * DIFF_WIDTH + perm
    w_in_p = (w_in * scale)[:, cols].astype(BF16)
    rows = np.arange(MIX_WIDTH)
    rows[DIFF_WIDTH:] = DIFF_WIDTH + perm
    w_out_p = w_out[rows, :].astype(BF16)

    T = SEQ_TILE
    table = rel_bias.astype(F32)

    def lookup(idx, cols):
        onehot = (idx[..., None] == jnp.arange(N_BUCKETS)).astype(F32)
        return jnp.einsum("...b,bc->...c", onehot, table[:, cols], precision=lax.Precision.HIGHEST)

    kk = jnp.arange(T)[:, None]
    qq = jnp.arange(T)[None, :]
    idx = jnp.stack([_rel_bucket(d * T + kk - qq) for d in (-2, -1, 0, 1, 2)])
    diff_tiles = lookup(idx, slice(0, 2 * DIFF_HEADS))
    diff_tiles = diff_tiles.transpose(3, 0, 1, 2).reshape(DIFF_HEADS, 2, 5, T, T)
    half = N_BUCKETS // 2
    diff_consts = jnp.stack([table[half - 1, 0:2 * DIFF_HEADS:2], table[N_BUCKETS - 1, 0:2 * DIFF_HEADS:2],
                             table[half - 1, 1:2 * DIFF_HEADS:2], table[N_BUCKETS - 1, 1:2 * DIFF_HEADS:2]],
                            axis=1)
    rel = jnp.arange(SWA_KEYS + T)[None, :] - T - jnp.arange(T)[:, None]
    swa_tiles = lookup(_rel_bucket(rel), slice(2 * DIFF_HEADS, None))
    swa_tiles = jnp.where((jnp.abs(rel) <= WINDOW)[:, :, None], swa_tiles, NEG).transpose(2, 0, 1)
    return w_in_p, w_out_p, diff_tiles, diff_consts, swa_tiles


def _plan_rows(seq_len):
    nsub = min((4, 5, 6), key=lambda n: (-(-seq_len // (n * SEQ_TILE)) * n, -n))
    return nsub, -(-seq_len // (nsub * SEQ_TILE)) * nsub * SEQ_TILE


def _trunk(x, meta_tokens, prep, attn_norm_g, lamv, subln_g, swa_sink, ffn_norm_g, w_router_t,
           w_gate, w_up, w_down, final_norm_g):
    w_in_p, w_out_p, diff_tiles, diff_consts, swa_tiles = prep
    B, S, _ = x.shape
    L = S + N_META
    nsub, P = _plan_rows(L)
    rows = B * P
    capacity = EC_FACTOR * (B * L) // N_EXPERTS
    ffn_rows = -(-capacity // BF16_ROWS) * BF16_ROWS
    cp = -(-(capacity + WIN_ROWS) // BF16_ROWS) * BF16_ROWS

    h0 = jnp.concatenate([jnp.broadcast_to(meta_tokens.astype(x.dtype)[None], (B, N_META, D_MODEL)), x,
                          jnp.zeros((B, P - L, D_MODEL), x.dtype)], axis=1).reshape(rows, D_MODEL)

    tm = SEQ_TILE * _largest_divisor(rows // SEQ_TILE, 2)
    proj, vt = _proj(h0, attn_norm_g.reshape(1, D_MODEL), w_in_p, tm)
    diff_o = _diff_attention(proj, vt, diff_consts, lamv, diff_tiles, subln_g.reshape(LANES, 1), B, P, L, nsub)
    swa_o = _swa_attention(proj, swa_sink, swa_tiles, B, P, L)
    h2, m, probs3 = _out_router(diff_o, swa_o, h0, w_out_p, ffn_norm_g.reshape(1, D_MODEL), w_router_t, B, P, L)
    pos, post, gatet, base3 = _topk(probs3, capacity)
    base_flat = base3[:, :, 0].T.reshape(-1)
    xc = _gather(base_flat, pos, m, cp)
    yc = _ffn(xc, w_gate, w_up, w_down, ffn_rows)
    return _combine(base_flat, h2, post, gatet, final_norm_g.reshape(1, D_MODEL), yc, B, P, S)


def kernel(x_prompt, x_sample, meta_tokens, rel_bias, attn_norm_g, w_in, diff_lambda_q1, diff_lambda_k1,
           diff_lambda_q2, diff_lambda_k2, diff_subln_g, swa_sink, w_out, ffn_norm_g, w_router, w_gate, w_up,
           w_down, final_norm_g):
    prep = _prep_params(rel_bias, w_in[0], w_out[0])
    lamv = jnp.stack([diff_lambda_q1[0], diff_lambda_k1[0], diff_lambda_q2[0], diff_lambda_k2[0]]).astype(F32)
    args = (meta_tokens, prep, attn_norm_g[0], lamv, diff_subln_g[0], swa_sink[0].astype(F32), ffn_norm_g[0],
            w_router[0].T, w_gate[0], w_up[0], w_down[0], final_norm_g)
    return (_trunk(x_prompt, *args), _trunk(x_sample, *args))
```

```python
import functools
import math

import jax
import jax.numpy as jnp
import numpy as np
from jax import lax
from jax.experimental import pallas as pl
from jax.experimental.pallas import tpu as pltpu

F32 = jnp.float32
BF16 = jnp.bfloat16

D_MODEL = 1024
HEAD_DIM = 64
N_META = 16
DIFF_HEADS = 4
DIFF_WIDTH = DIFF_HEADS * 2 * HEAD_DIM
SWA_Q_HEADS = 8
SWA_KV_HEADS = 2
SWA_GROUP = SWA_Q_HEADS // SWA_KV_HEADS
SWA_WIDTH = SWA_Q_HEADS * HEAD_DIM
SWA_KV_WIDTH = SWA_KV_HEADS * HEAD_DIM
MIX_WIDTH = DIFF_WIDTH + SWA_WIDTH
IN_COLS = 3 * DIFF_WIDTH + SWA_WIDTH + 2 * SWA_KV_WIDTH
WINDOW = 128
N_BUCKETS = 32
MAX_DISTANCE = 128
N_EXPERTS = 16
EC_FACTOR = 2
D_FF = 2752
RMS_EPS = 1e-6
LAM_INIT = 0.8 - 0.6 * math.exp(-0.3 * 0)

LANES = 128
BF16_ROWS = 16
VMEM_LIMIT = 56 * 1024 * 1024

SEQ_TILE = 256
SWA_KEYS = SEQ_TILE + 2 * WINDOW
FAR_GROUP = 2
FF_TILE = 256
CHUNK = LANES
VT_ROWS = LANES + BF16_ROWS
LOG2E = math.log2(math.e)
WIN_ROWS = CHUNK + BF16_ROWS
NEG = -1e30

_NT = (((1,), (1,)), ((), ()))


def _cparams(sem):
    return pltpu.CompilerParams(dimension_semantics=sem, vmem_limit_bytes=VMEM_LIMIT)


def _largest_divisor(n, cap):
    return max(d for d in range(1, cap + 1) if n % d == 0)


def _rel_bucket(rel):
    half = N_BUCKETS // 2
    max_exact = half // 2
    n = jnp.abs(rel)
    large = max_exact + (jnp.log(jnp.maximum(n, 1).astype(F32) / max_exact)
                         / math.log(MAX_DISTANCE / max_exact) * (half - max_exact)).astype(jnp.int32)
    large = jnp.minimum(large, half - 1)
    return jnp.where(rel > 0, half, 0) + jnp.where(n < max_exact, n, large)


def _proj_kernel(h_ref, g_ref, w_ref, proj_ref, vt_ref):
    x = h_ref[...]
    ms = jnp.mean(x * x, axis=-1, keepdims=True)
    a = (x * lax.rsqrt(ms + RMS_EPS) * g_ref[...]).astype(BF16)
    proj = jnp.dot(a, w_ref[...], preferred_element_type=F32)
    proj_ref[:, :DIFF_WIDTH] = (proj[:, :DIFF_WIDTH] * LOG2E).astype(BF16)
    proj_ref[:, DIFF_WIDTH:] = proj[:, DIFF_WIDTH:].astype(BF16)
    ones = jnp.ones((VT_ROWS - LANES, x.shape[0]), BF16)
    for h in range(DIFF_HEADS):
        v = proj[:, 2 * DIFF_WIDTH + h * LANES: 2 * DIFF_WIDTH + (h + 1) * LANES]
        vt_ref[h, :LANES, :] = v.T.astype(BF16)
        vt_ref[h, LANES:, :] = ones


def _proj(h0, g, w_in_bf16, tm):
    rows = h0.shape[0]
    return pl.pallas_call(
        _proj_kernel,
        grid=(rows // tm,),
        in_specs=[
            pl.BlockSpec((tm, D_MODEL), lambda i: (i, 0)),
            pl.BlockSpec((1, D_MODEL), lambda i: (0, 0)),
            pl.BlockSpec((D_MODEL, IN_COLS), lambda i: (0, 0)),
        ],
        out_specs=[
            pl.BlockSpec((tm, IN_COLS), lambda i: (i, 0)),
            pl.BlockSpec((DIFF_HEADS, VT_ROWS, tm), lambda i: (0, 0, i)),
        ],
        out_shape=[
            jax.ShapeDtypeStruct((rows, IN_COLS), BF16),
            jax.ShapeDtypeStruct((DIFF_HEADS, VT_ROWS, rows), BF16),
        ],
        compiler_params=_cparams(("parallel",)),
        name="proj",
    )(h0, g, w_in_bf16)


def _diff_kernel(consts_ref, lamv_ref, q_ref, k_ref, vt_ref, bias_ref, g_ref, o_ref,
                 acc_ref, m_ref, *, seq_len, nkv, nsub):
    T = SEQ_TILE
    h = pl.program_id(1)
    i = pl.program_id(2)
    q = q_ref[...]
    lane = lax.broadcasted_iota(jnp.int32, q.shape, 1)
    zero = jnp.zeros_like(q)
    q_maps = (jnp.where(lane < HEAD_DIM, q, zero), jnp.where(lane >= HEAD_DIM, q, zero))

    acc_ref[...] = jnp.zeros_like(acc_ref)
    m_ref[...] = jnp.full_like(m_ref, NEG)

    def chunks(j0, count, *, tile, side, mask):
        maps = range(2)
        c = [0.0, 0.0] if tile else [consts_ref[h, 2 * mp + side] for mp in maps]
        offs = [pl.multiple_of((j0 + t) * T, T) for t in range(count)]
        s = [[lax.dot_general(k_ref[pl.ds(off, T), :], q_maps[mp], _NT, preferred_element_type=F32)
              for mp in maps] for off in offs]
        m_run = [m_ref[mp] for mp in maps]
        for t, off in enumerate(offs):
            vb = vt_ref[0, :, pl.ds(off, T)]
            if mask:
                kvalid = (off + lax.broadcasted_iota(jnp.int32, (T, 1), 0)) < seq_len
            for mp in maps:
                st = s[t][mp]
                if tile:
                    st = jnp.concatenate(
                        [st[:, u * T:(u + 1) * T] + bias_ref[0, mp, jnp.clip(j0 + t - (i * nsub + u), -2, 2) + 2]
                         for u in range(nsub)], axis=1)
                if mask:
                    st = jnp.where(kvalid, st, NEG)
                m_new = jnp.maximum(m_run[mp], jnp.max(st, axis=0, keepdims=True) + c[mp])
                alpha = jnp.exp2(m_run[mp] - m_new)
                m_run[mp] = m_new
                p = jnp.exp2(st - (m_new - c[mp])).astype(BF16)
                acc_ref[mp] = alpha * acc_ref[mp] + jnp.dot(vb, p, preferred_element_type=F32)
        for mp in maps:
            m_ref[mp] = m_run[mp]

    def loop(lo, hi, group, **kw):
        ngroups = jnp.maximum(hi - lo, 0) // group

        def grouped(g, carry):
            chunks(lo + g * group, group, **kw)
            return carry

        def single(j, carry):
            chunks(j, 1, **kw)
            return carry

        lax.fori_loop(0, ngroups, grouped, 0)
        if group > 1:
            lax.fori_loop(lo + ngroups * group, hi, single, 0)

    near_lo = jnp.maximum(i * nsub - 1, 0)
    near_hi = jnp.minimum((i + 1) * nsub + 1, nkv)
    loop(0, near_lo, FAR_GROUP, tile=False, side=0, mask=False)
    loop(near_lo, near_hi, 1, tile=True, side=0, mask=True)
    loop(near_hi, nkv - 1, FAR_GROUP, tile=False, side=1, mask=False)
    loop(jnp.maximum(near_hi, nkv - 1), nkv, 1, tile=False, side=1, mask=True)

    lamv = lamv_ref[...]
    lam = (jnp.exp(jnp.sum(lamv[0:1] * lamv[1:2], axis=-1, keepdims=True))
           - jnp.exp(jnp.sum(lamv[2:3] * lamv[3:4], axis=-1, keepdims=True)) + LAM_INIT)
    o = (acc_ref[0, :LANES] / acc_ref[0, LANES:LANES + 1]
         - lam * (acc_ref[1, :LANES] / acc_ref[1, LANES:LANES + 1]))
    ms = jnp.mean(o * o, axis=0, keepdims=True)
    y = o * lax.rsqrt(ms + RMS_EPS) * g_ref[...] * (1.0 - LAM_INIT)
    o_ref[...] = y.T.astype(BF16)


def _diff_attention(proj, vt, consts, lamv, bias_t, subln_g, batch, P, seq_len, nsub):
    T = SEQ_TILE
    tq = nsub * T
    nq = P // tq
    nkv = pl.cdiv(seq_len, T)
    kern = functools.partial(_diff_kernel, seq_len=seq_len, nkv=nkv, nsub=nsub)
    return pl.pallas_call(
        kern,
        grid=(batch, DIFF_HEADS, nq),
        in_specs=[
            pl.BlockSpec(memory_space=pltpu.SMEM),
            pl.BlockSpec((4, HEAD_DIM), lambda b, h, i: (0, 0)),
            pl.BlockSpec((tq, LANES), lambda b, h, i: (b * nq + i, h)),
            pl.BlockSpec((P, LANES), lambda b, h, i: (b, DIFF_HEADS + h)),
            pl.BlockSpec((1, VT_ROWS, P), lambda b, h, i: (h, 0, b)),
            pl.BlockSpec((1, 2, 5, T, T), lambda b, h, i: (h, 0, 0, 0, 0)),
            pl.BlockSpec((LANES, 1), lambda b, h, i: (0, 0)),
        ],
        out_specs=pl.BlockSpec((tq, LANES), lambda b, h, i: (b * nq + i, h)),
        out_shape=jax.ShapeDtypeStruct((batch * P, DIFF_WIDTH), BF16),
        scratch_shapes=[
            pltpu.VMEM((2, VT_ROWS, tq), F32),
            pltpu.VMEM((2, 1, tq), F32),
        ],
        compiler_params=_cparams(("parallel", "parallel", "parallel")),
        name="diff_attn",
    )(consts, lamv, proj, proj, vt, bias_t, subln_g)


def _swa_kernel(sink_ref, q_ref, k_ref, v_ref, bias_ref, o_ref, *, seq_len, P):
    T = SEQ_TILE
    i = pl.program_id(1)
    start = pl.multiple_of(jnp.clip(i * T - WINDOW, 0, P - SWA_KEYS), LANES)
    boff = pl.multiple_of(start - i * T + T, LANES)
    kw = k_ref[pl.ds(start, SWA_KEYS), :]
    vw = v_ref[pl.ds(start, SWA_KEYS), :]
    kvalid = (start + lax.broadcasted_iota(jnp.int32, (1, SWA_KEYS), 1)) < seq_len
    lane = lax.broadcasted_iota(jnp.int32, (T, LANES), 1)
    for j in range(SWA_GROUP):
        qp = q_ref[:, j * LANES:(j + 1) * LANES]
        zero = jnp.zeros_like(qp)
        outs = []
        for kvh in range(SWA_KV_HEADS):
            head = kvh * SWA_GROUP + j
            in_half = (lane < HEAD_DIM) if kvh == 0 else (lane >= HEAD_DIM)
            qe = jnp.where(in_half, qp, zero)
            s = lax.dot_general(qe, kw, _NT, preferred_element_type=F32)
            s = s + bias_ref[head, :, pl.ds(boff, SWA_KEYS)]
            s = jnp.where(kvalid, s, NEG)
            sink = sink_ref[head]
            m = jnp.maximum(jnp.max(s, axis=-1, keepdims=True), sink)
            p = jnp.exp(s - m)
            l = jnp.sum(p, axis=-1, keepdims=True) + jnp.exp(sink - m)
            outs.append(jnp.dot(p.astype(BF16), vw, preferred_element_type=F32) / l)
        o_ref[:, j * LANES:(j + 1) * LANES] = jnp.where(lane < HEAD_DIM, outs[0], outs[1]).astype(BF16)


def _swa_attention(proj, sink, bias_w, batch, P, seq_len):
    T = SEQ_TILE
    nq = P // T
    q_blk = (3 * DIFF_WIDTH) // SWA_WIDTH
    k_blk = (3 * DIFF_WIDTH + SWA_WIDTH) // LANES
    kern = functools.partial(_swa_kernel, seq_len=seq_len, P=P)
    return pl.pallas_call(
        kern,
        grid=(batch, nq),
        in_specs=[
            pl.BlockSpec(memory_space=pltpu.SMEM),
            pl.BlockSpec((T, SWA_WIDTH), lambda b, i: (b * nq + i, q_blk)),
            pl.BlockSpec((P, LANES), lambda b, i: (b, k_blk)),
            pl.BlockSpec((P, LANES), lambda b, i: (b, k_blk + 1)),
            pl.BlockSpec((SWA_Q_HEADS, T, SWA_KEYS + T), lambda b, i: (0, 0, 0)),
        ],
        out_specs=pl.BlockSpec((T, SWA_WIDTH), lambda b, i: (b * nq + i, 0)),
        out_shape=jax.ShapeDtypeStruct((batch * P, SWA_WIDTH), BF16),
        compiler_params=_cparams(("parallel", "parallel")),
        name="swa_attn",
    )(sink, proj, proj, proj, bias_w)


def _out_kernel(do_ref, so_ref, h_ref, w_ref, g_ref, wr_ref, h2_ref, m_ref, p3_ref, *, seq_len, tm):
    i = pl.program_id(1)
    h2 = (h_ref[...]
          + jnp.dot(do_ref[...], w_ref[:DIFF_WIDTH, :], preferred_element_type=F32)
          + jnp.dot(so_ref[...], w_ref[DIFF_WIDTH:, :], preferred_element_type=F32))
    h2_ref[...] = h2
    ms = jnp.mean(h2 * h2, axis=-1, keepdims=True)
    mf = h2 * lax.rsqrt(ms + RMS_EPS) * g_ref[...]
    m_ref[...] = mf.astype(BF16)
    logits = lax.dot_general(wr_ref[...], mf, _NT, preferred_element_type=F32,
                             precision=lax.Precision.HIGHEST)
    e = jnp.exp(logits - jnp.max(logits, axis=0, keepdims=True))
    probs = e / jnp.sum(e, axis=0, keepdims=True)
    pos = i * tm + lax.broadcasted_iota(jnp.int32, (1, tm), 1)
    probs = jnp.where(pos < seq_len, probs, -1.0)
    for c in range(tm // CHUNK):
        p3_ref[c] = probs[:, c * CHUNK:(c + 1) * CHUNK]


def _out_router(diff_o, swa_o, h0, w_out_bf16, g, w_router_t, batch, P, seq_len):
    tm = SEQ_TILE
    nq = P // tm
    rows = batch * P
    kern = functools.partial(_out_kernel, seq_len=seq_len, tm=tm)
    return pl.pallas_call(
        kern,
        grid=(batch, nq),
        in_specs=[
            pl.BlockSpec((tm, DIFF_WIDTH), lambda b, i: (b * nq + i, 0)),
            pl.BlockSpec((tm, SWA_WIDTH), lambda b, i: (b * nq + i, 0)),
            pl.BlockSpec((tm, D_MODEL), lambda b, i: (b * nq + i, 0)),
            pl.BlockSpec((MIX_WIDTH, D_MODEL), lambda b, i: (0, 0)),
            pl.BlockSpec((1, D_MODEL), lambda b, i: (0, 0)),
            pl.BlockSpec((N_EXPERTS, D_MODEL), lambda b, i: (0, 0)),
        ],
        out_specs=[
            pl.BlockSpec((tm, D_MODEL), lambda b, i: (b * nq + i, 0)),
            pl.BlockSpec((tm, D_MODEL), lambda b, i: (b * nq + i, 0)),
            pl.BlockSpec((tm // CHUNK, N_EXPERTS, CHUNK), lambda b, i: (b * nq + i, 0, 0)),
        ],
        out_shape=[
            jax.ShapeDtypeStruct((rows, D_MODEL), F32),
            jax.ShapeDtypeStruct((rows, D_MODEL), BF16),
            jax.ShapeDtypeStruct((rows // CHUNK, N_EXPERTS, CHUNK), F32),
        ],
        compiler_params=_cparams(("parallel", "parallel")),
        name="out_router",
    )(diff_o, swa_o, h0, w_out_bf16, g, w_router_t)


def _topk_kernel(p3_ref, pos_ref, post_ref, gatet_ref, base_ref, incl_ref, flag_ref, *, capacity, nc):
    E = N_EXPERTS
    probs = p3_ref[...]
    bits = pltpu.bitcast(probs, jnp.int32)

    def count(pred):
        part = jnp.sum(jnp.where(pred, 1.0, 0.0), axis=0, keepdims=True)
        return jnp.broadcast_to(jnp.sum(part, axis=-1, keepdims=True), part.shape)

    def bisect(_, lohi):
        lo, hi = lohi
        mid = lo + ((hi - lo + 1) >> 1)
        ok = count(bits >= mid) >= capacity
        return jnp.where(ok, mid, lo), jnp.where(ok, hi, mid - 1)

    lo0 = jnp.zeros((1, E, CHUNK), jnp.int32)
    hi0 = jnp.full((1, E, CHUNK), 0x7F800000, jnp.int32)
    thr, _ = lax.fori_loop(0, 32, bisect, (lo0, hi0))
    gt = bits > thr
    eq = bits == thr
    need = capacity - count(gt)[0]

    tri = (lax.broadcasted_iota(jnp.int32, (CHUNK, CHUNK), 0)
           <= lax.broadcasted_iota(jnp.int32, (CHUNK, CHUNK), 1)).astype(BF16)

    def inclusive_prefix(flags):
        f2 = flags.astype(BF16).reshape(nc * E, CHUNK)
        return jnp.dot(f2, tri, preferred_element_type=F32).reshape(nc, E, CHUNK)

    eqf = jnp.where(eq, 1.0, 0.0)
    incl_ref[...] = inclusive_prefix(eqf)
    flag_ref[...] = eqf

    def tie_scan(c, run):
        inc = incl_ref[c]
        e_c = flag_ref[c]
        take = (e_c > 0.0) & ((run + inc - e_c) < need)
        flag_ref[c] = jnp.where(take, 1.0, 0.0)
        return run + jnp.broadcast_to(inc[:, CHUNK - 1:CHUNK], inc.shape)

    lax.fori_loop(0, nc, tie_scan, jnp.zeros((E, CHUNK), F32))
    self = jnp.where(gt, 1.0, flag_ref[...])
    flag_ref[...] = self
    incl_ref[...] = inclusive_prefix(self)

    zpad = jnp.zeros((CHUNK - E, CHUNK), F32)

    def to_token_major(x):
        return jnp.concatenate([x, zpad], axis=0).T[:, :E]

    def pos_scan(c, run):
        inc = incl_ref[c]
        s_c = flag_ref[c]
        sel = s_c > 0.0
        pos = jnp.where(sel, run + inc - s_c, -1.0)
        gate = jnp.where(sel, p3_ref[c], 0.0)
        off = pl.multiple_of(c * CHUNK, CHUNK)
        pos_ref[:, pl.ds(off, CHUNK)] = pos
        post_ref[pl.ds(off, CHUNK), :] = to_token_major(pos)
        gatet_ref[pl.ds(off, CHUNK), :] = to_token_major(gate)
        base_ref[c] = run.astype(jnp.int32)
        return run + jnp.broadcast_to(inc[:, CHUNK - 1:CHUNK], inc.shape)

    lax.fori_loop(0, nc, pos_scan, jnp.zeros((E, CHUNK), F32))


def _topk(probs3, capacity):
    nc = probs3.shape[0]
    nt = nc * CHUNK
    kern = functools.partial(_topk_kernel, capacity=capacity, nc=nc)
    return pl.pallas_call(
        kern,
        out_shape=[
            jax.ShapeDtypeStruct((N_EXPERTS, nt), F32),
            jax.ShapeDtypeStruct((nt, N_EXPERTS), F32),
            jax.ShapeDtypeStruct((nt, N_EXPERTS), F32),
            jax.ShapeDtypeStruct((nc, N_EXPERTS, CHUNK), jnp.int32),
        ],
        scratch_shapes=[
            pltpu.VMEM((nc, N_EXPERTS, CHUNK), F32),
            pltpu.VMEM((nc, N_EXPERTS, CHUNK), F32),
        ],
        compiler_params=pltpu.CompilerParams(vmem_limit_bytes=VMEM_LIMIT),
        name="topk",
    )(probs3)


def _gather_kernel(base_ref, pos_ref, m_ref, xc_ref, *, nc, sub):
    e = pl.program_id(0)
    sb = pl.program_id(1)

    @pl.when(sb == 0)
    def _():
        xc_ref[...] = jnp.zeros_like(xc_ref)

    row = lax.broadcasted_iota(jnp.int32, (WIN_ROWS, CHUNK), 0).astype(F32)
    erow = lax.broadcasted_iota(jnp.int32, (N_EXPERTS, CHUNK), 0)

    def body(u, carry):
        b0 = base_ref[e * nc + sb * sub + u]
        a = pl.multiple_of((b0 // BF16_ROWS) * BF16_ROWS, BF16_ROWS)
        off = pl.multiple_of(u * CHUNK, CHUNK)
        pblk = pos_ref[:, pl.ds(off, CHUNK)]
        prow = jnp.sum(jnp.where(erow == e, pblk, 0.0), axis=0, keepdims=True)
        onehot = jnp.where(row == prow - a.astype(F32), 1.0, 0.0).astype(BF16)
        x = m_ref[pl.ds(off, CHUNK), :]
        g = jnp.dot(onehot, x, preferred_element_type=F32)
        xc_ref[0, pl.ds(a, WIN_ROWS), :] += g.astype(BF16)
        return carry

    lax.fori_loop(0, sub, body, 0)


def _gather(base_flat, pos, m, cp):
    nt = pos.shape[1]
    nc = nt // CHUNK
    sub = _largest_divisor(nc, 17)
    ts = sub * CHUNK
    kern = functools.partial(_gather_kernel, nc=nc, sub=sub)
    return pl.pallas_call(
        kern,
        grid_spec=pltpu.PrefetchScalarGridSpec(
            num_scalar_prefetch=1,
            grid=(N_EXPERTS, nc // sub),
            in_specs=[
                pl.BlockSpec((N_EXPERTS, ts), lambda e, s, base: (0, s)),
                pl.BlockSpec((ts, D_MODEL), lambda e, s, base: (s, 0)),
            ],
            out_specs=pl.BlockSpec((1, cp, D_MODEL), lambda e, s, base: (e, 0, 0)),
        ),
        out_shape=jax.ShapeDtypeStruct((N_EXPERTS, cp, D_MODEL), BF16),
        compiler_params=_cparams(("parallel", "arbitrary")),
        name="gather",
    )(base_flat, pos, m)


def _ffn_kernel(x_ref, wg_ref, wu_ref, wd_ref, y_ref, acc_ref, *, rows):
    f = pl.program_id(1)
    nf = pl.num_programs(1)

    @pl.when(f == 0)
    def _():
        acc_ref[...] = jnp.zeros_like(acc_ref)

    x = x_ref[0, :rows, :]
    g = jnp.dot(x, wg_ref[0].astype(BF16), preferred_element_type=F32)
    u = jnp.dot(x, wu_ref[0].astype(BF16), preferred_element_type=F32)
    hid = g * jax.nn.sigmoid(g) * u
    valid = D_FF - f * FF_TILE
    col = lax.broadcasted_iota(jnp.int32, (1, FF_TILE), 1)
    hid = jnp.where(col < valid, hid, 0.0).astype(BF16)
    wrow = lax.broadcasted_iota(jnp.int32, (FF_TILE, 1), 0)
    wd = jnp.where(wrow < valid, wd_ref[0], 0.0).astype(BF16)
    acc_ref[...] += jnp.dot(hid, wd, preferred_element_type=F32)

    @pl.when(f == nf - 1)
    def _():
        y_ref[0, :rows, :] = acc_ref[...].astype(BF16)
        y_ref[0, rows:, :] = jnp.zeros((y_ref.shape[1] - rows, D_MODEL), BF16)


def _ffn(xc, w_gate, w_up, w_down, rows):
    cp = xc.shape[1]
    nf = pl.cdiv(D_FF, FF_TILE)
    kern = functools.partial(_ffn_kernel, rows=rows)
    return pl.pallas_call(
        kern,
        grid=(N_EXPERTS, nf),
        in_specs=[
            pl.BlockSpec((1, cp, D_MODEL), lambda e, f: (e, 0, 0)),
            pl.BlockSpec((1, D_MODEL, FF_TILE), lambda e, f: (e, 0, f)),
            pl.BlockSpec((1, D_MODEL, FF_TILE), lambda e, f: (e, 0, f)),
            pl.BlockSpec((1, FF_TILE, D_MODEL), lambda e, f: (e, f, 0)),
        ],
        out_specs=pl.BlockSpec((1, cp, D_MODEL), lambda e, f: (e, 0, 0)),
        out_shape=jax.ShapeDtypeStruct((N_EXPERTS, cp, D_MODEL), BF16),
        scratch_shapes=[pltpu.VMEM((rows, D_MODEL), F32)],
        compiler_params=_cparams(("parallel", "arbitrary")),
        name="ffn",
    )(xc, w_gate, w_up, w_down)


def _combine_kernel(base_ref, h2_ref, post_ref, gatet_ref, g_ref, yc_ref, o_ref,
                    win_ref, carry_ref, sem_ref, *, nc, cpb, out_blocks):
    E = N_EXPERTS
    b = pl.program_id(0)
    c = pl.program_id(1)
    step = b * cpb + c
    nsteps = pl.num_programs(0) * cpb
    slot = step % 2

    def window_copy(chunk, e, sl):
        b0 = base_ref[e * nc + chunk]
        a = pl.multiple_of((b0 // BF16_ROWS) * BF16_ROWS, BF16_ROWS)
        return pltpu.make_async_copy(yc_ref.at[e, pl.ds(a, WIN_ROWS), :], win_ref.at[sl, e], sem_ref.at[sl, e])

    @pl.when(step == 0)
    def _():
        for e in range(E):
            window_copy(step, e, slot).start()

    @pl.when(step + 1 < nsteps)
    def _():
        for e in range(E):
            window_copy(step + 1, e, 1 - slot).start()

    acc = h2_ref[...]
    col = lax.broadcasted_iota(jnp.int32, (CHUNK, WIN_ROWS), 1).astype(F32)
    for e in range(E):
        b0 = base_ref[e * nc + step]
        a = ((b0 // BF16_ROWS) * BF16_ROWS).astype(F32)
        window_copy(step, e, slot).wait()
        onehot = jnp.where(col == post_ref[:, e:e + 1] - a, 1.0, 0.0).astype(BF16)
        contrib = jnp.dot(onehot, win_ref[slot, e], preferred_element_type=F32)
        acc = acc + gatet_ref[:, e:e + 1] * contrib

    ms = jnp.mean(acc * acc, axis=-1, keepdims=True)
    y = acc * lax.rsqrt(ms + RMS_EPS) * g_ref[...]

    @pl.when((c >= 1) & (c <= out_blocks))
    def _():
        o_ref[0, :CHUNK - N_META, :] = carry_ref[N_META:, :]
        o_ref[0, CHUNK - N_META:, :] = y[:N_META, :]

    carry_ref[...] = y


def _combine(base_flat, h2, post, gatet, g, yc, batch, P, S):
    nt = h2.shape[0]
    nc = nt // CHUNK
    cpb = P // CHUNK
    out_blocks = S // CHUNK
    kern = functools.partial(_combine_kernel, nc=nc, cpb=cpb, out_blocks=out_blocks)
    return pl.pallas_call(
        kern,
        grid_spec=pltpu.PrefetchScalarGridSpec(
            num_scalar_prefetch=1,
            grid=(batch, cpb),
            in_specs=[
                pl.BlockSpec((CHUNK, D_MODEL), lambda b, c, base: (b * cpb + c, 0)),
                pl.BlockSpec((CHUNK, N_EXPERTS), lambda b, c, base: (b * cpb + c, 0)),
                pl.BlockSpec((CHUNK, N_EXPERTS), lambda b, c, base: (b * cpb + c, 0)),
                pl.BlockSpec((1, D_MODEL), lambda b, c, base: (0, 0)),
                pl.BlockSpec(memory_space=pl.ANY),
            ],
            out_specs=pl.BlockSpec(
                (1, CHUNK, D_MODEL), lambda b, c, base: (b, jnp.clip(c - 1, 0, out_blocks - 1), 0)),
            scratch_shapes=[
                pltpu.VMEM((2, N_EXPERTS, WIN_ROWS, D_MODEL), BF16),
                pltpu.VMEM((CHUNK, D_MODEL), F32),
                pltpu.SemaphoreType.DMA((2, N_EXPERTS)),
            ],
        ),
        out_shape=jax.ShapeDtypeStruct((batch, S, D_MODEL), F32),
        compiler_params=_cparams(("arbitrary", "arbitrary")),
        name="combine",
    )(base_flat, h2, post, gatet, g, yc)


def _swa_head_perm():
    perm = np.arange(SWA_WIDTH).reshape(SWA_KV_HEADS, SWA_GROUP, HEAD_DIM)
    return perm.transpose(1, 0, 2).reshape(-1)


def _prep_params(rel_bias, w_in, w_out):
    perm = _swa_head_perm()
    scale = np.ones((IN_COLS,), np.float32)
    scale[:DIFF_WIDTH] = HEAD_DIM ** -0.5
    scale[3 * DIFF_WIDTH:3 * DIFF_WIDTH + SWA_WIDTH] = HEAD_DIM ** -0.5
    cols = np.arange(IN_COLS)
    cols[3 * DIFF_WIDTH:3 * DIFF_WIDTH + SWA_WIDTH] = 3 * DIFF_WIDTH + perm
    w_in_p = (w_in * scale)[:, cols].astype(BF16)
    rows = np.arange(MIX_WIDTH)
    rows[DIFF_WIDTH:] = DIFF_WIDTH + perm
    w_out_p = w_out[rows, :].astype(BF16)

    T = SEQ_TILE
    table = rel_bias.astype(F32)

    def lookup(idx, cols):
        onehot = (idx[..., None] == jnp.arange(N_BUCKETS)).astype(F32)
        return jnp.einsum("...b,bc->...c", onehot, table[:, cols], precision=lax.Precision.HIGHEST)

    kk = jnp.arange(T)[:, None]
    qq = jnp.arange(T)[None, :]
    idx = jnp.stack([_rel_bucket(d * T + kk - qq) for d in (-2, -1, 0, 1, 2)])
    diff_tiles = lookup(idx, slice(0, 2 * DIFF_HEADS))
    diff_tiles = diff_tiles.transpose(3, 0, 1, 2).reshape(DIFF_HEADS, 2, 5, T, T) * LOG2E
    half = N_BUCKETS // 2
    diff_consts = jnp.stack([table[half - 1, 0:2 * DIFF_HEADS:2], table[N_BUCKETS - 1, 0:2 * DIFF_HEADS:2],
                             table[half - 1, 1:2 * DIFF_HEADS:2], table[N_BUCKETS - 1, 1:2 * DIFF_HEADS:2]],
                            axis=1) * LOG2E
    rel = jnp.arange(SWA_KEYS + T)[None, :] - T - jnp.arange(T)[:, None]
    swa_tiles = lookup(_rel_bucket(rel), slice(2 * DIFF_HEADS, None))
    swa_tiles = jnp.where((jnp.abs(rel) <= WINDOW)[:, :, None], swa_tiles, NEG).transpose(2, 0, 1)
    return w_in_p, w_out_p, diff_tiles, diff_consts, swa_tiles


def _plan_rows(seq_len):
    nsub = min((4, 5, 6), key=lambda n: (-(-seq_len // (n * SEQ_TILE)) * n, -n))
    return nsub, -(-seq_len // (nsub * SEQ_TILE)) * nsub * SEQ_TILE


def _trunk(x, meta_tokens, prep, attn_norm_g, lamv, subln_g, swa_sink, ffn_norm_g, w_router_t,
           w_gate, w_up, w_down, final_norm_g):
    w_in_p, w_out_p, diff_tiles, diff_consts, swa_tiles = prep
    B, S, _ = x.shape
    L = S + N_META
    nsub, P = _plan_rows(L)
    rows = B * P
    capacity = EC_FACTOR * (B * L) // N_EXPERTS
    ffn_rows = -(-capacity // BF16_ROWS) * BF16_ROWS
    cp = -(-(capacity + WIN_ROWS) // BF16_ROWS) * BF16_ROWS

    h0 = jnp.concatenate([jnp.broadcast_to(meta_tokens.astype(x.dtype)[None], (B, N_META, D_MODEL)), x,
                          jnp.zeros((B, P - L, D_MODEL), x.dtype)], axis=1).reshape(rows, D_MODEL)

    tm = SEQ_TILE * _largest_divisor(rows // SEQ_TILE, 2)
    proj, vt = _proj(h0, attn_norm_g.reshape(1, D_MODEL), w_in_p, tm)
    diff_o = _diff_attention(proj, vt, diff_consts, lamv, diff_tiles, subln_g.reshape(LANES, 1), B, P, L, nsub)
    swa_o = _swa_attention(proj, swa_sink, swa_tiles, B, P, L)
    h2, m, probs3 = _out_router(diff_o, swa_o, h0, w_out_p, ffn_norm_g.reshape(1, D_MODEL), w_router_t, B, P, L)
    pos, post, gatet, base3 = _topk(probs3, capacity)
    base_flat = base3[:, :, 0].T.reshape(-1)
    xc = _gather(base_flat, pos, m, cp)
    yc = _ffn(xc, w_gate, w_up, w_down, ffn_rows)
    return _combine(base_flat, h2, post, gatet, final_norm_g.reshape(1, D_MODEL), yc, B, P, S)


def kernel(x_prompt, x_sample, meta_tokens, rel_bias, attn_norm_g, w_in, diff_lambda_q1, diff_lambda_k1,
           diff_lambda_q2, diff_lambda_k2, diff_subln_g, swa_sink, w_out, ffn_norm_g, w_router, w_gate, w_up,
           w_down, final_norm_g):
    prep = _prep_params(rel_bias, w_in[0], w_out[0])
    lamv = jnp.stack([diff_lambda_q1[0], diff_lambda_k1[0], diff_lambda_q2[0], diff_lambda_k2[0]]).astype(F32)
    args = (meta_tokens, prep, attn_norm_g[0], lamv, diff_subln_g[0], swa_sink[0].astype(F32), ffn_norm_g[0],
            w_router[0].T, w_gate[0], w_up[0], w_down[0], final_norm_g)
    return (_trunk(x_prompt, *args), _trunk(x_sample, *args))
```

```python
import functools
import math

import jax
import jax.numpy as jnp
import numpy as np
from jax import lax
from jax.experimental import pallas as pl
from jax.experimental.pallas import tpu as pltpu

F32 = jnp.float32
BF16 = jnp.bfloat16

D_MODEL = 1024
HEAD_DIM = 64
N_META = 16
DIFF_HEADS = 4
DIFF_WIDTH = DIFF_HEADS * 2 * HEAD_DIM
SWA_Q_HEADS = 8
SWA_KV_HEADS = 2
SWA_GROUP = SWA_Q_HEADS // SWA_KV_HEADS
SWA_WIDTH = SWA_Q_HEADS * HEAD_DIM
SWA_KV_WIDTH = SWA_KV_HEADS * HEAD_DIM
MIX_WIDTH = DIFF_WIDTH + SWA_WIDTH
IN_COLS = 3 * DIFF_WIDTH + SWA_WIDTH + 2 * SWA_KV_WIDTH
WINDOW = 128
N_BUCKETS = 32
MAX_DISTANCE = 128
N_EXPERTS = 16
EC_FACTOR = 2
D_FF = 2752
RMS_EPS = 1e-6
LAM_INIT = 0.8 - 0.6 * math.exp(-0.3 * 0)

LANES = 128
BF16_ROWS = 16
VMEM_LIMIT = 56 * 1024 * 1024

SEQ_TILE = 256
SWA_KEYS = SEQ_TILE + 2 * WINDOW
FAR_GROUP = 2
FF_TILE = 256
CHUNK = LANES
VT_ROWS = LANES + BF16_ROWS
LOG2E = math.log2(math.e)
WIN_ROWS = CHUNK + BF16_ROWS
FAST_ROWS = 64
FAST_CNT = FAST_ROWS - BF16_ROWS
NEG = -1e30

_NT = (((1,), (1,)), ((), ()))


def _cparams(sem):
    return pltpu.CompilerParams(dimension_semantics=sem, vmem_limit_bytes=VMEM_LIMIT)


def _largest_divisor(n, cap):
    return max(d for d in range(1, cap + 1) if n % d == 0)


def _rel_bucket(rel):
    half = N_BUCKETS // 2
    max_exact = half // 2
    n = jnp.abs(rel)
    large = max_exact + (jnp.log(jnp.maximum(n, 1).astype(F32) / max_exact)
                         / math.log(MAX_DISTANCE / max_exact) * (half - max_exact)).astype(jnp.int32)
    large = jnp.minimum(large, half - 1)
    return jnp.where(rel > 0, half, 0) + jnp.where(n < max_exact, n, large)


def _proj_kernel(h_ref, g_ref, w_ref, proj_ref, vt_ref):
    x = h_ref[...]
    ms = jnp.mean(x * x, axis=-1, keepdims=True)
    a = (x * lax.rsqrt(ms + RMS_EPS) * g_ref[...]).astype(BF16)
    proj = jnp.dot(a, w_ref[...], preferred_element_type=F32)
    proj_ref[:, :DIFF_WIDTH] = (proj[:, :DIFF_WIDTH] * LOG2E).astype(BF16)
    proj_ref[:, DIFF_WIDTH:] = proj[:, DIFF_WIDTH:].astype(BF16)
    ones = jnp.ones((VT_ROWS - LANES, x.shape[0]), BF16)
    for h in range(DIFF_HEADS):
        v = proj[:, 2 * DIFF_WIDTH + h * LANES: 2 * DIFF_WIDTH + (h + 1) * LANES]
        vt_ref[h, :LANES, :] = v.T.astype(BF16)
        vt_ref[h, LANES:, :] = ones


def _proj(h0, g, w_in_bf16, tm):
    rows = h0.shape[0]
    return pl.pallas_call(
        _proj_kernel,
        grid=(rows // tm,),
        in_specs=[
            pl.BlockSpec((tm, D_MODEL), lambda i: (i, 0)),
            pl.BlockSpec((1, D_MODEL), lambda i: (0, 0)),
            pl.BlockSpec((D_MODEL, IN_COLS), lambda i: (0, 0)),
        ],
        out_specs=[
            pl.BlockSpec((tm, IN_COLS), lambda i: (i, 0)),
            pl.BlockSpec((DIFF_HEADS, VT_ROWS, tm), lambda i: (0, 0, i)),
        ],
        out_shape=[
            jax.ShapeDtypeStruct((rows, IN_COLS), BF16),
            jax.ShapeDtypeStruct((DIFF_HEADS, VT_ROWS, rows), BF16),
        ],
        compiler_params=_cparams(("parallel",)),
        name="proj",
    )(h0, g, w_in_bf16)


def _diff_kernel(consts_ref, lamv_ref, q_ref, k_ref, vt_ref, bias_ref, g_ref, o_ref,
                 acc_ref, m_ref, *, seq_len, nkv, nsub):
    T = SEQ_TILE
    h = pl.program_id(1)
    i = pl.program_id(2)
    q = q_ref[...]
    lane = lax.broadcasted_iota(jnp.int32, q.shape, 1)
    zero = jnp.zeros_like(q)
    q_maps = (jnp.where(lane < HEAD_DIM, q, zero), jnp.where(lane >= HEAD_DIM, q, zero))

    acc_ref[...] = jnp.zeros_like(acc_ref)
    m_ref[...] = jnp.full_like(m_ref, NEG)

    def chunks(j0, count, *, tile, side, mask):
        maps = range(2)
        c = [0.0, 0.0] if tile else [consts_ref[h, 2 * mp + side] for mp in maps]
        offs = [pl.multiple_of((j0 + t) * T, T) for t in range(count)]
        s = [[lax.dot_general(k_ref[pl.ds(off, T), :], q_maps[mp], _NT, preferred_element_type=F32)
              for mp in maps] for off in offs]
        m_run = [m_ref[mp] for mp in maps]
        for t, off in enumerate(offs):
            vb = vt_ref[0, :, pl.ds(off, T)]
            if mask:
                kvalid = (off + lax.broadcasted_iota(jnp.int32, (T, 1), 0)) < seq_len
            for mp in maps:
                st = s[t][mp]
                if tile:
                    st = jnp.concatenate(
                        [st[:, u * T:(u + 1) * T] + bias_ref[0, mp, jnp.clip(j0 + t - (i * nsub + u), -2, 2) + 2]
                         for u in range(nsub)], axis=1)
                if mask:
                    st = jnp.where(kvalid, st, NEG)
                m_new = jnp.maximum(m_run[mp], jnp.max(st, axis=0, keepdims=True) + c[mp])
                alpha = jnp.exp2(m_run[mp] - m_new)
                m_run[mp] = m_new
                p = jnp.exp2(st - (m_new - c[mp])).astype(BF16)
                acc_ref[mp] = alpha * acc_ref[mp] + jnp.dot(vb, p, preferred_element_type=F32)
        for mp in maps:
            m_ref[mp] = m_run[mp]

    def loop(lo, hi, group, **kw):
        ngroups = jnp.maximum(hi - lo, 0) // group

        def grouped(g, carry):
            chunks(lo + g * group, group, **kw)
            return carry

        def single(j, carry):
            chunks(j, 1, **kw)
            return carry

        lax.fori_loop(0, ngroups, grouped, 0)
        if group > 1:
            lax.fori_loop(lo + ngroups * group, hi, single, 0)

    near_lo = jnp.maximum(i * nsub - 1, 0)
    near_hi = jnp.minimum((i + 1) * nsub + 1, nkv)
    loop(0, near_lo, FAR_GROUP, tile=False, side=0, mask=False)
    loop(near_lo, near_hi, 1, tile=True, side=0, mask=True)
    loop(near_hi, nkv - 1, FAR_GROUP, tile=False, side=1, mask=False)
    loop(jnp.maximum(near_hi, nkv - 1), nkv, 1, tile=False, side=1, mask=True)

    lamv = lamv_ref[...]
    lam = (jnp.exp(jnp.sum(lamv[0:1] * lamv[1:2], axis=-1, keepdims=True))
           - jnp.exp(jnp.sum(lamv[2:3] * lamv[3:4], axis=-1, keepdims=True)) + LAM_INIT)
    o = (acc_ref[0, :LANES] / acc_ref[0, LANES:LANES + 1]
         - lam * (acc_ref[1, :LANES] / acc_ref[1, LANES:LANES + 1]))
    ms = jnp.mean(o * o, axis=0, keepdims=True)
    y = o * lax.rsqrt(ms + RMS_EPS) * g_ref[...] * (1.0 - LAM_INIT)
    o_ref[...] = y.T.astype(BF16)


def _diff_attention(proj, vt, consts, lamv, bias_t, subln_g, batch, P, seq_len, nsub):
    T = SEQ_TILE
    tq = nsub * T
    nq = P // tq
    nkv = pl.cdiv(seq_len, T)
    kern = functools.partial(_diff_kernel, seq_len=seq_len, nkv=nkv, nsub=nsub)
    return pl.pallas_call(
        kern,
        grid=(batch, DIFF_HEADS, nq),
        in_specs=[
            pl.BlockSpec(memory_space=pltpu.SMEM),
            pl.BlockSpec((4, HEAD_DIM), lambda b, h, i: (0, 0)),
            pl.BlockSpec((tq, LANES), lambda b, h, i: (b * nq + i, h)),
            pl.BlockSpec((P, LANES), lambda b, h, i: (b, DIFF_HEADS + h)),
            pl.BlockSpec((1, VT_ROWS, P), lambda b, h, i: (h, 0, b)),
            pl.BlockSpec((1, 2, 5, T, T), lambda b, h, i: (h, 0, 0, 0, 0)),
            pl.BlockSpec((LANES, 1), lambda b, h, i: (0, 0)),
        ],
        out_specs=pl.BlockSpec((tq, LANES), lambda b, h, i: (b * nq + i, h)),
        out_shape=jax.ShapeDtypeStruct((batch * P, DIFF_WIDTH), BF16),
        scratch_shapes=[
            pltpu.VMEM((2, VT_ROWS, tq), F32),
            pltpu.VMEM((2, 1, tq), F32),
        ],
        compiler_params=_cparams(("parallel", "parallel", "parallel")),
        name="diff_attn",
    )(consts, lamv, proj, proj, vt, bias_t, subln_g)


def _swa_kernel(sink_ref, q_ref, k_ref, v_ref, bias_ref, o_ref, *, seq_len, P):
    T = SEQ_TILE
    i = pl.program_id(1)
    start = pl.multiple_of(jnp.clip(i * T - WINDOW, 0, P - SWA_KEYS), LANES)
    boff = pl.multiple_of(start - i * T + T, LANES)
    kw = k_ref[pl.ds(start, SWA_KEYS), :]
    vw = v_ref[pl.ds(start, SWA_KEYS), :]
    kvalid = (start + lax.broadcasted_iota(jnp.int32, (1, SWA_KEYS), 1)) < seq_len
    lane = lax.broadcasted_iota(jnp.int32, (T, LANES), 1)
    for j in range(SWA_GROUP):
        qp = q_ref[:, j * LANES:(j + 1) * LANES]
        zero = jnp.zeros_like(qp)
        outs = []
        for kvh in range(SWA_KV_HEADS):
            head = kvh * SWA_GROUP + j
            in_half = (lane < HEAD_DIM) if kvh == 0 else (lane >= HEAD_DIM)
            qe = jnp.where(in_half, qp, zero)
            s = lax.dot_general(qe, kw, _NT, preferred_element_type=F32)
            s = s + bias_ref[head, :, pl.ds(boff, SWA_KEYS)]
            s = jnp.where(kvalid, s, NEG)
            sink = sink_ref[head]
            m = jnp.maximum(jnp.max(s, axis=-1, keepdims=True), sink)
            p = jnp.exp(s - m)
            l = jnp.sum(p, axis=-1, keepdims=True) + jnp.exp(sink - m)
            outs.append(jnp.dot(p.astype(BF16), vw, preferred_element_type=F32) / l)
        o_ref[:, j * LANES:(j + 1) * LANES] = jnp.where(lane < HEAD_DIM, outs[0], outs[1]).astype(BF16)


def _swa_attention(proj, sink, bias_w, batch, P, seq_len):
    T = SEQ_TILE
    nq = P // T
    q_blk = (3 * DIFF_WIDTH) // SWA_WIDTH
    k_blk = (3 * DIFF_WIDTH + SWA_WIDTH) // LANES
    kern = functools.partial(_swa_kernel, seq_len=seq_len, P=P)
    return pl.pallas_call(
        kern,
        grid=(batch, nq),
        in_specs=[
            pl.BlockSpec(memory_space=pltpu.SMEM),
            pl.BlockSpec((T, SWA_WIDTH), lambda b, i: (b * nq + i, q_blk)),
            pl.BlockSpec((P, LANES), lambda b, i: (b, k_blk)),
            pl.BlockSpec((P, LANES), lambda b, i: (b, k_blk + 1)),
            pl.BlockSpec((SWA_Q_HEADS, T, SWA_KEYS + T), lambda b, i: (0, 0, 0)),
        ],
        out_specs=pl.BlockSpec((T, SWA_WIDTH), lambda b, i: (b * nq + i, 0)),
        out_shape=jax.ShapeDtypeStruct((batch * P, SWA_WIDTH), BF16),
        compiler_params=_cparams(("parallel", "parallel")),
        name="swa_attn",
    )(sink, proj, proj, proj, bias_w)


def _out_kernel(do_ref, so_ref, h_ref, w_ref, g_ref, wr_ref, h2_ref, m_ref, p3_ref, *, seq_len, tm):
    i = pl.program_id(1)
    h2 = (h_ref[...]
          + jnp.dot(do_ref[...], w_ref[:DIFF_WIDTH, :], preferred_element_type=F32)
          + jnp.dot(so_ref[...], w_ref[DIFF_WIDTH:, :], preferred_element_type=F32))
    h2_ref[...] = h2
    ms = jnp.mean(h2 * h2, axis=-1, keepdims=True)
    mf = h2 * lax.rsqrt(ms + RMS_EPS) * g_ref[...]
    m_ref[...] = mf.astype(BF16)
    logits = lax.dot_general(wr_ref[...], mf, _NT, preferred_element_type=F32,
                             precision=lax.Precision.HIGHEST)
    e = jnp.exp(logits - jnp.max(logits, axis=0, keepdims=True))
    probs = e / jnp.sum(e, axis=0, keepdims=True)
    pos = i * tm + lax.broadcasted_iota(jnp.int32, (1, tm), 1)
    probs = jnp.where(pos < seq_len, probs, -1.0)
    for c in range(tm // CHUNK):
        p3_ref[c] = probs[:, c * CHUNK:(c + 1) * CHUNK]


def _out_router(diff_o, swa_o, h0, w_out_bf16, g, w_router_t, batch, P, seq_len):
    tm = SEQ_TILE
    nq = P // tm
    rows = batch * P
    kern = functools.partial(_out_kernel, seq_len=seq_len, tm=tm)
    return pl.pallas_call(
        kern,
        grid=(batch, nq),
        in_specs=[
            pl.BlockSpec((tm, DIFF_WIDTH), lambda b, i: (b * nq + i, 0)),
            pl.BlockSpec((tm, SWA_WIDTH), lambda b, i: (b * nq + i, 0)),
            pl.BlockSpec((tm, D_MODEL), lambda b, i: (b * nq + i, 0)),
            pl.BlockSpec((MIX_WIDTH, D_MODEL), lambda b, i: (0, 0)),
            pl.BlockSpec((1, D_MODEL), lambda b, i: (0, 0)),
            pl.BlockSpec((N_EXPERTS, D_MODEL), lambda b, i: (0, 0)),
        ],
        out_specs=[
            pl.BlockSpec((tm, D_MODEL), lambda b, i: (b * nq + i, 0)),
            pl.BlockSpec((tm, D_MODEL), lambda b, i: (b * nq + i, 0)),
            pl.BlockSpec((tm // CHUNK, N_EXPERTS, CHUNK), lambda b, i: (b * nq + i, 0, 0)),
        ],
        out_shape=[
            jax.ShapeDtypeStruct((rows, D_MODEL), F32),
            jax.ShapeDtypeStruct((rows, D_MODEL), BF16),
            jax.ShapeDtypeStruct((rows // CHUNK, N_EXPERTS, CHUNK), F32),
        ],
        compiler_params=_cparams(("parallel", "parallel")),
        name="out_router",
    )(diff_o, swa_o, h0, w_out_bf16, g, w_router_t)


def _topk_kernel(p3_ref, pos_ref, post_ref, gatet_ref, base_ref, incl_ref, flag_ref, *, capacity, nc):
    E = N_EXPERTS
    probs = p3_ref[...]
    bits = pltpu.bitcast(probs, jnp.int32)

    def count(pred):
        part = jnp.sum(jnp.where(pred, 1.0, 0.0), axis=0, keepdims=True)
        return jnp.broadcast_to(jnp.sum(part, axis=-1, keepdims=True), part.shape)

    def bisect(_, lohi):
        lo, hi = lohi
        mid = lo + ((hi - lo + 1) >> 1)
        ok = count(bits >= mid) >= capacity
        return jnp.where(ok, mid, lo), jnp.where(ok, hi, mid - 1)

    lo0 = jnp.zeros((1, E, CHUNK), jnp.int32)
    hi0 = jnp.full((1, E, CHUNK), 0x7F800000, jnp.int32)
    thr, _ = lax.fori_loop(0, 32, bisect, (lo0, hi0))
    gt = bits > thr
    eq = bits == thr
    need = capacity - count(gt)[0]

    tri = (lax.broadcasted_iota(jnp.int32, (CHUNK, CHUNK), 0)
           <= lax.broadcasted_iota(jnp.int32, (CHUNK, CHUNK), 1)).astype(BF16)

    def inclusive_prefix(flags):
        f2 = flags.astype(BF16).reshape(nc * E, CHUNK)
        return jnp.dot(f2, tri, preferred_element_type=F32).reshape(nc, E, CHUNK)

    eqf = jnp.where(eq, 1.0, 0.0)
    incl_ref[...] = inclusive_prefix(eqf)
    flag_ref[...] = eqf

    def tie_scan(c, run):
        inc = incl_ref[c]
        e_c = flag_ref[c]
        take = (e_c > 0.0) & ((run + inc - e_c) < need)
        flag_ref[c] = jnp.where(take, 1.0, 0.0)
        return run + jnp.broadcast_to(inc[:, CHUNK - 1:CHUNK], inc.shape)

    lax.fori_loop(0, nc, tie_scan, jnp.zeros((E, CHUNK), F32))
    self = jnp.where(gt, 1.0, flag_ref[...])
    flag_ref[...] = self
    incl_ref[...] = inclusive_prefix(self)

    zpad = jnp.zeros((CHUNK - E, CHUNK), F32)

    def to_token_major(x):
        return jnp.concatenate([x, zpad], axis=0).T[:, :E]

    def pos_scan(c, run):
        inc = incl_ref[c]
        s_c = flag_ref[c]
        sel = s_c > 0.0
        pos = jnp.where(sel, run + inc - s_c, -1.0)
        gate = jnp.where(sel, p3_ref[c], 0.0)
        off = pl.multiple_of(c * CHUNK, CHUNK)
        pos_ref[:, pl.ds(off, CHUNK)] = pos
        post_ref[pl.ds(off, CHUNK), :] = to_token_major(pos)
        gatet_ref[pl.ds(off, CHUNK), :] = to_token_major(gate)
        base_ref[c] = run.astype(jnp.int32)
        return run + jnp.broadcast_to(inc[:, CHUNK - 1:CHUNK], inc.shape)

    lax.fori_loop(0, nc, pos_scan, jnp.zeros((E, CHUNK), F32))


def _topk(probs3, capacity):
    nc = probs3.shape[0]
    nt = nc * CHUNK
    kern = functools.partial(_topk_kernel, capacity=capacity, nc=nc)
    return pl.pallas_call(
        kern,
        out_shape=[
            jax.ShapeDtypeStruct((N_EXPERTS, nt), F32),
            jax.ShapeDtypeStruct((nt, N_EXPERTS), F32),
            jax.ShapeDtypeStruct((nt, N_EXPERTS), F32),
            jax.ShapeDtypeStruct((nc, N_EXPERTS, CHUNK), jnp.int32),
        ],
        scratch_shapes=[
            pltpu.VMEM((nc, N_EXPERTS, CHUNK), F32),
            pltpu.VMEM((nc, N_EXPERTS, CHUNK), F32),
        ],
        compiler_params=pltpu.CompilerParams(vmem_limit_bytes=VMEM_LIMIT),
        name="topk",
    )(probs3)


def _gather_kernel(base_ref, cnt_ref, pos_ref, m_ref, xc_ref, *, nc, sub):
    e = pl.program_id(0)
    sb = pl.program_id(1)

    @pl.when(sb == 0)
    def _():
        xc_ref[...] = jnp.zeros_like(xc_ref)

    erow = lax.broadcasted_iota(jnp.int32, (N_EXPERTS, CHUNK), 0)

    def body(u, carry):
        idx = e * nc + sb * sub + u
        a = pl.multiple_of((base_ref[idx] // BF16_ROWS) * BF16_ROWS, BF16_ROWS)
        off = pl.multiple_of(u * CHUNK, CHUNK)
        pblk = pos_ref[:, pl.ds(off, CHUNK)]
        prow = jnp.sum(jnp.where(erow == e, pblk, 0.0), axis=0, keepdims=True)
        prow = prow - a.astype(F32)

        def copy_rows(rows):
            row = lax.broadcasted_iota(jnp.int32, (rows, CHUNK), 0).astype(F32)
            onehot = jnp.where(row == prow, 1.0, 0.0).astype(BF16)
            g = jnp.dot(onehot, m_ref[pl.ds(off, CHUNK), :], preferred_element_type=F32)
            xc_ref[0, pl.ds(a, rows), :] += g.astype(BF16)

        few = cnt_ref[idx] <= FAST_CNT
        pl.when(few)(lambda: copy_rows(FAST_ROWS))
        pl.when(jnp.logical_not(few))(lambda: copy_rows(WIN_ROWS))
        return carry

    lax.fori_loop(0, sub, body, 0)


def _gather(base_flat, cnt_flat, pos, m, cp):
    nt = pos.shape[1]
    nc = nt // CHUNK
    sub = _largest_divisor(nc, 17)
    ts = sub * CHUNK
    kern = functools.partial(_gather_kernel, nc=nc, sub=sub)
    return pl.pallas_call(
        kern,
        grid_spec=pltpu.PrefetchScalarGridSpec(
            num_scalar_prefetch=2,
            grid=(N_EXPERTS, nc // sub),
            in_specs=[
                pl.BlockSpec((N_EXPERTS, ts), lambda e, s, base, cnt: (0, s)),
                pl.BlockSpec((ts, D_MODEL), lambda e, s, base, cnt: (s, 0)),
            ],
            out_specs=pl.BlockSpec((1, cp, D_MODEL), lambda e, s, base, cnt: (e, 0, 0)),
        ),
        out_shape=jax.ShapeDtypeStruct((N_EXPERTS, cp, D_MODEL), BF16),
        compiler_params=_cparams(("parallel", "arbitrary")),
        name="gather",
    )(base_flat, cnt_flat, pos, m)


def _ffn_kernel(x_ref, wg_ref, wu_ref, wd_ref, y_ref, acc_ref, *, rows):
    f = pl.program_id(1)
    nf = pl.num_programs(1)

    @pl.when(f == 0)
    def _():
        acc_ref[...] = jnp.zeros_like(acc_ref)

    x = x_ref[0, :rows, :]
    g = jnp.dot(x, wg_ref[0].astype(BF16), preferred_element_type=F32)
    u = jnp.dot(x, wu_ref[0].astype(BF16), preferred_element_type=F32)
    hid = g * jax.nn.sigmoid(g) * u
    valid = D_FF - f * FF_TILE
    col = lax.broadcasted_iota(jnp.int32, (1, FF_TILE), 1)
    hid = jnp.where(col < valid, hid, 0.0).astype(BF16)
    wrow = lax.broadcasted_iota(jnp.int32, (FF_TILE, 1), 0)
    wd = jnp.where(wrow < valid, wd_ref[0], 0.0).astype(BF16)
    acc_ref[...] += jnp.dot(hid, wd, preferred_element_type=F32)

    @pl.when(f == nf - 1)
    def _():
        y_ref[0, :rows, :] = acc_ref[...].astype(BF16)
        y_ref[0, rows:, :] = jnp.zeros((y_ref.shape[1] - rows, D_MODEL), BF16)


def _ffn(xc, w_gate, w_up, w_down, rows):
    cp = xc.shape[1]
    nf = pl.cdiv(D_FF, FF_TILE)
    kern = functools.partial(_ffn_kernel, rows=rows)
    return pl.pallas_call(
        kern,
        grid=(N_EXPERTS, nf),
        in_specs=[
            pl.BlockSpec((1, cp, D_MODEL), lambda e, f: (e, 0, 0)),
            pl.BlockSpec((1, D_MODEL, FF_TILE), lambda e, f: (e, 0, f)),
            pl.BlockSpec((1, D_MODEL, FF_TILE), lambda e, f: (e, 0, f)),
            pl.BlockSpec((1, FF_TILE, D_MODEL), lambda e, f: (e, f, 0)),
        ],
        out_specs=pl.BlockSpec((1, cp, D_MODEL), lambda e, f: (e, 0, 0)),
        out_shape=jax.ShapeDtypeStruct((N_EXPERTS, cp, D_MODEL), BF16),
        scratch_shapes=[pltpu.VMEM((rows, D_MODEL), F32)],
        compiler_params=_cparams(("parallel", "arbitrary")),
        name="ffn",
    )(xc, w_gate, w_up, w_down)


def _combine_kernel(base_ref, fast_ref, h2_ref, post_ref, gatet_ref, g_ref, expand_ref, yc_ref, o_ref,
                    win_ref, big_ref, acc_ref, carry_ref, sem_ref, big_sem, *, nc, cpb, out_blocks):
    E = N_EXPERTS
    b = pl.program_id(0)
    c = pl.program_id(1)
    step = b * cpb + c
    nsteps = pl.num_programs(0) * cpb
    slot = step % 2

    def window_start(chunk, e):
        return pl.multiple_of((base_ref[e * nc + chunk] // BF16_ROWS) * BF16_ROWS, BF16_ROWS)

    def fast_copy(chunk, e, sl):
        return pltpu.make_async_copy(yc_ref.at[e, pl.ds(window_start(chunk, e), FAST_ROWS), :],
                                     win_ref.at[sl, pl.ds(e * FAST_ROWS, FAST_ROWS), :], sem_ref.at[sl, e])

    @pl.when((step == 0) & (fast_ref[0] == 1))
    def _():
        for e in range(E):
            fast_copy(0, e, 0).start()

    nxt = jnp.minimum(step + 1, nsteps - 1)

    @pl.when((step + 1 < nsteps) & (fast_ref[nxt] == 1))
    def _():
        for e in range(E):
            fast_copy(step + 1, e, 1 - slot).start()

    is_fast = fast_ref[step] == 1

    @pl.when(is_fast)
    def _():
        for e in range(E):
            fast_copy(step, e, slot).wait()
        lane_e = lax.broadcasted_iota(jnp.int32, (1, E), 1)
        a_vec = jnp.zeros((1, E), F32)
        for e in range(E):
            a_vec = jnp.where(lane_e == e, window_start(step, e).astype(F32), a_vec)
        pos = post_ref[...]
        rel = jnp.where(pos >= 0.0, pos - a_vec, -1.0).astype(BF16)
        gate = gatet_ref[...]
        g_hi = gate.astype(BF16)
        g_lo = (gate - g_hi.astype(F32)).astype(BF16)
        expand = expand_ref[...]
        spread = lambda x: jnp.dot(x, expand, preferred_element_type=F32)
        slot_row = (lax.broadcasted_iota(jnp.int32, (1, E * FAST_ROWS), 1) % FAST_ROWS).astype(F32)
        match = spread(rel) == slot_row
        w = win_ref[slot]
        acc_ref[...] = (h2_ref[...]
                        + jnp.dot(jnp.where(match, spread(g_hi), 0.0).astype(BF16), w, preferred_element_type=F32)
                        + jnp.dot(jnp.where(match, spread(g_lo), 0.0).astype(BF16), w, preferred_element_type=F32))

    @pl.when(jnp.logical_not(is_fast))
    def _():
        acc = h2_ref[...]
        col = lax.broadcasted_iota(jnp.int32, (CHUNK, WIN_ROWS), 1).astype(F32)
        for e in range(E):
            a = window_start(step, e)
            copy = pltpu.make_async_copy(yc_ref.at[e, pl.ds(a, WIN_ROWS), :], big_ref, big_sem)
            copy.start()
            copy.wait()
            onehot = jnp.where(col == post_ref[:, e:e + 1] - a.astype(F32), 1.0, 0.0).astype(BF16)
            acc = acc + gatet_ref[:, e:e + 1] * jnp.dot(onehot, big_ref[...], preferred_element_type=F32)
        acc_ref[...] = acc

    acc = acc_ref[...]
    ms = jnp.mean(acc * acc, axis=-1, keepdims=True)
    y = acc * lax.rsqrt(ms + RMS_EPS) * g_ref[...]

    @pl.when((c >= 1) & (c <= out_blocks))
    def _():
        o_ref[0, :CHUNK - N_META, :] = carry_ref[N_META:, :]
        o_ref[0, CHUNK - N_META:, :] = y[:N_META, :]

    carry_ref[...] = y


def _combine(base_flat, fast, h2, post, gatet, g, yc, batch, P, S):
    nt = h2.shape[0]
    nc = nt // CHUNK
    cpb = P // CHUNK
    out_blocks = S // CHUNK
    stacked = N_EXPERTS * FAST_ROWS
    expand = jnp.asarray(np.arange(stacked)[None, :] // FAST_ROWS == np.arange(N_EXPERTS)[:, None], BF16)
    kern = functools.partial(_combine_kernel, nc=nc, cpb=cpb, out_blocks=out_blocks)
    row_blk = lambda b, c, base, fast: (b * cpb + c, 0)
    return pl.pallas_call(
        kern,
        grid_spec=pltpu.PrefetchScalarGridSpec(
            num_scalar_prefetch=2,
            grid=(batch, cpb),
            in_specs=[
                pl.BlockSpec((CHUNK, D_MODEL), row_blk),
                pl.BlockSpec((CHUNK, N_EXPERTS), row_blk),
                pl.BlockSpec((CHUNK, N_EXPERTS), row_blk),
                pl.BlockSpec((1, D_MODEL), lambda b, c, base, fast: (0, 0)),
                pl.BlockSpec((N_EXPERTS, stacked), lambda b, c, base, fast: (0, 0)),
                pl.BlockSpec(memory_space=pl.ANY),
            ],
            out_specs=pl.BlockSpec(
                (1, CHUNK, D_MODEL), lambda b, c, base, fast: (b, jnp.clip(c - 1, 0, out_blocks - 1), 0)),
            scratch_shapes=[
                pltpu.VMEM((2, stacked, D_MODEL), BF16),
                pltpu.VMEM((WIN_ROWS, D_MODEL), BF16),
                pltpu.VMEM((CHUNK, D_MODEL), F32),
                pltpu.VMEM((CHUNK, D_MODEL), F32),
                pltpu.SemaphoreType.DMA((2, N_EXPERTS)),
                pltpu.SemaphoreType.DMA,
            ],
        ),
        out_shape=jax.ShapeDtypeStruct((batch, S, D_MODEL), F32),
        compiler_params=_cparams(("arbitrary", "arbitrary")),
        name="combine",
    )(base_flat, fast, h2, post, gatet, g, expand, yc)


def _swa_head_perm():
    perm = np.arange(SWA_WIDTH).reshape(SWA_KV_HEADS, SWA_GROUP, HEAD_DIM)
    return perm.transpose(1, 0, 2).reshape(-1)


def _prep_params(rel_bias, w_in, w_out):
    perm = _swa_head_perm()
    scale = np.ones((IN_COLS,), np.float32)
    scale[:DIFF_WIDTH] = HEAD_DIM ** -0.5
    scale[3 * DIFF_WIDTH:3 * DIFF_WIDTH + SWA_WIDTH] = HEAD_DIM ** -0.5
    cols = np.arange(IN_COLS)
    cols[3 * DIFF_WIDTH:3 * DIFF_WIDTH + SWA_WIDTH] = 3 * DIFF_WIDTH + perm
    w_in_p = (w_in * scale)[:, cols].astype(BF16)
    rows = np.arange(MIX_WIDTH)
    rows[DIFF_WIDTH:] = DIFF_WIDTH + perm
    w_out_p = w_out[rows, :].astype(BF16)

    T = SEQ_TILE
    table = rel_bias.astype(F32)

    def lookup(idx, cols):
        onehot = (idx[..., None] == jnp.arange(N_BUCKETS)).astype(F32)
        return jnp.einsum("...b,bc->...c", onehot, table[:, cols], precision=lax.Precision.HIGHEST)

    kk = jnp.arange(T)[:, None]
    qq = jnp.arange(T)[None, :]
    idx = jnp.stack([_rel_bucket(d * T + kk - qq) for d in (-2, -1, 0, 1, 2)])
    diff_tiles = lookup(idx, slice(0, 2 * DIFF_HEADS))
    diff_tiles = diff_tiles.transpose(3, 0, 1, 2).reshape(DIFF_HEADS, 2, 5, T, T) * LOG2E
    half = N_BUCKETS // 2
    diff_consts = jnp.stack([table[half - 1, 0:2 * DIFF_HEADS:2], table[N_BUCKETS - 1, 0:2 * DIFF_HEADS:2],
                             table[half - 1, 1:2 * DIFF_HEADS:2], table[N_BUCKETS - 1, 1:2 * DIFF_HEADS:2]],
                            axis=1) * LOG2E
    rel = jnp.arange(SWA_KEYS + T)[None, :] - T - jnp.arange(T)[:, None]
    swa_tiles = lookup(_rel_bucket(rel), slice(2 * DIFF_HEADS, None))
    swa_tiles = jnp.where((jnp.abs(rel) <= WINDOW)[:, :, None], swa_tiles, NEG).transpose(2, 0, 1)
    return w_in_p, w_out_p, diff_tiles, diff_consts, swa_tiles


def _plan_rows(seq_len):
    nsub = min((4, 5, 6), key=lambda n: (-(-seq_len // (n * SEQ_TILE)) * n, -n))
    return nsub, -(-seq_len // (nsub * SEQ_TILE)) * nsub * SEQ_TILE


def _trunk(x, meta_tokens, prep, attn_norm_g, lamv, subln_g, swa_sink, ffn_norm_g, w_router_t,
           w_gate, w_up, w_down, final_norm_g):
    w_in_p, w_out_p, diff_tiles, diff_consts, swa_tiles = prep
    B, S, _ = x.shape
    L = S + N_META
    nsub, P = _plan_rows(L)
    rows = B * P
    capacity = EC_FACTOR * (B * L) // N_EXPERTS
    ffn_rows = -(-capacity // BF16_ROWS) * BF16_ROWS
    cp = -(-(capacity + WIN_ROWS) // BF16_ROWS) * BF16_ROWS

    h0 = jnp.concatenate([jnp.broadcast_to(meta_tokens.astype(x.dtype)[None], (B, N_META, D_MODEL)), x,
                          jnp.zeros((B, P - L, D_MODEL), x.dtype)], axis=1).reshape(rows, D_MODEL)

    tm = SEQ_TILE * _largest_divisor(rows // SEQ_TILE, 2)
    proj, vt = _proj(h0, attn_norm_g.reshape(1, D_MODEL), w_in_p, tm)
    diff_o = _diff_attention(proj, vt, diff_consts, lamv, diff_tiles, subln_g.reshape(LANES, 1), B, P, L, nsub)
    swa_o = _swa_attention(proj, swa_sink, swa_tiles, B, P, L)
    h2, m, probs3 = _out_router(diff_o, swa_o, h0, w_out_p, ffn_norm_g.reshape(1, D_MODEL), w_router_t, B, P, L)
    pos, post, gatet, base3 = _topk(probs3, capacity)
    base_ec = base3[:, :, 0].T
    nxt = jnp.concatenate([base_ec[:, 1:], jnp.full((N_EXPERTS, 1), capacity, jnp.int32)], axis=1)
    cnt_ec = nxt - base_ec
    fast = jnp.all(cnt_ec <= FAST_CNT, axis=0).astype(jnp.int32)
    base_flat = base_ec.reshape(-1)
    xc = _gather(base_flat, cnt_ec.reshape(-1), pos, m, cp)
    yc = _ffn(xc, w_gate, w_up, w_down, ffn_rows)
    return _combine(base_flat, fast, h2, post, gatet, final_norm_g.reshape(1, D_MODEL), yc, B, P, S)


def kernel(x_prompt, x_sample, meta_tokens, rel_bias, attn_norm_g, w_in, diff_lambda_q1, diff_lambda_k1,
           diff_lambda_q2, diff_lambda_k2, diff_subln_g, swa_sink, w_out, ffn_norm_g, w_router, w_gate, w_up,
           w_down, final_norm_g):
    prep = _prep_params(rel_bias, w_in[0], w_out[0])
    lamv = jnp.stack([diff_lambda_q1[0], diff_lambda_k1[0], diff_lambda_q2[0], diff_lambda_k2[0]]).astype(F32)
    args = (meta_tokens, prep, attn_norm_g[0], lamv, diff_subln_g[0], swa_sink[0].astype(F32), ffn_norm_g[0],
            w_router[0].T, w_gate[0], w_up[0], w_down[0], final_norm_g)
    return (_trunk(x_prompt, *args), _trunk(x_sample, *args))
```

```python
import functools
import math

import jax
import jax.numpy as jnp
import numpy as np
from jax import lax
from jax.experimental import pallas as pl
from jax.experimental.pallas import tpu as pltpu

F32 = jnp.float32
BF16 = jnp.bfloat16

D_MODEL = 1024
HEAD_DIM = 64
N_META = 16
DIFF_HEADS = 4
DIFF_WIDTH = DIFF_HEADS * 2 * HEAD_DIM
SWA_Q_HEADS = 8
SWA_KV_HEADS = 2
SWA_GROUP = SWA_Q_HEADS // SWA_KV_HEADS
SWA_WIDTH = SWA_Q_HEADS * HEAD_DIM
SWA_KV_WIDTH = SWA_KV_HEADS * HEAD_DIM
MIX_WIDTH = DIFF_WIDTH + SWA_WIDTH
IN_COLS = 3 * DIFF_WIDTH + SWA_WIDTH + 2 * SWA_KV_WIDTH
WINDOW = 128
N_BUCKETS = 32
MAX_DISTANCE = 128
N_EXPERTS = 16
EC_FACTOR = 2
D_FF = 2752
RMS_EPS = 1e-6
LAM_INIT = 0.8 - 0.6 * math.exp(-0.3 * 0)

LANES = 128
BF16_ROWS = 16
VMEM_LIMIT = 56 * 1024 * 1024

SEQ_TILE = 256
SWA_KEYS = SEQ_TILE + 2 * WINDOW
FAR_GROUP = 2
FF_TILE = 688
FFN_ROW_BLOCKS = 2
CHUNK = LANES
VT_ROWS = LANES + BF16_ROWS
LOG2E = math.log2(math.e)
WIN_ROWS = CHUNK + BF16_ROWS
FAST_ROWS = 64
FAST_CNT = FAST_ROWS - BF16_ROWS
GATHER_GROUP = 4
NEG = -1e30

_NT = (((1,), (1,)), ((), ()))


def _cparams(sem):
    return pltpu.CompilerParams(dimension_semantics=sem, vmem_limit_bytes=VMEM_LIMIT)


def _largest_divisor(n, cap):
    return max(d for d in range(1, cap + 1) if n % d == 0)


def _rel_bucket(rel):
    half = N_BUCKETS // 2
    max_exact = half // 2
    n = jnp.abs(rel)
    large = max_exact + (jnp.log(jnp.maximum(n, 1).astype(F32) / max_exact)
                         / math.log(MAX_DISTANCE / max_exact) * (half - max_exact)).astype(jnp.int32)
    large = jnp.minimum(large, half - 1)
    return jnp.where(rel > 0, half, 0) + jnp.where(n < max_exact, n, large)


def _proj_kernel(h_ref, g_ref, w_ref, proj_ref, vt_ref):
    x = h_ref[...]
    ms = jnp.mean(x * x, axis=-1, keepdims=True)
    a = (x * lax.rsqrt(ms + RMS_EPS) * g_ref[...]).astype(BF16)
    proj = jnp.dot(a, w_ref[...], preferred_element_type=F32)
    proj_ref[:, :DIFF_WIDTH] = (proj[:, :DIFF_WIDTH] * LOG2E).astype(BF16)
    proj_ref[:, DIFF_WIDTH:] = proj[:, DIFF_WIDTH:].astype(BF16)
    ones = jnp.ones((VT_ROWS - LANES, x.shape[0]), BF16)
    for h in range(DIFF_HEADS):
        v = proj[:, 2 * DIFF_WIDTH + h * LANES: 2 * DIFF_WIDTH + (h + 1) * LANES]
        vt_ref[h, :LANES, :] = v.T.astype(BF16)
        vt_ref[h, LANES:, :] = ones


def _proj(h0, g, w_in_bf16, tm):
    rows = h0.shape[0]
    return pl.pallas_call(
        _proj_kernel,
        grid=(rows // tm,),
        in_specs=[
            pl.BlockSpec((tm, D_MODEL), lambda i: (i, 0)),
            pl.BlockSpec((1, D_MODEL), lambda i: (0, 0)),
            pl.BlockSpec((D_MODEL, IN_COLS), lambda i: (0, 0)),
        ],
        out_specs=[
            pl.BlockSpec((tm, IN_COLS), lambda i: (i, 0)),
            pl.BlockSpec((DIFF_HEADS, VT_ROWS, tm), lambda i: (0, 0, i)),
        ],
        out_shape=[
            jax.ShapeDtypeStruct((rows, IN_COLS), BF16),
            jax.ShapeDtypeStruct((DIFF_HEADS, VT_ROWS, rows), BF16),
        ],
        compiler_params=_cparams(("parallel",)),
        name="proj",
    )(h0, g, w_in_bf16)


def _diff_kernel(consts_ref, lamv_ref, q_ref, k_ref, vt_ref, bias_ref, g_ref, o_ref,
                 acc_ref, m_ref, *, seq_len, nkv, nsub):
    T = SEQ_TILE
    h = pl.program_id(1)
    i = pl.program_id(2)
    q = q_ref[...]
    lane = lax.broadcasted_iota(jnp.int32, q.shape, 1)
    zero = jnp.zeros_like(q)
    q_maps = (jnp.where(lane < HEAD_DIM, q, zero), jnp.where(lane >= HEAD_DIM, q, zero))

    acc_ref[...] = jnp.zeros_like(acc_ref)
    m_ref[...] = jnp.full_like(m_ref, NEG)

    def chunks(j0, count, *, tile, side, mask):
        maps = range(2)
        c = [0.0, 0.0] if tile else [consts_ref[h, 2 * mp + side] for mp in maps]
        offs = [pl.multiple_of((j0 + t) * T, T) for t in range(count)]
        s = [[lax.dot_general(k_ref[pl.ds(off, T), :], q_maps[mp], _NT, preferred_element_type=F32)
              for mp in maps] for off in offs]
        m_run = [m_ref[mp] for mp in maps]
        for t, off in enumerate(offs):
            vb = vt_ref[0, :, pl.ds(off, T)]
            if mask:
                kvalid = (off + lax.broadcasted_iota(jnp.int32, (T, 1), 0)) < seq_len
            for mp in maps:
                st = s[t][mp]
                if tile:
                    st = jnp.concatenate(
                        [st[:, u * T:(u + 1) * T] + bias_ref[0, mp, jnp.clip(j0 + t - (i * nsub + u), -2, 2) + 2]
                         for u in range(nsub)], axis=1)
                if mask:
                    st = jnp.where(kvalid, st, NEG)
                m_new = jnp.maximum(m_run[mp], jnp.max(st, axis=0, keepdims=True) + c[mp])
                alpha = jnp.exp2(m_run[mp] - m_new)
                m_run[mp] = m_new
                p = jnp.exp2(st - (m_new - c[mp])).astype(BF16)
                acc_ref[mp] = alpha * acc_ref[mp] + jnp.dot(vb, p, preferred_element_type=F32)
        for mp in maps:
            m_ref[mp] = m_run[mp]

    def loop(lo, hi, group, **kw):
        ngroups = jnp.maximum(hi - lo, 0) // group

        def grouped(g, carry):
            chunks(lo + g * group, group, **kw)
            return carry

        def single(j, carry):
            chunks(j, 1, **kw)
            return carry

        lax.fori_loop(0, ngroups, grouped, 0)
        if group > 1:
            lax.fori_loop(lo + ngroups * group, hi, single, 0)

    near_lo = jnp.maximum(i * nsub - 1, 0)
    near_hi = jnp.minimum((i + 1) * nsub + 1, nkv)
    loop(0, near_lo, FAR_GROUP, tile=False, side=0, mask=False)
    loop(near_lo, near_hi, 1, tile=True, side=0, mask=True)
    loop(near_hi, nkv - 1, FAR_GROUP, tile=False, side=1, mask=False)
    loop(jnp.maximum(near_hi, nkv - 1), nkv, 1, tile=False, side=1, mask=True)

    lamv = lamv_ref[...]
    lam = (jnp.exp(jnp.sum(lamv[0:1] * lamv[1:2], axis=-1, keepdims=True))
           - jnp.exp(jnp.sum(lamv[2:3] * lamv[3:4], axis=-1, keepdims=True)) + LAM_INIT)
    o = (acc_ref[0, :LANES] / acc_ref[0, LANES:LANES + 1]
         - lam * (acc_ref[1, :LANES] / acc_ref[1, LANES:LANES + 1]))
    ms = jnp.mean(o * o, axis=0, keepdims=True)
    y = o * lax.rsqrt(ms + RMS_EPS) * g_ref[...] * (1.0 - LAM_INIT)
    o_ref[...] = y.T.astype(BF16)


def _diff_attention(proj, vt, consts, lamv, bias_t, subln_g, batch, P, seq_len, nsub):
    T = SEQ_TILE
    tq = nsub * T
    nq = P // tq
    nkv = pl.cdiv(seq_len, T)
    kern = functools.partial(_diff_kernel, seq_len=seq_len, nkv=nkv, nsub=nsub)
    return pl.pallas_call(
        kern,
        grid=(batch, DIFF_HEADS, nq),
        in_specs=[
            pl.BlockSpec(memory_space=pltpu.SMEM),
            pl.BlockSpec((4, HEAD_DIM), lambda b, h, i: (0, 0)),
            pl.BlockSpec((tq, LANES), lambda b, h, i: (b * nq + i, h)),
            pl.BlockSpec((P, LANES), lambda b, h, i: (b, DIFF_HEADS + h)),
            pl.BlockSpec((1, VT_ROWS, P), lambda b, h, i: (h, 0, b)),
            pl.BlockSpec((1, 2, 5, T, T), lambda b, h, i: (h, 0, 0, 0, 0)),
            pl.BlockSpec((LANES, 1), lambda b, h, i: (0, 0)),
        ],
        out_specs=pl.BlockSpec((tq, LANES), lambda b, h, i: (b * nq + i, h)),
        out_shape=jax.ShapeDtypeStruct((batch * P, DIFF_WIDTH), BF16),
        scratch_shapes=[
            pltpu.VMEM((2, VT_ROWS, tq), F32),
            pltpu.VMEM((2, 1, tq), F32),
        ],
        compiler_params=_cparams(("parallel", "parallel", "parallel")),
        name="diff_attn",
    )(consts, lamv, proj, proj, vt, bias_t, subln_g)


def _swa_kernel(sink_ref, q_ref, k_ref, v_ref, bias_ref, o_ref, *, seq_len, P):
    T = SEQ_TILE
    i = pl.program_id(1)
    start = pl.multiple_of(jnp.clip(i * T - WINDOW, 0, P - SWA_KEYS), LANES)
    boff = pl.multiple_of(start - i * T + T, LANES)
    kw = k_ref[pl.ds(start, SWA_KEYS), :]
    vw = v_ref[pl.ds(start, SWA_KEYS), :]
    kvalid = (start + lax.broadcasted_iota(jnp.int32, (1, SWA_KEYS), 1)) < seq_len
    lane = lax.broadcasted_iota(jnp.int32, (T, LANES), 1)
    for j in range(SWA_GROUP):
        qp = q_ref[:, j * LANES:(j + 1) * LANES]
        zero = jnp.zeros_like(qp)
        outs = []
        for kvh in range(SWA_KV_HEADS):
            head = kvh * SWA_GROUP + j
            in_half = (lane < HEAD_DIM) if kvh == 0 else (lane >= HEAD_DIM)
            qe = jnp.where(in_half, qp, zero)
            s = lax.dot_general(qe, kw, _NT, preferred_element_type=F32)
            s = s + bias_ref[head, :, pl.ds(boff, SWA_KEYS)]
            s = jnp.where(kvalid, s, NEG)
            sink = sink_ref[head]
            m = jnp.maximum(jnp.max(s, axis=-1, keepdims=True), sink)
            p = jnp.exp(s - m)
            l = jnp.sum(p, axis=-1, keepdims=True) + jnp.exp(sink - m)
            outs.append(jnp.dot(p.astype(BF16), vw, preferred_element_type=F32) / l)
        o_ref[:, j * LANES:(j + 1) * LANES] = jnp.where(lane < HEAD_DIM, outs[0], outs[1]).astype(BF16)


def _swa_attention(proj, sink, bias_w, batch, P, seq_len):
    T = SEQ_TILE
    nq = P // T
    q_blk = (3 * DIFF_WIDTH) // SWA_WIDTH
    k_blk = (3 * DIFF_WIDTH + SWA_WIDTH) // LANES
    kern = functools.partial(_swa_kernel, seq_len=seq_len, P=P)
    return pl.pallas_call(
        kern,
        grid=(batch, nq),
        in_specs=[
            pl.BlockSpec(memory_space=pltpu.SMEM),
            pl.BlockSpec((T, SWA_WIDTH), lambda b, i: (b * nq + i, q_blk)),
            pl.BlockSpec((P, LANES), lambda b, i: (b, k_blk)),
            pl.BlockSpec((P, LANES), lambda b, i: (b, k_blk + 1)),
            pl.BlockSpec((SWA_Q_HEADS, T, SWA_KEYS + T), lambda b, i: (0, 0, 0)),
        ],
        out_specs=pl.BlockSpec((T, SWA_WIDTH), lambda b, i: (b * nq + i, 0)),
        out_shape=jax.ShapeDtypeStruct((batch * P, SWA_WIDTH), BF16),
        compiler_params=_cparams(("parallel", "parallel")),
        name="swa_attn",
    )(sink, proj, proj, proj, bias_w)


def _out_kernel(do_ref, so_ref, h_ref, w_ref, g_ref, wr_ref, h2_ref, m_ref, p3_ref, *, seq_len, tm):
    i = pl.program_id(1)
    h2 = (h_ref[...]
          + jnp.dot(do_ref[...], w_ref[:DIFF_WIDTH, :], preferred_element_type=F32)
          + jnp.dot(so_ref[...], w_ref[DIFF_WIDTH:, :], preferred_element_type=F32))
    h2_ref[...] = h2
    ms = jnp.mean(h2 * h2, axis=-1, keepdims=True)
    mf = h2 * lax.rsqrt(ms + RMS_EPS) * g_ref[...]
    m_ref[...] = mf.astype(BF16)
    logits = lax.dot_general(wr_ref[...], mf, _NT, preferred_element_type=F32,
                             precision=lax.Precision.HIGHEST)
    e = jnp.exp(logits - jnp.max(logits, axis=0, keepdims=True))
    probs = e / jnp.sum(e, axis=0, keepdims=True)
    pos = i * tm + lax.broadcasted_iota(jnp.int32, (1, tm), 1)
    probs = jnp.where(pos < seq_len, probs, -1.0)
    for c in range(tm // CHUNK):
        p3_ref[c] = probs[:, c * CHUNK:(c + 1) * CHUNK]


def _out_router(diff_o, swa_o, h0, w_out_bf16, g, w_router_t, batch, P, seq_len):
    tm = SEQ_TILE
    nq = P // tm
    rows = batch * P
    kern = functools.partial(_out_kernel, seq_len=seq_len, tm=tm)
    return pl.pallas_call(
        kern,
        grid=(batch, nq),
        in_specs=[
            pl.BlockSpec((tm, DIFF_WIDTH), lambda b, i: (b * nq + i, 0)),
            pl.BlockSpec((tm, SWA_WIDTH), lambda b, i: (b * nq + i, 0)),
            pl.BlockSpec((tm, D_MODEL), lambda b, i: (b * nq + i, 0)),
            pl.BlockSpec((MIX_WIDTH, D_MODEL), lambda b, i: (0, 0)),
            pl.BlockSpec((1, D_MODEL), lambda b, i: (0, 0)),
            pl.BlockSpec((N_EXPERTS, D_MODEL), lambda b, i: (0, 0)),
        ],
        out_specs=[
            pl.BlockSpec((tm, D_MODEL), lambda b, i: (b * nq + i, 0)),
            pl.BlockSpec((tm, D_MODEL), lambda b, i: (b * nq + i, 0)),
            pl.BlockSpec((tm // CHUNK, N_EXPERTS, CHUNK), lambda b, i: (b * nq + i, 0, 0)),
        ],
        out_shape=[
            jax.ShapeDtypeStruct((rows, D_MODEL), F32),
            jax.ShapeDtypeStruct((rows, D_MODEL), BF16),
            jax.ShapeDtypeStruct((rows // CHUNK, N_EXPERTS, CHUNK), F32),
        ],
        compiler_params=_cparams(("parallel", "parallel")),
        name="out_router",
    )(diff_o, swa_o, h0, w_out_bf16, g, w_router_t)


def _topk_kernel(p3_ref, pos_ref, post_ref, gatet_ref, base_ref, incl_ref, flag_ref, *, capacity, nc):
    E = N_EXPERTS
    probs = p3_ref[...]
    bits = pltpu.bitcast(probs, jnp.int32)

    def count(pred):
        part = jnp.sum(jnp.where(pred, 1.0, 0.0), axis=0, keepdims=True)
        return jnp.broadcast_to(jnp.sum(part, axis=-1, keepdims=True), part.shape)

    def bisect(_, lohi):
        lo, hi = lohi
        mid = lo + ((hi - lo + 1) >> 1)
        ok = count(bits >= mid) >= capacity
        return jnp.where(ok, mid, lo), jnp.where(ok, hi, mid - 1)

    lo0 = jnp.zeros((1, E, CHUNK), jnp.int32)
    hi0 = jnp.full((1, E, CHUNK), 0x7F800000, jnp.int32)
    thr, _ = lax.fori_loop(0, 32, bisect, (lo0, hi0))
    gt = bits > thr
    eq = bits == thr
    need = capacity - count(gt)[0]

    tri = (lax.broadcasted_iota(jnp.int32, (CHUNK, CHUNK), 0)
           <= lax.broadcasted_iota(jnp.int32, (CHUNK, CHUNK), 1)).astype(BF16)

    def inclusive_prefix(flags):
        f2 = flags.astype(BF16).reshape(nc * E, CHUNK)
        return jnp.dot(f2, tri, preferred_element_type=F32).reshape(nc, E, CHUNK)

    eqf = jnp.where(eq, 1.0, 0.0)
    incl_ref[...] = inclusive_prefix(eqf)
    flag_ref[...] = eqf

    def tie_scan(c, run):
        inc = incl_ref[c]
        e_c = flag_ref[c]
        take = (e_c > 0.0) & ((run + inc - e_c) < need)
        flag_ref[c] = jnp.where(take, 1.0, 0.0)
        return run + jnp.broadcast_to(inc[:, CHUNK - 1:CHUNK], inc.shape)

    lax.fori_loop(0, nc, tie_scan, jnp.zeros((E, CHUNK), F32))
    self = jnp.where(gt, 1.0, flag_ref[...])
    flag_ref[...] = self
    incl_ref[...] = inclusive_prefix(self)

    zpad = jnp.zeros((CHUNK - E, CHUNK), F32)

    def to_token_major(x):
        return jnp.concatenate([x, zpad], axis=0).T[:, :E]

    def pos_scan(c, run):
        inc = incl_ref[c]
        s_c = flag_ref[c]
        sel = s_c > 0.0
        pos = jnp.where(sel, run + inc - s_c, -1.0)
        gate = jnp.where(sel, p3_ref[c], 0.0)
        off = pl.multiple_of(c * CHUNK, CHUNK)
        pos_ref[:, pl.ds(off, CHUNK)] = pos
        post_ref[pl.ds(off, CHUNK), :] = to_token_major(pos)
        gatet_ref[pl.ds(off, CHUNK), :] = to_token_major(gate)
        base_ref[c] = run.astype(jnp.int32)
        return run + jnp.broadcast_to(inc[:, CHUNK - 1:CHUNK], inc.shape)

    lax.fori_loop(0, nc, pos_scan, jnp.zeros((E, CHUNK), F32))


def _topk(probs3, capacity):
    nc = probs3.shape[0]
    nt = nc * CHUNK
    kern = functools.partial(_topk_kernel, capacity=capacity, nc=nc)
    return pl.pallas_call(
        kern,
        out_shape=[
            jax.ShapeDtypeStruct((N_EXPERTS, nt), F32),
            jax.ShapeDtypeStruct((nt, N_EXPERTS), F32),
            jax.ShapeDtypeStruct((nt, N_EXPERTS), F32),
            jax.ShapeDtypeStruct((nc, N_EXPERTS, CHUNK), jnp.int32),
        ],
        scratch_shapes=[
            pltpu.VMEM((nc, N_EXPERTS, CHUNK), F32),
            pltpu.VMEM((nc, N_EXPERTS, CHUNK), F32),
        ],
        compiler_params=pltpu.CompilerParams(vmem_limit_bytes=VMEM_LIMIT),
        name="topk",
    )(probs3)


def _gather_kernel(base_ref, cnt_ref, pos_ref, m_ref, xc_ref, *, nc, sub):
    e = pl.program_id(0)
    sb = pl.program_id(1)

    @pl.when(sb == 0)
    def _():
        xc_ref[...] = jnp.zeros_like(xc_ref)

    first = e * nc + sb * sub

    def copy_rows(u, k, rows):
        erow = lax.broadcasted_iota(jnp.int32, (N_EXPERTS, CHUNK), 0)
        row = lax.broadcasted_iota(jnp.int32, (rows, CHUNK), 0).astype(F32)
        starts = [pl.multiple_of((base_ref[first + u + t] // BF16_ROWS) * BF16_ROWS, BF16_ROWS) for t in range(k)]
        offs = [pl.multiple_of((u + t) * CHUNK, CHUNK) for t in range(k)]
        prow = [jnp.sum(jnp.where(erow == e, pos_ref[:, pl.ds(off, CHUNK)], 0.0), axis=0, keepdims=True)
                - a.astype(F32) for a, off in zip(starts, offs)]
        onehot = [jnp.where(row == p, 1.0, 0.0).astype(BF16) for p in prow]
        g = [jnp.dot(oh, m_ref[pl.ds(off, CHUNK), :], preferred_element_type=F32)
             for oh, off in zip(onehot, offs)]
        for a, gt in zip(starts, g):
            xc_ref[0, pl.ds(a, rows), :] += gt.astype(BF16)

    def single(u):
        few = cnt_ref[first + u] <= FAST_CNT
        pl.when(few)(lambda: copy_rows(u, 1, FAST_ROWS))
        pl.when(jnp.logical_not(few))(lambda: copy_rows(u, 1, WIN_ROWS))

    def group(gi, carry):
        u0 = gi * GATHER_GROUP
        most = functools.reduce(jnp.maximum, [cnt_ref[first + u0 + t] for t in range(GATHER_GROUP)])
        few = most <= FAST_CNT
        pl.when(few)(lambda: copy_rows(u0, GATHER_GROUP, FAST_ROWS))
        pl.when(jnp.logical_not(few))(lambda: copy_rows(u0, GATHER_GROUP, WIN_ROWS))
        return carry

    ngroups = sub // GATHER_GROUP
    lax.fori_loop(0, ngroups, group, 0)
    for u in range(ngroups * GATHER_GROUP, sub):
        single(u)


def _gather(base_flat, cnt_flat, pos, m, cp):
    nt = pos.shape[1]
    nc = nt // CHUNK
    sub = _largest_divisor(nc, 17)
    ts = sub * CHUNK
    kern = functools.partial(_gather_kernel, nc=nc, sub=sub)
    return pl.pallas_call(
        kern,
        grid_spec=pltpu.PrefetchScalarGridSpec(
            num_scalar_prefetch=2,
            grid=(N_EXPERTS, nc // sub),
            in_specs=[
                pl.BlockSpec((N_EXPERTS, ts), lambda e, s, base, cnt: (0, s)),
                pl.BlockSpec((ts, D_MODEL), lambda e, s, base, cnt: (s, 0)),
            ],
            out_specs=pl.BlockSpec((1, cp, D_MODEL), lambda e, s, base, cnt: (e, 0, 0)),
        ),
        out_shape=jax.ShapeDtypeStruct((N_EXPERTS, cp, D_MODEL), BF16),
        compiler_params=_cparams(("parallel", "arbitrary")),
        name="gather",
    )(base_flat, cnt_flat, pos, m)


def _ffn_kernel(x_ref, wg_ref, wu_ref, wd_ref, y_ref, acc_ref, *, rows):
    f = pl.program_id(1)
    nf = pl.num_programs(1)

    @pl.when(f == 0)
    def _():
        acc_ref[...] = jnp.zeros_like(acc_ref)

    wg = wg_ref[0].astype(BF16)
    wu = wu_ref[0].astype(BF16)
    wd = wd_ref[0].astype(BF16)
    block = -(-rows // (FFN_ROW_BLOCKS * BF16_ROWS)) * BF16_ROWS
    bounds = list(range(0, rows, block)) + [rows]
    for r0, r1 in zip(bounds[:-1], bounds[1:]):
        x = x_ref[0, r0:r1, :]
        g = lax.dot_general(x, wg, _NT, preferred_element_type=F32)
        u = lax.dot_general(x, wu, _NT, preferred_element_type=F32)
        hid = (g * jax.nn.sigmoid(g) * u).astype(BF16)
        acc_ref[r0:r1, :] += jnp.dot(hid, wd, preferred_element_type=F32)

    @pl.when(f == nf - 1)
    def _():
        y_ref[0, :rows, :] = acc_ref[...].astype(BF16)
        y_ref[0, rows:, :] = jnp.zeros((y_ref.shape[1] - rows, D_MODEL), BF16)


def _ffn(xc, w_gate_t, w_up_t, w_down, rows):
    cp = xc.shape[1]
    kern = functools.partial(_ffn_kernel, rows=rows)
    once = pl.Buffered(1)
    w_spec = pl.BlockSpec((1, FF_TILE, D_MODEL), lambda e, f: (e, f, 0))
    return pl.pallas_call(
        kern,
        grid=(N_EXPERTS, D_FF // FF_TILE),
        in_specs=[pl.BlockSpec((1, cp, D_MODEL), lambda e, f: (e, 0, 0), pipeline_mode=once), w_spec, w_spec, w_spec],
        out_specs=pl.BlockSpec((1, cp, D_MODEL), lambda e, f: (e, 0, 0), pipeline_mode=once),
        out_shape=jax.ShapeDtypeStruct((N_EXPERTS, cp, D_MODEL), BF16),
        scratch_shapes=[pltpu.VMEM((rows, D_MODEL), F32)],
        compiler_params=_cparams(("parallel", "arbitrary")),
        name="ffn",
    )(xc, w_gate_t, w_up_t, w_down)


def _combine_kernel(base_ref, fast_ref, h2_ref, post_ref, gatet_ref, g_ref, expand_ref, yc_ref, o_ref,
                    win_ref, big_ref, acc_ref, carry_ref, sem_ref, big_sem, *, nc, cpb, out_blocks):
    E = N_EXPERTS
    b = pl.program_id(0)
    c = pl.program_id(1)
    step = b * cpb + c
    nsteps = pl.num_programs(0) * cpb
    slot = step % 2

    def window_start(chunk, e):
        return pl.multiple_of((base_ref[e * nc + chunk] // BF16_ROWS) * BF16_ROWS, BF16_ROWS)

    def fast_copy(chunk, e, sl):
        return pltpu.make_async_copy(yc_ref.at[e, pl.ds(window_start(chunk, e), FAST_ROWS), :],
                                     win_ref.at[sl, pl.ds(e * FAST_ROWS, FAST_ROWS), :], sem_ref.at[sl, e])

    @pl.when((step == 0) & (fast_ref[0] == 1))
    def _():
        for e in range(E):
            fast_copy(0, e, 0).start()

    nxt = jnp.minimum(step + 1, nsteps - 1)

    @pl.when((step + 1 < nsteps) & (fast_ref[nxt] == 1))
    def _():
        for e in range(E):
            fast_copy(step + 1, e, 1 - slot).start()

    is_fast = fast_ref[step] == 1

    @pl.when(is_fast)
    def _():
        for e in range(E):
            fast_copy(step, e, slot).wait()
        lane_e = lax.broadcasted_iota(jnp.int32, (1, E), 1)
        a_vec = jnp.zeros((1, E), F32)
        for e in range(E):
            a_vec = jnp.where(lane_e == e, window_start(step, e).astype(F32), a_vec)
        pos = post_ref[...]
        rel = jnp.where(pos >= 0.0, pos - a_vec, -1.0).astype(BF16)
        gate = gatet_ref[...]
        g_hi = gate.astype(BF16)
        g_lo = (gate - g_hi.astype(F32)).astype(BF16)
        expand = expand_ref[...]
        spread = lambda x: jnp.dot(x, expand, preferred_element_type=F32)
        slot_row = (lax.broadcasted_iota(jnp.int32, (1, E * FAST_ROWS), 1) % FAST_ROWS).astype(F32)
        match = spread(rel) == slot_row
        w = win_ref[slot]
        acc_ref[...] = (h2_ref[...]
                        + jnp.dot(jnp.where(match, spread(g_hi), 0.0).astype(BF16), w, preferred_element_type=F32)
                        + jnp.dot(jnp.where(match, spread(g_lo), 0.0).astype(BF16), w, preferred_element_type=F32))

    @pl.when(jnp.logical_not(is_fast))
    def _():
        acc = h2_ref[...]
        col = lax.broadcasted_iota(jnp.int32, (CHUNK, WIN_ROWS), 1).astype(F32)
        for e in range(E):
            a = window_start(step, e)
            copy = pltpu.make_async_copy(yc_ref.at[e, pl.ds(a, WIN_ROWS), :], big_ref, big_sem)
            copy.start()
            copy.wait()
            onehot = jnp.where(col == post_ref[:, e:e + 1] - a.astype(F32), 1.0, 0.0).astype(BF16)
            acc = acc + gatet_ref[:, e:e + 1] * jnp.dot(onehot, big_ref[...], preferred_element_type=F32)
        acc_ref[...] = acc

    acc = acc_ref[...]
    ms = jnp.mean(acc * acc, axis=-1, keepdims=True)
    y = acc * lax.rsqrt(ms + RMS_EPS) * g_ref[...]

    @pl.when((c >= 1) & (c <= out_blocks))
    def _():
        o_ref[0, :CHUNK - N_META, :] = carry_ref[N_META:, :]
        o_ref[0, CHUNK - N_META:, :] = y[:N_META, :]

    carry_ref[...] = y


def _combine(base_flat, fast, h2, post, gatet, g, yc, batch, P, S):
    nt = h2.shape[0]
    nc = nt // CHUNK
    cpb = P // CHUNK
    out_blocks = S // CHUNK
    stacked = N_EXPERTS * FAST_ROWS
    expand = jnp.asarray(np.arange(stacked)[None, :] // FAST_ROWS == np.arange(N_EXPERTS)[:, None], BF16)
    kern = functools.partial(_combine_kernel, nc=nc, cpb=cpb, out_blocks=out_blocks)
    row_blk = lambda b, c, base, fast: (b * cpb + c, 0)
    return pl.pallas_call(
        kern,
        grid_spec=pltpu.PrefetchScalarGridSpec(
            num_scalar_prefetch=2,
            grid=(batch, cpb),
            in_specs=[
                pl.BlockSpec((CHUNK, D_MODEL), row_blk),
                pl.BlockSpec((CHUNK, N_EXPERTS), row_blk),
                pl.BlockSpec((CHUNK, N_EXPERTS), row_blk),
                pl.BlockSpec((1, D_MODEL), lambda b, c, base, fast: (0, 0)),
                pl.BlockSpec((N_EXPERTS, stacked), lambda b, c, base, fast: (0, 0)),
                pl.BlockSpec(memory_space=pl.ANY),
            ],
            out_specs=pl.BlockSpec(
                (1, CHUNK, D_MODEL), lambda b, c, base, fast: (b, jnp.clip(c - 1, 0, out_blocks - 1), 0)),
            scratch_shapes=[
                pltpu.VMEM((2, stacked, D_MODEL), BF16),
                pltpu.VMEM((WIN_ROWS, D_MODEL), BF16),
                pltpu.VMEM((CHUNK, D_MODEL), F32),
                pltpu.VMEM((CHUNK, D_MODEL), F32),
                pltpu.SemaphoreType.DMA((2, N_EXPERTS)),
                pltpu.SemaphoreType.DMA,
            ],
        ),
        out_shape=jax.ShapeDtypeStruct((batch, S, D_MODEL), F32),
        compiler_params=_cparams(("arbitrary", "arbitrary")),
        name="combine",
    )(base_flat, fast, h2, post, gatet, g, expand, yc)


def _swa_head_perm():
    perm = np.arange(SWA_WIDTH).reshape(SWA_KV_HEADS, SWA_GROUP, HEAD_DIM)
    return perm.transpose(1, 0, 2).reshape(-1)


def _prep_params(rel_bias, w_in, w_out):
    perm = _swa_head_perm()
    scale = np.ones((IN_COLS,), np.float32)
    scale[:DIFF_WIDTH] = HEAD_DIM ** -0.5
    scale[3 * DIFF_WIDTH:3 * DIFF_WIDTH + SWA_WIDTH] = HEAD_DIM ** -0.5
    cols = np.arange(IN_COLS)
    cols[3 * DIFF_WIDTH:3 * DIFF_WIDTH + SWA_WIDTH] = 3 * DIFF_WIDTH + perm
    w_in_p = (w_in * scale)[:, cols].astype(BF16)
    rows = np.arange(MIX_WIDTH)
    rows[DIFF_WIDTH:] = DIFF_WIDTH + perm
    w_out_p = w_out[rows, :].astype(BF16)

    T = SEQ_TILE
    table = rel_bias.astype(F32)

    def lookup(idx, cols):
        onehot = (idx[..., None] == jnp.arange(N_BUCKETS)).astype(F32)
        return jnp.einsum("...b,bc->...c", onehot, table[:, cols], precision=lax.Precision.HIGHEST)

    kk = jnp.arange(T)[:, None]
    qq = jnp.arange(T)[None, :]
    idx = jnp.stack([_rel_bucket(d * T + kk - qq) for d in (-2, -1, 0, 1, 2)])
    diff_tiles = lookup(idx, slice(0, 2 * DIFF_HEADS))
    diff_tiles = diff_tiles.transpose(3, 0, 1, 2).reshape(DIFF_HEADS, 2, 5, T, T) * LOG2E
    half = N_BUCKETS // 2
    diff_consts = jnp.stack([table[half - 1, 0:2 * DIFF_HEADS:2], table[N_BUCKETS - 1, 0:2 * DIFF_HEADS:2],
                             table[half - 1, 1:2 * DIFF_HEADS:2], table[N_BUCKETS - 1, 1:2 * DIFF_HEADS:2]],
                            axis=1) * LOG2E
    rel = jnp.arange(SWA_KEYS + T)[None, :] - T - jnp.arange(T)[:, None]
    swa_tiles = lookup(_rel_bucket(rel), slice(2 * DIFF_HEADS, None))
    swa_tiles = jnp.where((jnp.abs(rel) <= WINDOW)[:, :, None], swa_tiles, NEG).transpose(2, 0, 1)
    return w_in_p, w_out_p, diff_tiles, diff_consts, swa_tiles


def _plan_rows(seq_len):
    nsub = min((4, 5, 6), key=lambda n: (-(-seq_len // (n * SEQ_TILE)) * n, -n))
    return nsub, -(-seq_len // (nsub * SEQ_TILE)) * nsub * SEQ_TILE


def _trunk(x, meta_tokens, prep, attn_norm_g, lamv, subln_g, swa_sink, ffn_norm_g, w_router_t,
           w_gate, w_up, w_down, final_norm_g):
    w_in_p, w_out_p, diff_tiles, diff_consts, swa_tiles = prep
    B, S, _ = x.shape
    L = S + N_META
    nsub, P = _plan_rows(L)
    rows = B * P
    capacity = EC_FACTOR * (B * L) // N_EXPERTS
    ffn_rows = -(-capacity // BF16_ROWS) * BF16_ROWS
    cp = -(-(capacity + WIN_ROWS) // BF16_ROWS) * BF16_ROWS

    h0 = jnp.concatenate([jnp.broadcast_to(meta_tokens.astype(x.dtype)[None], (B, N_META, D_MODEL)), x,
                          jnp.zeros((B, P - L, D_MODEL), x.dtype)], axis=1).reshape(rows, D_MODEL)

    tm = SEQ_TILE * _largest_divisor(rows // SEQ_TILE, 2)
    proj, vt = _proj(h0, attn_norm_g.reshape(1, D_MODEL), w_in_p, tm)
    diff_o = _diff_attention(proj, vt, diff_consts, lamv, diff_tiles, subln_g.reshape(LANES, 1), B, P, L, nsub)
    swa_o = _swa_attention(proj, swa_sink, swa_tiles, B, P, L)
    h2, m, probs3 = _out_router(diff_o, swa_o, h0, w_out_p, ffn_norm_g.reshape(1, D_MODEL), w_router_t, B, P, L)
    pos, post, gatet, base3 = _topk(probs3, capacity)
    base_ec = base3[:, :, 0].T
    nxt = jnp.concatenate([base_ec[:, 1:], jnp.full((N_EXPERTS, 1), capacity, jnp.int32)], axis=1)
    cnt_ec = nxt - base_ec
    fast = jnp.all(cnt_ec <= FAST_CNT, axis=0).astype(jnp.int32)
    base_flat = base_ec.reshape(-1)
    xc = _gather(base_flat, cnt_ec.reshape(-1), pos, m, cp)
    yc = _ffn(xc, w_gate, w_up, w_down, ffn_rows)
    return _combine(base_flat, fast, h2, post, gatet, final_norm_g.reshape(1, D_MODEL), yc, B, P, S)


def kernel(x_prompt, x_sample, meta_tokens, rel_bias, attn_norm_g, w_in, diff_lambda_q1, diff_lambda_k1,
           diff_lambda_q2, diff_lambda_k2, diff_subln_g, swa_sink, w_out, ffn_norm_g, w_router, w_gate, w_up,
           w_down, final_norm_g):
    prep = _prep_params(rel_bias, w_in[0], w_out[0])
    lamv = jnp.stack([diff_lambda_q1[0], diff_lambda_k1[0], diff_lambda_q2[0], diff_lambda_k2[0]]).astype(F32)
    args = (meta_tokens, prep, attn_norm_g[0], lamv, diff_subln_g[0], swa_sink[0].astype(F32), ffn_norm_g[0],
            w_router[0].T, jnp.swapaxes(w_gate[0], 1, 2), jnp.swapaxes(w_up[0], 1, 2), w_down[0], final_norm_g)
    return (_trunk(x_prompt, *args), _trunk(x_sample, *args))
```

```python
import functools
import math

import jax
import jax.numpy as jnp
import numpy as np
from jax import lax
from jax.experimental import pallas as pl
from jax.experimental.pallas import tpu as pltpu

F32 = jnp.float32
BF16 = jnp.bfloat16

D_MODEL = 1024
HEAD_DIM = 64
N_META = 16
DIFF_HEADS = 4
DIFF_WIDTH = DIFF_HEADS * 2 * HEAD_DIM
SWA_Q_HEADS = 8
SWA_KV_HEADS = 2
SWA_GROUP = SWA_Q_HEADS // SWA_KV_HEADS
SWA_WIDTH = SWA_Q_HEADS * HEAD_DIM
SWA_KV_WIDTH = SWA_KV_HEADS * HEAD_DIM
MIX_WIDTH = DIFF_WIDTH + SWA_WIDTH
IN_COLS = 3 * DIFF_WIDTH + SWA_WIDTH + 2 * SWA_KV_WIDTH
WINDOW = 128
N_BUCKETS = 32
MAX_DISTANCE = 128
N_EXPERTS = 16
EC_FACTOR = 2
D_FF = 2752
RMS_EPS = 1e-6
LAM_INIT = 0.8 - 0.6 * math.exp(-0.3 * 0)

LANES = 128
SUBLANES = 8
BF16_ROWS = 16
VMEM_LIMIT = 56 * 1024 * 1024

SEQ_TILE = 256
SWA_KEYS = SEQ_TILE + 2 * WINDOW
FAR_GROUP = 4
FF_TILE = 688
FFN_ROW_BLOCKS = 2
CHUNK = LANES
VT_ROWS = LANES + BF16_ROWS
LOG2E = math.log2(math.e)
WIN_ROWS = CHUNK + BF16_ROWS
FAST_ROWS = 64
FAST_CNT = FAST_ROWS - BF16_ROWS
GATHER_GROUP = 4
NEG = -1e30

_NT = (((1,), (1,)), ((), ()))


def _cparams(sem):
    return pltpu.CompilerParams(dimension_semantics=sem, vmem_limit_bytes=VMEM_LIMIT)


def _largest_divisor(n, cap):
    return max(d for d in range(1, cap + 1) if n % d == 0)


def _rel_bucket(rel):
    half = N_BUCKETS // 2
    max_exact = half // 2
    n = jnp.abs(rel)
    large = max_exact + (jnp.log(jnp.maximum(n, 1).astype(F32) / max_exact)
                         / math.log(MAX_DISTANCE / max_exact) * (half - max_exact)).astype(jnp.int32)
    large = jnp.minimum(large, half - 1)
    return jnp.where(rel > 0, half, 0) + jnp.where(n < max_exact, n, large)


def _proj_kernel(h_ref, g_ref, w_ref, proj_ref, vt_ref):
    x = h_ref[...]
    ms = jnp.mean(x * x, axis=-1, keepdims=True)
    a = (x * lax.rsqrt(ms + RMS_EPS) * g_ref[...]).astype(BF16)
    proj = jnp.dot(a, w_ref[...], preferred_element_type=F32)
    q_lo, q_hi = 3 * DIFF_WIDTH, 3 * DIFF_WIDTH + SWA_WIDTH
    proj_ref[:, :DIFF_WIDTH] = (proj[:, :DIFF_WIDTH] * LOG2E).astype(BF16)
    proj_ref[:, DIFF_WIDTH:q_lo] = proj[:, DIFF_WIDTH:q_lo].astype(BF16)
    proj_ref[:, q_lo:q_hi] = (proj[:, q_lo:q_hi] * LOG2E).astype(BF16)
    proj_ref[:, q_hi:] = proj[:, q_hi:].astype(BF16)
    ones = jnp.ones((VT_ROWS - LANES, x.shape[0]), BF16)
    for h in range(DIFF_HEADS):
        v = proj[:, 2 * DIFF_WIDTH + h * LANES: 2 * DIFF_WIDTH + (h + 1) * LANES]
        vt_ref[h, :LANES, :] = v.T.astype(BF16)
        vt_ref[h, LANES:, :] = ones


def _proj(h0, g, w_in_bf16, tm):
    rows = h0.shape[0]
    return pl.pallas_call(
        _proj_kernel,
        grid=(rows // tm,),
        in_specs=[
            pl.BlockSpec((tm, D_MODEL), lambda i: (i, 0)),
            pl.BlockSpec((1, D_MODEL), lambda i: (0, 0)),
            pl.BlockSpec((D_MODEL, IN_COLS), lambda i: (0, 0)),
        ],
        out_specs=[
            pl.BlockSpec((tm, IN_COLS), lambda i: (i, 0)),
            pl.BlockSpec((DIFF_HEADS, VT_ROWS, tm), lambda i: (0, 0, i)),
        ],
        out_shape=[
            jax.ShapeDtypeStruct((rows, IN_COLS), BF16),
            jax.ShapeDtypeStruct((DIFF_HEADS, VT_ROWS, rows), BF16),
        ],
        compiler_params=_cparams(("parallel",)),
        name="proj",
    )(h0, g, w_in_bf16)


def _diff_kernel(consts_ref, lamv_ref, q_ref, k_ref, vt_ref, bias_ref, g_ref, o_ref,
                 acc_ref, m_ref, *, seq_len, nkv, nsub):
    T = SEQ_TILE
    h = pl.program_id(1)
    i = pl.program_id(2)
    q = q_ref[...]
    lane = lax.broadcasted_iota(jnp.int32, q.shape, 1)
    zero = jnp.zeros_like(q)
    q_maps = (jnp.where(lane < HEAD_DIM, q, zero), jnp.where(lane >= HEAD_DIM, q, zero))

    acc_ref[...] = jnp.zeros_like(acc_ref)
    m_ref[...] = jnp.full_like(m_ref, NEG)

    def chunks(j0, count, *, tile, side, mask):
        maps = range(2)
        c = [0.0, 0.0] if tile else [consts_ref[h, 2 * mp + side] for mp in maps]
        offs = [pl.multiple_of((j0 + t) * T, T) for t in range(count)]
        s = [[lax.dot_general(k_ref[pl.ds(off, T), :], q_maps[mp], _NT, preferred_element_type=F32)
              for mp in maps] for off in offs]
        m_run = [m_ref[mp] for mp in maps]
        for t, off in enumerate(offs):
            vb = vt_ref[0, :, pl.ds(off, T)]
            if mask:
                kvalid = (off + lax.broadcasted_iota(jnp.int32, (T, 1), 0)) < seq_len
            for mp in maps:
                st = s[t][mp]
                if tile:
                    st = jnp.concatenate(
                        [st[:, u * T:(u + 1) * T] + bias_ref[0, mp, jnp.clip(j0 + t - (i * nsub + u), -2, 2) + 2]
                         for u in range(nsub)], axis=1)
                if mask:
                    st = jnp.where(kvalid, st, NEG)
                slab = functools.reduce(jnp.maximum, [st[r:r + SUBLANES] for r in range(0, T, SUBLANES)])
                m_new = jnp.maximum(m_run[mp], jnp.max(slab, axis=0, keepdims=True) + c[mp])
                alpha = jnp.exp2(m_run[mp] - m_new)
                m_run[mp] = m_new
                p = jnp.exp2(st - (m_new - c[mp])).astype(BF16)
                acc_ref[mp] = alpha * acc_ref[mp] + jnp.dot(vb, p, preferred_element_type=F32)
        for mp in maps:
            m_ref[mp] = m_run[mp]

    def loop(lo, hi, group, **kw):
        ngroups = jnp.maximum(hi - lo, 0) // group

        def grouped(g, carry):
            chunks(lo + g * group, group, **kw)
            return carry

        def single(j, carry):
            chunks(j, 1, **kw)
            return carry

        lax.fori_loop(0, ngroups, grouped, 0)
        if group > 1:
            lax.fori_loop(lo + ngroups * group, hi, single, 0)

    near_lo = jnp.maximum(i * nsub - 1, 0)
    near_hi = jnp.minimum((i + 1) * nsub + 1, nkv)
    loop(0, near_lo, FAR_GROUP, tile=False, side=0, mask=False)
    loop(near_lo, near_hi, 1, tile=True, side=0, mask=True)
    loop(near_hi, nkv - 1, FAR_GROUP, tile=False, side=1, mask=False)
    loop(jnp.maximum(near_hi, nkv - 1), nkv, 1, tile=False, side=1, mask=True)

    lamv = lamv_ref[...]
    lam = (jnp.exp(jnp.sum(lamv[0:1] * lamv[1:2], axis=-1, keepdims=True))
           - jnp.exp(jnp.sum(lamv[2:3] * lamv[3:4], axis=-1, keepdims=True)) + LAM_INIT)
    o = (acc_ref[0, :LANES] / acc_ref[0, LANES:LANES + 1]
         - lam * (acc_ref[1, :LANES] / acc_ref[1, LANES:LANES + 1]))
    ms = jnp.mean(o * o, axis=0, keepdims=True)
    y = o * lax.rsqrt(ms + RMS_EPS) * g_ref[...] * (1.0 - LAM_INIT)
    o_ref[...] = y.T.astype(BF16)


def _diff_attention(proj, vt, consts, lamv, bias_t, subln_g, batch, P, seq_len, nsub):
    T = SEQ_TILE
    tq = nsub * T
    nq = P // tq
    nkv = pl.cdiv(seq_len, T)
    kern = functools.partial(_diff_kernel, seq_len=seq_len, nkv=nkv, nsub=nsub)
    return pl.pallas_call(
        kern,
        grid=(batch, DIFF_HEADS, nq),
        in_specs=[
            pl.BlockSpec(memory_space=pltpu.SMEM),
            pl.BlockSpec((4, HEAD_DIM), lambda b, h, i: (0, 0)),
            pl.BlockSpec((tq, LANES), lambda b, h, i: (b * nq + i, h)),
            pl.BlockSpec((P, LANES), lambda b, h, i: (b, DIFF_HEADS + h)),
            pl.BlockSpec((1, VT_ROWS, P), lambda b, h, i: (h, 0, b)),
            pl.BlockSpec((1, 2, 5, T, T), lambda b, h, i: (h, 0, 0, 0, 0)),
            pl.BlockSpec((LANES, 1), lambda b, h, i: (0, 0)),
        ],
        out_specs=pl.BlockSpec((tq, LANES), lambda b, h, i: (b * nq + i, h)),
        out_shape=jax.ShapeDtypeStruct((batch * P, DIFF_WIDTH), BF16),
        scratch_shapes=[
            pltpu.VMEM((2, VT_ROWS, tq), F32),
            pltpu.VMEM((2, 1, tq), F32),
        ],
        compiler_params=_cparams(("parallel", "parallel", "parallel")),
        name="diff_attn",
    )(consts, lamv, proj, proj, vt, bias_t, subln_g)


def _swa_kernel(sink_ref, q_ref, k_ref, v_ref, bias_ref, o_ref, *, seq_len, P):
    T = SEQ_TILE
    i = pl.program_id(1)
    start = pl.multiple_of(jnp.clip(i * T - WINDOW, 0, P - SWA_KEYS), LANES)
    boff = pl.multiple_of(start - i * T + T, LANES)

    def attend(masked):
        kw = k_ref[pl.ds(start, SWA_KEYS), :]
        vw = v_ref[pl.ds(start, SWA_KEYS), :]
        kvalid = (start + lax.broadcasted_iota(jnp.int32, (1, SWA_KEYS), 1)) < seq_len
        lane = lax.broadcasted_iota(jnp.int32, (T, LANES), 1)
        for j in range(SWA_GROUP):
            qp = q_ref[:, j * LANES:(j + 1) * LANES]
            zero = jnp.zeros_like(qp)
            outs = []
            for kvh in range(SWA_KV_HEADS):
                head = kvh * SWA_GROUP + j
                in_half = (lane < HEAD_DIM) if kvh == 0 else (lane >= HEAD_DIM)
                qe = jnp.where(in_half, qp, zero)
                s = lax.dot_general(qe, kw, _NT, preferred_element_type=F32)
                s = s + bias_ref[head, :, pl.ds(boff, SWA_KEYS)]
                if masked:
                    s = jnp.where(kvalid, s, NEG)
                sink = sink_ref[head]
                m = jnp.maximum(jnp.max(s, axis=-1, keepdims=True), sink)
                p = jnp.exp2(s - m)
                l = jnp.sum(p, axis=-1, keepdims=True) + jnp.exp2(sink - m)
                outs.append(jnp.dot(p.astype(BF16), vw, preferred_element_type=F32) / l)
            o_ref[:, j * LANES:(j + 1) * LANES] = jnp.where(lane < HEAD_DIM, outs[0], outs[1]).astype(BF16)

    has_padding = start + SWA_KEYS > seq_len
    pl.when(has_padding)(lambda: attend(True))
    pl.when(jnp.logical_not(has_padding))(lambda: attend(False))


def _swa_attention(proj, sink, bias_w, batch, P, seq_len):
    T = SEQ_TILE
    nq = P // T
    q_blk = (3 * DIFF_WIDTH) // SWA_WIDTH
    k_blk = (3 * DIFF_WIDTH + SWA_WIDTH) // LANES
    kern = functools.partial(_swa_kernel, seq_len=seq_len, P=P)
    return pl.pallas_call(
        kern,
        grid=(batch, nq),
        in_specs=[
            pl.BlockSpec(memory_space=pltpu.SMEM),
            pl.BlockSpec((T, SWA_WIDTH), lambda b, i: (b * nq + i, q_blk)),
            pl.BlockSpec((P, LANES), lambda b, i: (b, k_blk)),
            pl.BlockSpec((P, LANES), lambda b, i: (b, k_blk + 1)),
            pl.BlockSpec((SWA_Q_HEADS, T, SWA_KEYS + T), lambda b, i: (0, 0, 0)),
        ],
        out_specs=pl.BlockSpec((T, SWA_WIDTH), lambda b, i: (b * nq + i, 0)),
        out_shape=jax.ShapeDtypeStruct((batch * P, SWA_WIDTH), BF16),
        compiler_params=_cparams(("parallel", "parallel")),
        name="swa_attn",
    )(sink, proj, proj, proj, bias_w)


def _out_kernel(do_ref, so_ref, h_ref, w_ref, g_ref, wrh_ref, wrl_ref, h2_ref, m_ref, p3_ref, *, seq_len, tm):
    i = pl.program_id(1)
    h2 = (h_ref[...]
          + jnp.dot(do_ref[...], w_ref[:DIFF_WIDTH, :], preferred_element_type=F32)
          + jnp.dot(so_ref[...], w_ref[DIFF_WIDTH:, :], preferred_element_type=F32))
    h2_ref[...] = h2
    ms = jnp.mean(h2 * h2, axis=-1, keepdims=True)
    mf = h2 * lax.rsqrt(ms + RMS_EPS) * g_ref[...]
    mf_hi = mf.astype(BF16)
    m_ref[...] = mf_hi
    mf_lo = (mf - mf_hi.astype(F32)).astype(BF16)
    logits = (jnp.dot(mf_hi, wrh_ref[...], preferred_element_type=F32)
              + jnp.dot(mf_lo, wrh_ref[...], preferred_element_type=F32)
              + jnp.dot(mf_hi, wrl_ref[...], preferred_element_type=F32))
    lane = lax.broadcasted_iota(jnp.int32, logits.shape, 1)
    logits = jnp.where(lane < N_EXPERTS, logits, NEG)
    e = jnp.exp(logits - jnp.max(logits, axis=-1, keepdims=True))
    probs = (e / jnp.sum(e, axis=-1, keepdims=True)).T[:N_EXPERTS]
    pos = i * tm + lax.broadcasted_iota(jnp.int32, (1, tm), 1)
    probs = jnp.where(pos < seq_len, probs, -1.0)
    for c in range(tm // CHUNK):
        p3_ref[c] = probs[:, c * CHUNK:(c + 1) * CHUNK]


def _out_router(diff_o, swa_o, h0, w_out_bf16, g, w_router, batch, P, seq_len):
    tm = SEQ_TILE
    nq = P // tm
    rows = batch * P
    kern = functools.partial(_out_kernel, seq_len=seq_len, tm=tm)
    wr = jnp.pad(w_router.astype(F32), ((0, 0), (0, LANES - N_EXPERTS)))
    wr_hi = wr.astype(BF16)
    wr_lo = (wr - wr_hi.astype(F32)).astype(BF16)
    return pl.pallas_call(
        kern,
        grid=(batch, nq),
        in_specs=[
            pl.BlockSpec((tm, DIFF_WIDTH), lambda b, i: (b * nq + i, 0)),
            pl.BlockSpec((tm, SWA_WIDTH), lambda b, i: (b * nq + i, 0)),
            pl.BlockSpec((tm, D_MODEL), lambda b, i: (b * nq + i, 0)),
            pl.BlockSpec((MIX_WIDTH, D_MODEL), lambda b, i: (0, 0)),
            pl.BlockSpec((1, D_MODEL), lambda b, i: (0, 0)),
            pl.BlockSpec((D_MODEL, LANES), lambda b, i: (0, 0)),
            pl.BlockSpec((D_MODEL, LANES), lambda b, i: (0, 0)),
        ],
        out_specs=[
            pl.BlockSpec((tm, D_MODEL), lambda b, i: (b * nq + i, 0)),
            pl.BlockSpec((tm, D_MODEL), lambda b, i: (b * nq + i, 0)),
            pl.BlockSpec((tm // CHUNK, N_EXPERTS, CHUNK), lambda b, i: (b * nq + i, 0, 0)),
        ],
        out_shape=[
            jax.ShapeDtypeStruct((rows, D_MODEL), F32),
            jax.ShapeDtypeStruct((rows, D_MODEL), BF16),
            jax.ShapeDtypeStruct((rows // CHUNK, N_EXPERTS, CHUNK), F32),
        ],
        compiler_params=_cparams(("parallel", "parallel")),
        name="out_router",
    )(diff_o, swa_o, h0, w_out_bf16, g, wr_hi, wr_lo)


def _topk_kernel(p3_ref, pos_ref, post_ref, gatet_ref, base_ref, incl_ref, flag_ref, *, capacity, nc):
    E = N_EXPERTS
    probs = p3_ref[...]
    bits = pltpu.bitcast(probs, jnp.int32)

    def count(pred):
        part = jnp.sum(jnp.where(pred, 1.0, 0.0), axis=0, keepdims=True)
        return jnp.broadcast_to(jnp.sum(part, axis=-1, keepdims=True), part.shape)

    def bisect(_, lohi):
        lo, hi = lohi
        mid = lo + ((hi - lo + 1) >> 1)
        ok = count(bits >= mid) >= capacity
        return jnp.where(ok, mid, lo), jnp.where(ok, hi, mid - 1)

    lo0 = jnp.zeros((1, E, CHUNK), jnp.int32)
    hi0 = jnp.full((1, E, CHUNK), 0x7F800000, jnp.int32)
    thr, _ = lax.fori_loop(0, 32, bisect, (lo0, hi0))
    gt = bits > thr
    eq = bits == thr
    need = capacity - count(gt)[0]

    tri = (lax.broadcasted_iota(jnp.int32, (CHUNK, CHUNK), 0)
           <= lax.broadcasted_iota(jnp.int32, (CHUNK, CHUNK), 1)).astype(BF16)

    def inclusive_prefix(flags):
        f2 = flags.astype(BF16).reshape(nc * E, CHUNK)
        return jnp.dot(f2, tri, preferred_element_type=F32).reshape(nc, E, CHUNK)

    eqf = jnp.where(eq, 1.0, 0.0)
    incl_ref[...] = inclusive_prefix(eqf)
    flag_ref[...] = eqf

    def tie_scan(c, run):
        inc = incl_ref[c]
        e_c = flag_ref[c]
        take = (e_c > 0.0) & ((run + inc - e_c) < need)
        flag_ref[c] = jnp.where(take, 1.0, 0.0)
        return run + jnp.broadcast_to(inc[:, CHUNK - 1:CHUNK], inc.shape)

    lax.fori_loop(0, nc, tie_scan, jnp.zeros((E, CHUNK), F32))
    self = jnp.where(gt, 1.0, flag_ref[...])
    flag_ref[...] = self
    incl_ref[...] = inclusive_prefix(self)

    zpad = jnp.zeros((CHUNK - E, CHUNK), F32)

    def to_token_major(x):
        return jnp.concatenate([x, zpad], axis=0).T[:, :E]

    def pos_scan(c, run):
        inc = incl_ref[c]
        s_c = flag_ref[c]
        sel = s_c > 0.0
        pos = jnp.where(sel, run + inc - s_c, -1.0)
        gate = jnp.where(sel, p3_ref[c], 0.0)
        off = pl.multiple_of(c * CHUNK, CHUNK)
        pos_ref[:, pl.ds(off, CHUNK)] = pos
        post_ref[pl.ds(off, CHUNK), :] = to_token_major(pos)
        gatet_ref[pl.ds(off, CHUNK), :] = to_token_major(gate)
        base_ref[c] = run.astype(jnp.int32)
        return run + jnp.broadcast_to(inc[:, CHUNK - 1:CHUNK], inc.shape)

    lax.fori_loop(0, nc, pos_scan, jnp.zeros((E, CHUNK), F32))


def _topk(probs3, capacity):
    nc = probs3.shape[0]
    nt = nc * CHUNK
    kern = functools.partial(_topk_kernel, capacity=capacity, nc=nc)
    return pl.pallas_call(
        kern,
        out_shape=[
            jax.ShapeDtypeStruct((N_EXPERTS, nt), F32),
            jax.ShapeDtypeStruct((nt, N_EXPERTS), F32),
            jax.ShapeDtypeStruct((nt, N_EXPERTS), F32),
            jax.ShapeDtypeStruct((nc, N_EXPERTS, CHUNK), jnp.int32),
        ],
        scratch_shapes=[
            pltpu.VMEM((nc, N_EXPERTS, CHUNK), F32),
            pltpu.VMEM((nc, N_EXPERTS, CHUNK), F32),
        ],
        compiler_params=pltpu.CompilerParams(vmem_limit_bytes=VMEM_LIMIT),
        name="topk",
    )(probs3)


def _gather_kernel(base_ref, cnt_ref, pos_ref, m_ref, xc_ref, *, nc, sub):
    e = pl.program_id(0)
    sb = pl.program_id(1)

    @pl.when(sb == 0)
    def _():
        xc_ref[...] = jnp.zeros_like(xc_ref)

    first = e * nc + sb * sub

    def copy_rows(u, k, rows):
        erow = lax.broadcasted_iota(jnp.int32, (N_EXPERTS, CHUNK), 0)
        row = lax.broadcasted_iota(jnp.int32, (rows, CHUNK), 0).astype(F32)
        starts = [pl.multiple_of((base_ref[first + u + t] // BF16_ROWS) * BF16_ROWS, BF16_ROWS) for t in range(k)]
        offs = [pl.multiple_of((u + t) * CHUNK, CHUNK) for t in range(k)]
        prow = [jnp.sum(jnp.where(erow == e, pos_ref[:, pl.ds(off, CHUNK)], 0.0), axis=0, keepdims=True)
                - a.astype(F32) for a, off in zip(starts, offs)]
        onehot = [jnp.where(row == p, 1.0, 0.0).astype(BF16) for p in prow]
        g = [jnp.dot(oh, m_ref[pl.ds(off, CHUNK), :], preferred_element_type=F32)
             for oh, off in zip(onehot, offs)]
        for a, gt in zip(starts, g):
            xc_ref[0, pl.ds(a, rows), :] += gt.astype(BF16)

    def single(u):
        few = cnt_ref[first + u] <= FAST_CNT
        pl.when(few)(lambda: copy_rows(u, 1, FAST_ROWS))
        pl.when(jnp.logical_not(few))(lambda: copy_rows(u, 1, WIN_ROWS))

    def group(gi, carry):
        u0 = gi * GATHER_GROUP
        most = functools.reduce(jnp.maximum, [cnt_ref[first + u0 + t] for t in range(GATHER_GROUP)])
        few = most <= FAST_CNT
        pl.when(few)(lambda: copy_rows(u0, GATHER_GROUP, FAST_ROWS))
        pl.when(jnp.logical_not(few))(lambda: copy_rows(u0, GATHER_GROUP, WIN_ROWS))
        return carry

    ngroups = sub // GATHER_GROUP
    lax.fori_loop(0, ngroups, group, 0)
    for u in range(ngroups * GATHER_GROUP, sub):
        single(u)


def _gather(base_flat, cnt_flat, pos, m, cp):
    nt = pos.shape[1]
    nc = nt // CHUNK
    sub = _largest_divisor(nc, 17)
    ts = sub * CHUNK
    kern = functools.partial(_gather_kernel, nc=nc, sub=sub)
    return pl.pallas_call(
        kern,
        grid_spec=pltpu.PrefetchScalarGridSpec(
            num_scalar_prefetch=2,
            grid=(N_EXPERTS, nc // sub),
            in_specs=[
                pl.BlockSpec((N_EXPERTS, ts), lambda e, s, base, cnt: (0, s)),
                pl.BlockSpec((ts, D_MODEL), lambda e, s, base, cnt: (s, 0)),
            ],
            out_specs=pl.BlockSpec((1, cp, D_MODEL), lambda e, s, base, cnt: (e, 0, 0)),
        ),
        out_shape=jax.ShapeDtypeStruct((N_EXPERTS, cp, D_MODEL), BF16),
        compiler_params=_cparams(("parallel", "arbitrary")),
        name="gather",
    )(base_flat, cnt_flat, pos, m)


def _ffn_kernel(x_ref, wg_ref, wu_ref, wd_ref, y_ref, acc_ref, *, rows):
    f = pl.program_id(1)
    nf = pl.num_programs(1)

    @pl.when(f == 0)
    def _():
        acc_ref[...] = jnp.zeros_like(acc_ref)

    wg = wg_ref[0].astype(BF16)
    wu = wu_ref[0].astype(BF16)
    wd = wd_ref[0].astype(BF16)
    block = -(-rows // (FFN_ROW_BLOCKS * BF16_ROWS)) * BF16_ROWS
    bounds = list(range(0, rows, block)) + [rows]
    for r0, r1 in zip(bounds[:-1], bounds[1:]):
        x = x_ref[0, r0:r1, :]
        g = lax.dot_general(x, wg, _NT, preferred_element_type=F32)
        u = lax.dot_general(x, wu, _NT, preferred_element_type=F32)
        hid = (g * jax.nn.sigmoid(g) * u).astype(BF16)
        acc_ref[r0:r1, :] += jnp.dot(hid, wd, preferred_element_type=F32)

    @pl.when(f == nf - 1)
    def _():
        y_ref[0, :rows, :] = acc_ref[...].astype(BF16)
        y_ref[0, rows:, :] = jnp.zeros((y_ref.shape[1] - rows, D_MODEL), BF16)


def _ffn(xc, w_gate_t, w_up_t, w_down, rows):
    cp = xc.shape[1]
    kern = functools.partial(_ffn_kernel, rows=rows)
    once = pl.Buffered(1)
    w_spec = pl.BlockSpec((1, FF_TILE, D_MODEL), lambda e, f: (e, f, 0))
    return pl.pallas_call(
        kern,
        grid=(N_EXPERTS, D_FF // FF_TILE),
        in_specs=[pl.BlockSpec((1, cp, D_MODEL), lambda e, f: (e, 0, 0), pipeline_mode=once), w_spec, w_spec, w_spec],
        out_specs=pl.BlockSpec((1, cp, D_MODEL), lambda e, f: (e, 0, 0), pipeline_mode=once),
        out_shape=jax.ShapeDtypeStruct((N_EXPERTS, cp, D_MODEL), BF16),
        scratch_shapes=[pltpu.VMEM((rows, D_MODEL), F32)],
        compiler_params=_cparams(("parallel", "arbitrary")),
        name="ffn",
    )(xc, w_gate_t, w_up_t, w_down)


def _combine_kernel(base_ref, fast_ref, h2_ref, post_ref, gatet_ref, g_ref, expand_ref, yc_ref, o_ref,
                    win_ref, big_ref, acc_ref, carry_ref, sem_ref, big_sem, *, nc, cpb, out_blocks):
    E = N_EXPERTS
    b = pl.program_id(0)
    c = pl.program_id(1)
    step = b * cpb + c
    nsteps = pl.num_programs(0) * cpb
    slot = step % 2

    def window_start(chunk, e):
        return pl.multiple_of((base_ref[e * nc + chunk] // BF16_ROWS) * BF16_ROWS, BF16_ROWS)

    def fast_copy(chunk, e, sl):
        return pltpu.make_async_copy(yc_ref.at[e, pl.ds(window_start(chunk, e), FAST_ROWS), :],
                                     win_ref.at[sl, pl.ds(e * FAST_ROWS, FAST_ROWS), :], sem_ref.at[sl, e])

    @pl.when((step == 0) & (fast_ref[0] == 1))
    def _():
        for e in range(E):
            fast_copy(0, e, 0).start()

    nxt = jnp.minimum(step + 1, nsteps - 1)

    @pl.when((step + 1 < nsteps) & (fast_ref[nxt] == 1))
    def _():
        for e in range(E):
            fast_copy(step + 1, e, 1 - slot).start()

    is_fast = fast_ref[step] == 1

    @pl.when(is_fast)
    def _():
        for e in range(E):
            fast_copy(step, e, slot).wait()
        lane_e = lax.broadcasted_iota(jnp.int32, (1, E), 1)
        a_vec = jnp.zeros((1, E), F32)
        for e in range(E):
            a_vec = jnp.where(lane_e == e, window_start(step, e).astype(F32), a_vec)
        pos = post_ref[...]
        rel = jnp.where(pos >= 0.0, pos - a_vec, -1.0).astype(BF16)
        gate = gatet_ref[...]
        g_hi = gate.astype(BF16)
        g_lo = (gate - g_hi.astype(F32)).astype(BF16)
        expand = expand_ref[...]
        spread = lambda x: jnp.dot(x, expand, preferred_element_type=F32)
        slot_row = (lax.broadcasted_iota(jnp.int32, (1, E * FAST_ROWS), 1) % FAST_ROWS).astype(F32)
        match = spread(rel) == slot_row
        w = win_ref[slot]
        acc_ref[...] = (h2_ref[...]
                        + jnp.dot(jnp.where(match, spread(g_hi), 0.0).astype(BF16), w, preferred_element_type=F32)
                        + jnp.dot(jnp.where(match, spread(g_lo), 0.0).astype(BF16), w, preferred_element_type=F32))

    @pl.when(jnp.logical_not(is_fast))
    def _():
        acc = h2_ref[...]
        col = lax.broadcasted_iota(jnp.int32, (CHUNK, WIN_ROWS), 1).astype(F32)
        for e in range(E):
            a = window_start(step, e)
            copy = pltpu.make_async_copy(yc_ref.at[e, pl.ds(a, WIN_ROWS), :], big_ref, big_sem)
            copy.start()
            copy.wait()
            onehot = jnp.where(col == post_ref[:, e:e + 1] - a.astype(F32), 1.0, 0.0).astype(BF16)
            acc = acc + gatet_ref[:, e:e + 1] * jnp.dot(onehot, big_ref[...], preferred_element_type=F32)
        acc_ref[...] = acc

    acc = acc_ref[...]
    ms = jnp.mean(acc * acc, axis=-1, keepdims=True)
    y = acc * lax.rsqrt(ms + RMS_EPS) * g_ref[...]

    @pl.when((c >= 1) & (c <= out_blocks))
    def _():
        o_ref[0, :CHUNK - N_META, :] = carry_ref[N_META:, :]
        o_ref[0, CHUNK - N_META:, :] = y[:N_META, :]

    carry_ref[...] = y


def _combine(base_flat, fast, h2, post, gatet, g, yc, batch, P, S):
    nt = h2.shape[0]
    nc = nt // CHUNK
    cpb = P // CHUNK
    out_blocks = S // CHUNK
    stacked = N_EXPERTS * FAST_ROWS
    expand = jnp.asarray(np.arange(stacked)[None, :] // FAST_ROWS == np.arange(N_EXPERTS)[:, None], BF16)
    kern = functools.partial(_combine_kernel, nc=nc, cpb=cpb, out_blocks=out_blocks)
    row_blk = lambda b, c, base, fast: (b * cpb + c, 0)
    return pl.pallas_call(
        kern,
        grid_spec=pltpu.PrefetchScalarGridSpec(
            num_scalar_prefetch=2,
            grid=(batch, cpb),
            in_specs=[
                pl.BlockSpec((CHUNK, D_MODEL), row_blk),
                pl.BlockSpec((CHUNK, N_EXPERTS), row_blk),
                pl.BlockSpec((CHUNK, N_EXPERTS), row_blk),
                pl.BlockSpec((1, D_MODEL), lambda b, c, base, fast: (0, 0)),
                pl.BlockSpec((N_EXPERTS, stacked), lambda b, c, base, fast: (0, 0)),
                pl.BlockSpec(memory_space=pl.ANY),
            ],
            out_specs=pl.BlockSpec(
                (1, CHUNK, D_MODEL), lambda b, c, base, fast: (b, jnp.clip(c - 1, 0, out_blocks - 1), 0)),
            scratch_shapes=[
                pltpu.VMEM((2, stacked, D_MODEL), BF16),
                pltpu.VMEM((WIN_ROWS, D_MODEL), BF16),
                pltpu.VMEM((CHUNK, D_MODEL), F32),
                pltpu.VMEM((CHUNK, D_MODEL), F32),
                pltpu.SemaphoreType.DMA((2, N_EXPERTS)),
                pltpu.SemaphoreType.DMA,
            ],
        ),
        out_shape=jax.ShapeDtypeStruct((batch, S, D_MODEL), F32),
        compiler_params=_cparams(("arbitrary", "arbitrary")),
        name="combine",
    )(base_flat, fast, h2, post, gatet, g, expand, yc)


def _swa_head_perm():
    perm = np.arange(SWA_WIDTH).reshape(SWA_KV_HEADS, SWA_GROUP, HEAD_DIM)
    return perm.transpose(1, 0, 2).reshape(-1)


def _prep_params(rel_bias, w_in, w_out):
    perm = _swa_head_perm()
    scale = np.ones((IN_COLS,), np.float32)
    scale[:DIFF_WIDTH] = HEAD_DIM ** -0.5
    scale[3 * DIFF_WIDTH:3 * DIFF_WIDTH + SWA_WIDTH] = HEAD_DIM ** -0.5
    cols = np.arange(IN_COLS)
    cols[3 * DIFF_WIDTH:3 * DIFF_WIDTH + SWA_WIDTH] = 3 * DIFF_WIDTH + perm
    w_in_p = (w_in * scale)[:, cols].astype(BF16)
    rows = np.arange(MIX_WIDTH)
    rows[DIFF_WIDTH:] = DIFF_WIDTH + perm
    w_out_p = w_out[rows, :].astype(BF16)

    T = SEQ_TILE
    table = rel_bias.astype(F32)

    def lookup(idx, cols):
        onehot = (idx[..., None] == jnp.arange(N_BUCKETS)).astype(F32)
        return jnp.einsum("...b,bc->...c", onehot, table[:, cols], precision=lax.Precision.HIGHEST)

    kk = jnp.arange(T)[:, None]
    qq = jnp.arange(T)[None, :]
    idx = jnp.stack([_rel_bucket(d * T + kk - qq) for d in (-2, -1, 0, 1, 2)])
    diff_tiles = lookup(idx, slice(0, 2 * DIFF_HEADS))
    diff_tiles = diff_tiles.transpose(3, 0, 1, 2).reshape(DIFF_HEADS, 2, 5, T, T) * LOG2E
    half = N_BUCKETS // 2
    diff_consts = jnp.stack([table[half - 1, 0:2 * DIFF_HEADS:2], table[N_BUCKETS - 1, 0:2 * DIFF_HEADS:2],
                             table[half - 1, 1:2 * DIFF_HEADS:2], table[N_BUCKETS - 1, 1:2 * DIFF_HEADS:2]],
                            axis=1) * LOG2E
    rel = jnp.arange(SWA_KEYS + T)[None, :] - T - jnp.arange(T)[:, None]
    swa_tiles = lookup(_rel_bucket(rel), slice(2 * DIFF_HEADS, None))
    swa_tiles = jnp.where((jnp.abs(rel) <= WINDOW)[:, :, None], swa_tiles * LOG2E, NEG).transpose(2, 0, 1)
    return w_in_p, w_out_p, diff_tiles, diff_consts, swa_tiles


def _plan_rows(seq_len):
    nsub = min((4, 5, 6), key=lambda n: (-(-seq_len // (n * SEQ_TILE)) * n, -n))
    return nsub, -(-seq_len // (nsub * SEQ_TILE)) * nsub * SEQ_TILE


def _trunk(x, meta_tokens, prep, attn_norm_g, lamv, subln_g, swa_sink, ffn_norm_g, w_router,
           w_gate, w_up, w_down, final_norm_g):
    w_in_p, w_out_p, diff_tiles, diff_consts, swa_tiles = prep
    B, S, _ = x.shape
    L = S + N_META
    nsub, P = _plan_rows(L)
    rows = B * P
    capacity = EC_FACTOR * (B * L) // N_EXPERTS
    ffn_rows = -(-capacity // BF16_ROWS) * BF16_ROWS
    cp = -(-(capacity + WIN_ROWS) // BF16_ROWS) * BF16_ROWS

    h0 = jnp.concatenate([jnp.broadcast_to(meta_tokens.astype(x.dtype)[None], (B, N_META, D_MODEL)), x,
                          jnp.zeros((B, P - L, D_MODEL), x.dtype)], axis=1).reshape(rows, D_MODEL)

    tm = SEQ_TILE * _largest_divisor(rows // SEQ_TILE, 2)
    proj, vt = _proj(h0, attn_norm_g.reshape(1, D_MODEL), w_in_p, tm)
    diff_o = _diff_attention(proj, vt, diff_consts, lamv, diff_tiles, subln_g.reshape(LANES, 1), B, P, L, nsub)
    swa_o = _swa_attention(proj, swa_sink * LOG2E, swa_tiles, B, P, L)
    h2, m, probs3 = _out_router(diff_o, swa_o, h0, w_out_p, ffn_norm_g.reshape(1, D_MODEL), w_router, B, P, L)
    pos, post, gatet, base3 = _topk(probs3, capacity)
    base_ec = base3[:, :, 0].T
    nxt = jnp.concatenate([base_ec[:, 1:], jnp.full((N_EXPERTS, 1), capacity, jnp.int32)], axis=1)
    cnt_ec = nxt - base_ec
    fast = jnp.all(cnt_ec <= FAST_CNT, axis=0).astype(jnp.int32)
    base_flat = base_ec.reshape(-1)
    xc = _gather(base_flat, cnt_ec.reshape(-1), pos, m, cp)
    yc = _ffn(xc, w_gate, w_up, w_down, ffn_rows)
    return _combine(base_flat, fast, h2, post, gatet, final_norm_g.reshape(1, D_MODEL), yc, B, P, S)


def kernel(x_prompt, x_sample, meta_tokens, rel_bias, attn_norm_g, w_in, diff_lambda_q1, diff_lambda_k1,
           diff_lambda_q2, diff_lambda_k2, diff_subln_g, swa_sink, w_out, ffn_norm_g, w_router, w_gate, w_up,
           w_down, final_norm_g):
    prep = _prep_params(rel_bias, w_in[0], w_out[0])
    lamv = jnp.stack([diff_lambda_q1[0], diff_lambda_k1[0], diff_lambda_q2[0], diff_lambda_k2[0]]).astype(F32)
    args = (meta_tokens, prep, attn_norm_g[0], lamv, diff_subln_g[0], swa_sink[0].astype(F32), ffn_norm_g[0],
            w_router[0], jnp.swapaxes(w_gate[0], 1, 2), jnp.swapaxes(w_up[0], 1, 2), w_down[0], final_norm_g)
    return (_trunk(x_prompt, *args), _trunk(x_sample, *args))
```

```python
import functools
import math

import jax
import jax.numpy as jnp
import numpy as np
from jax import lax
from jax.experimental import pallas as pl
from jax.experimental.pallas import tpu as pltpu

F32 = jnp.float32
BF16 = jnp.bfloat16

D_MODEL = 1024
HEAD_DIM = 64
N_META = 16
DIFF_HEADS = 4
DIFF_WIDTH = DIFF_HEADS * 2 * HEAD_DIM
SWA_Q_HEADS = 8
SWA_KV_HEADS = 2
SWA_GROUP = SWA_Q_HEADS // SWA_KV_HEADS
SWA_WIDTH = SWA_Q_HEADS * HEAD_DIM
SWA_KV_WIDTH = SWA_KV_HEADS * HEAD_DIM
MIX_WIDTH = DIFF_WIDTH + SWA_WIDTH
IN_COLS = 3 * DIFF_WIDTH + SWA_WIDTH + 2 * SWA_KV_WIDTH
WINDOW = 128
N_BUCKETS = 32
MAX_DISTANCE = 128
N_EXPERTS = 16
EC_FACTOR = 2
D_FF = 2752
RMS_EPS = 1e-6
LAM_INIT = 0.8 - 0.6 * math.exp(-0.3 * 0)

LANES = 128
SUBLANES = 8
F32_MANT_BITS = 23
F32_EXP_BIAS = 127
BF16_ROWS = 16
VMEM_LIMIT = 56 * 1024 * 1024

SEQ_TILE = 256
SWA_KEYS = SEQ_TILE + 2 * WINDOW
FAR_GROUP = 4
FF_TILE = 688
FFN_ROW_BLOCKS = 2
CHUNK = LANES
VT_ROWS = LANES + BF16_ROWS
LOG2E = math.log2(math.e)
WIN_ROWS = CHUNK + BF16_ROWS
FAST_ROWS = 64
FAST_CNT = FAST_ROWS - BF16_ROWS
GATHER_GROUP = 4
GATHER_EXPERTS = 2
STALE_MAX_LIMIT = 64.0
NEG = -1e30

_NT = (((1,), (1,)), ((), ()))


def _cparams(sem):
    return pltpu.CompilerParams(dimension_semantics=sem, vmem_limit_bytes=VMEM_LIMIT)


def _largest_divisor(n, cap):
    return max(d for d in range(1, cap + 1) if n % d == 0)


def _bucket_thresholds(max_rel):
    half = N_BUCKETS // 2
    max_exact = half // 2
    n = jnp.arange(max_rel)
    large = max_exact + (jnp.log(jnp.maximum(n, 1).astype(F32) / max_exact)
                         / math.log(MAX_DISTANCE / max_exact) * (half - max_exact)).astype(jnp.int32)
    mag = jnp.where(n < max_exact, n, jnp.minimum(large, half - 1))
    mag = jnp.arange(half)[mag]
    return jnp.sum(mag[None, :] < jnp.arange(1, half)[:, None], axis=1)


def _rel_bucket(rel, thresholds):
    mag = jnp.sum(jnp.abs(rel)[..., None] >= thresholds, axis=-1)
    return jnp.where(rel > 0, N_BUCKETS // 2, 0) + mag


def _proj_kernel(h_ref, g_ref, w_ref, proj_ref, vt_ref):
    x = h_ref[...]
    ms = jnp.mean(x * x, axis=-1, keepdims=True)
    a = (x * lax.rsqrt(ms + RMS_EPS) * g_ref[...]).astype(BF16)
    proj = jnp.dot(a, w_ref[...], preferred_element_type=F32)
    q_lo, q_hi = 3 * DIFF_WIDTH, 3 * DIFF_WIDTH + SWA_WIDTH
    proj_ref[:, :DIFF_WIDTH] = (proj[:, :DIFF_WIDTH] * LOG2E).astype(BF16)
    proj_ref[:, DIFF_WIDTH:q_lo] = proj[:, DIFF_WIDTH:q_lo].astype(BF16)
    proj_ref[:, q_lo:q_hi] = (proj[:, q_lo:q_hi] * LOG2E).astype(BF16)
    proj_ref[:, q_hi:] = proj[:, q_hi:].astype(BF16)
    ones = jnp.ones((VT_ROWS - LANES, x.shape[0]), BF16)
    for h in range(DIFF_HEADS):
        v = proj[:, 2 * DIFF_WIDTH + h * LANES: 2 * DIFF_WIDTH + (h + 1) * LANES]
        vt_ref[h, :LANES, :] = v.T.astype(BF16)
        vt_ref[h, LANES:, :] = ones


def _proj(h0, g, w_in_bf16, tm):
    rows = h0.shape[0]
    return pl.pallas_call(
        _proj_kernel,
        grid=(rows // tm,),
        in_specs=[
            pl.BlockSpec((tm, D_MODEL), lambda i: (i, 0)),
            pl.BlockSpec((1, D_MODEL), lambda i: (0, 0)),
            pl.BlockSpec((D_MODEL, IN_COLS), lambda i: (0, 0)),
        ],
        out_specs=[
            pl.BlockSpec((tm, IN_COLS), lambda i: (i, 0)),
            pl.BlockSpec((DIFF_HEADS, VT_ROWS, tm), lambda i: (0, 0, i)),
        ],
        out_shape=[
            jax.ShapeDtypeStruct((rows, IN_COLS), BF16),
            jax.ShapeDtypeStruct((DIFF_HEADS, VT_ROWS, rows), BF16),
        ],
        compiler_params=_cparams(("parallel",)),
        name="proj",
    )(h0, g, w_in_bf16)


def _diff_kernel(consts_ref, lamv_ref, q_ref, k_ref, vt_ref, bias_ref, g_ref, o_ref,
                 acc_ref, m_ref, excess_ref, *, seq_len, nkv, nsub):
    T = SEQ_TILE
    h = pl.program_id(1)
    i = pl.program_id(2)
    q = q_ref[...]
    lane = lax.broadcasted_iota(jnp.int32, q.shape, 1)
    zero = jnp.zeros_like(q)
    q_maps = (jnp.where(lane < HEAD_DIM, q, zero), jnp.where(lane >= HEAD_DIM, q, zero))

    def chunks(j0, count, *, tile, side, mask, exact):
        maps = range(2)
        c = [0.0, 0.0] if tile else [consts_ref[h, 2 * mp + side] for mp in maps]
        offs = [pl.multiple_of((j0 + t) * T, T) for t in range(count)]
        s = [[lax.dot_general(k_ref[pl.ds(off, T), :], q_maps[mp], _NT, preferred_element_type=F32)
              for mp in maps] for off in offs]
        m_run = [m_ref[mp] for mp in maps]
        lead = list(m_run)
        peak = [None, None]
        pv_sum = [None, None]
        for t, off in enumerate(offs):
            vb = vt_ref[0, :, pl.ds(off, T)]
            if mask:
                kvalid = (off + lax.broadcasted_iota(jnp.int32, (T, 1), 0)) < seq_len
            for mp in maps:
                st = s[t][mp]
                if tile:
                    st = jnp.concatenate(
                        [st[:, u * T:(u + 1) * T] + bias_ref[0, mp, jnp.clip(j0 + t - (i * nsub + u), -2, 2) + 2]
                         for u in range(nsub)], axis=1)
                if mask:
                    st = jnp.where(kvalid, st, NEG)
                top = jnp.max(st, axis=0, keepdims=True) + c[mp]
                if exact:
                    m_new = jnp.maximum(m_run[mp], top)
                    alpha = jnp.exp2(m_run[mp] - m_new)
                    m_run[mp] = m_new
                    p = jnp.exp2(st - (m_new - c[mp])).astype(BF16)
                    acc_ref[mp] = alpha * acc_ref[mp] + jnp.dot(vb, p, preferred_element_type=F32)
                else:
                    p = jnp.exp2(st - (lead[mp] - c[mp])).astype(BF16)
                    pv = jnp.dot(vb, p, preferred_element_type=F32)
                    pv_sum[mp] = pv if t == 0 else pv_sum[mp] + pv
                    peak[mp] = top if t == 0 else jnp.maximum(peak[mp], top)
        for mp in maps:
            if not exact:
                m_run[mp] = jnp.maximum(lead[mp], peak[mp])
                excess_ref[mp] = jnp.maximum(excess_ref[mp], peak[mp] - lead[mp])
                acc_ref[mp] = (acc_ref[mp] + pv_sum[mp]) * jnp.exp2(lead[mp] - m_run[mp])
            m_ref[mp] = m_run[mp]

    def loop(lo, hi, group, **kw):
        ngroups = jnp.maximum(hi - lo, 0) // group

        def grouped(g, carry):
            chunks(lo + g * group, group, **kw)
            return carry

        def single(j, carry):
            chunks(j, 1, **kw)
            return carry

        lax.fori_loop(0, ngroups, grouped, 0)
        if group > 1:
            lax.fori_loop(lo + ngroups * group, hi, single, 0)

    def all_chunks(exact):
        near_lo = jnp.maximum(i * nsub - 1, 0)
        near_hi = jnp.minimum((i + 1) * nsub + 1, nkv)
        loop(0, near_lo, FAR_GROUP, tile=False, side=0, mask=False, exact=exact)
        loop(near_lo, near_hi, 1, tile=True, side=0, mask=True, exact=exact)
        loop(near_hi, nkv - 1, FAR_GROUP, tile=False, side=1, mask=False, exact=exact)
        loop(jnp.maximum(near_hi, nkv - 1), nkv, 1, tile=False, side=1, mask=True, exact=exact)

    acc_ref[...] = jnp.zeros_like(acc_ref)
    excess_ref[...] = jnp.zeros_like(excess_ref)
    for mp in range(2):
        s0 = lax.dot_general(k_ref[pl.ds(0, T), :], q_maps[mp], _NT, preferred_element_type=F32)
        m_ref[mp] = jnp.max(s0, axis=0, keepdims=True) + consts_ref[h, 2 * mp]
    all_chunks(exact=False)

    @pl.when(jnp.max(excess_ref[...]) > STALE_MAX_LIMIT)
    def _():
        acc_ref[...] = jnp.zeros_like(acc_ref)
        m_ref[...] = jnp.full_like(m_ref, NEG)
        all_chunks(exact=True)

    lamv = lamv_ref[...]
    lam = (jnp.exp(jnp.sum(lamv[0:1] * lamv[1:2], axis=-1, keepdims=True))
           - jnp.exp(jnp.sum(lamv[2:3] * lamv[3:4], axis=-1, keepdims=True)) + LAM_INIT)
    o = (acc_ref[0, :LANES] / acc_ref[0, LANES:LANES + 1]
         - lam * (acc_ref[1, :LANES] / acc_ref[1, LANES:LANES + 1]))
    ms = jnp.mean(o * o, axis=0, keepdims=True)
    y = o * lax.rsqrt(ms + RMS_EPS) * g_ref[...] * (1.0 - LAM_INIT)
    o_ref[...] = y.T.astype(BF16)


def _diff_attention(proj, vt, consts, lamv, bias_t, subln_g, batch, P, seq_len, nsub):
    T = SEQ_TILE
    tq = nsub * T
    nq = P // tq
    nkv = pl.cdiv(seq_len, T)
    kern = functools.partial(_diff_kernel, seq_len=seq_len, nkv=nkv, nsub=nsub)
    return pl.pallas_call(
        kern,
        grid=(batch, DIFF_HEADS, nq),
        in_specs=[
            pl.BlockSpec(memory_space=pltpu.SMEM),
            pl.BlockSpec((4, HEAD_DIM), lambda b, h, i: (0, 0)),
            pl.BlockSpec((tq, LANES), lambda b, h, i: (b * nq + i, h)),
            pl.BlockSpec((P, LANES), lambda b, h, i: (b, DIFF_HEADS + h)),
            pl.BlockSpec((1, VT_ROWS, P), lambda b, h, i: (h, 0, b)),
            pl.BlockSpec((1, 2, 5, T, T), lambda b, h, i: (h, 0, 0, 0, 0)),
            pl.BlockSpec((LANES, 1), lambda b, h, i: (0, 0)),
        ],
        out_specs=pl.BlockSpec((tq, LANES), lambda b, h, i: (b * nq + i, h)),
        out_shape=jax.ShapeDtypeStruct((batch * P, DIFF_WIDTH), BF16),
        scratch_shapes=[
            pltpu.VMEM((2, VT_ROWS, tq), F32),
            pltpu.VMEM((2, 1, tq), F32),
            pltpu.VMEM((2, 1, tq), F32),
        ],
        compiler_params=_cparams(("parallel", "parallel", "parallel")),
        name="diff_attn",
    )(consts, lamv, proj, proj, vt, bias_t, subln_g)


def _swa_kernel(sink_ref, q_ref, k_ref, v_ref, bias_ref, o_ref, *, seq_len, P):
    T = SEQ_TILE
    i = pl.program_id(1)
    start = pl.multiple_of(jnp.clip(i * T - WINDOW, 0, P - SWA_KEYS), LANES)
    boff = pl.multiple_of(start - i * T + T, LANES)

    def attend(masked):
        kw = k_ref[pl.ds(start, SWA_KEYS), :]
        vw = v_ref[pl.ds(start, SWA_KEYS), :]
        kvalid = (start + lax.broadcasted_iota(jnp.int32, (1, SWA_KEYS), 1)) < seq_len
        lane = lax.broadcasted_iota(jnp.int32, (T, LANES), 1)
        for j in range(SWA_GROUP):
            qp = q_ref[:, j * LANES:(j + 1) * LANES]
            zero = jnp.zeros_like(qp)
            outs = []
            for kvh in range(SWA_KV_HEADS):
                head = kvh * SWA_GROUP + j
                in_half = (lane < HEAD_DIM) if kvh == 0 else (lane >= HEAD_DIM)
                qe = jnp.where(in_half, qp, zero)
                s = lax.dot_general(qe, kw, _NT, preferred_element_type=F32)
                s = s + bias_ref[head, :, pl.ds(boff, SWA_KEYS)]
                if masked:
                    s = jnp.where(kvalid, s, NEG)
                sink = sink_ref[head]
                m = jnp.maximum(jnp.max(s, axis=-1, keepdims=True), sink)
                p = jnp.exp2(s - m)
                l = jnp.sum(p, axis=-1, keepdims=True) + jnp.exp2(sink - m)
                outs.append(jnp.dot(p.astype(BF16), vw, preferred_element_type=F32) / l)
            o_ref[:, j * LANES:(j + 1) * LANES] = jnp.where(lane < HEAD_DIM, outs[0], outs[1]).astype(BF16)

    has_padding = start + SWA_KEYS > seq_len
    pl.when(has_padding)(lambda: attend(True))
    pl.when(jnp.logical_not(has_padding))(lambda: attend(False))


def _swa_attention(proj, sink, bias_w, batch, P, seq_len):
    T = SEQ_TILE
    nq = P // T
    q_blk = (3 * DIFF_WIDTH) // SWA_WIDTH
    k_blk = (3 * DIFF_WIDTH + SWA_WIDTH) // LANES
    kern = functools.partial(_swa_kernel, seq_len=seq_len, P=P)
    return pl.pallas_call(
        kern,
        grid=(batch, nq),
        in_specs=[
            pl.BlockSpec(memory_space=pltpu.SMEM),
            pl.BlockSpec((T, SWA_WIDTH), lambda b, i: (b * nq + i, q_blk)),
            pl.BlockSpec((P, LANES), lambda b, i: (b, k_blk)),
            pl.BlockSpec((P, LANES), lambda b, i: (b, k_blk + 1)),
            pl.BlockSpec((SWA_Q_HEADS, T, SWA_KEYS + T), lambda b, i: (0, 0, 0)),
        ],
        out_specs=pl.BlockSpec((T, SWA_WIDTH), lambda b, i: (b * nq + i, 0)),
        out_shape=jax.ShapeDtypeStruct((batch * P, SWA_WIDTH), BF16),
        compiler_params=_cparams(("parallel", "parallel")),
        name="swa_attn",
    )(sink, proj, proj, proj, bias_w)


def _out_kernel(do_ref, so_ref, h_ref, w_ref, g_ref, wrh_ref, wrl_ref, h2_ref, m_ref, p3_ref, *, seq_len, tm):
    i = pl.program_id(1)
    h2 = (h_ref[...]
          + jnp.dot(do_ref[...], w_ref[:DIFF_WIDTH, :], preferred_element_type=F32)
          + jnp.dot(so_ref[...], w_ref[DIFF_WIDTH:, :], preferred_element_type=F32))
    h2_ref[...] = h2
    ms = jnp.mean(h2 * h2, axis=-1, keepdims=True)
    mf = h2 * lax.rsqrt(ms + RMS_EPS) * g_ref[...]
    mf_hi = mf.astype(BF16)
    m_ref[...] = mf_hi
    mf_lo = (mf - mf_hi.astype(F32)).astype(BF16)
    logits = (jnp.dot(mf_hi, wrh_ref[...], preferred_element_type=F32)
              + jnp.dot(mf_lo, wrh_ref[...], preferred_element_type=F32)
              + jnp.dot(mf_hi, wrl_ref[...], preferred_element_type=F32))
    lane = lax.broadcasted_iota(jnp.int32, logits.shape, 1)
    logits = jnp.where(lane < N_EXPERTS, logits, NEG)
    e = jnp.exp(logits - jnp.max(logits, axis=-1, keepdims=True))
    probs = (e / jnp.sum(e, axis=-1, keepdims=True)).T[:N_EXPERTS]
    pos = i * tm + lax.broadcasted_iota(jnp.int32, (1, tm), 1)
    probs = jnp.where(pos < seq_len, probs, -1.0)
    for c in range(tm // CHUNK):
        p3_ref[c] = probs[:, c * CHUNK:(c + 1) * CHUNK]


def _out_router(diff_o, swa_o, h0, w_out_bf16, g, w_router, batch, P, seq_len):
    tm = SEQ_TILE
    nq = P // tm
    rows = batch * P
    kern = functools.partial(_out_kernel, seq_len=seq_len, tm=tm)
    wr = jnp.pad(w_router.astype(F32), ((0, 0), (0, LANES - N_EXPERTS)))
    wr_hi = wr.astype(BF16)
    wr_lo = (wr - wr_hi.astype(F32)).astype(BF16)
    return pl.pallas_call(
        kern,
        grid=(batch, nq),
        in_specs=[
            pl.BlockSpec((tm, DIFF_WIDTH), lambda b, i: (b * nq + i, 0)),
            pl.BlockSpec((tm, SWA_WIDTH), lambda b, i: (b * nq + i, 0)),
            pl.BlockSpec((tm, D_MODEL), lambda b, i: (b * nq + i, 0)),
            pl.BlockSpec((MIX_WIDTH, D_MODEL), lambda b, i: (0, 0)),
            pl.BlockSpec((1, D_MODEL), lambda b, i: (0, 0)),
            pl.BlockSpec((D_MODEL, LANES), lambda b, i: (0, 0)),
            pl.BlockSpec((D_MODEL, LANES), lambda b, i: (0, 0)),
        ],
        out_specs=[
            pl.BlockSpec((tm, D_MODEL), lambda b, i: (b * nq + i, 0)),
            pl.BlockSpec((tm, D_MODEL), lambda b, i: (b * nq + i, 0)),
            pl.BlockSpec((tm // CHUNK, N_EXPERTS, CHUNK), lambda b, i: (b * nq + i, 0, 0)),
        ],
        out_shape=[
            jax.ShapeDtypeStruct((rows, D_MODEL), F32),
            jax.ShapeDtypeStruct((rows, D_MODEL), BF16),
            jax.ShapeDtypeStruct((rows // CHUNK, N_EXPERTS, CHUNK), F32),
        ],
        compiler_params=_cparams(("parallel", "parallel")),
        name="out_router",
    )(diff_o, swa_o, h0, w_out_bf16, g, wr_hi, wr_lo)


def _topk_kernel(p3_ref, pos_ref, post_ref, gatet_ref, base_ref, incl_ref, flag_ref, *, capacity, nc):
    E = N_EXPERTS
    probs = p3_ref[...]

    def count(pred):
        part = jnp.sum(jnp.where(pred, 1.0, 0.0), axis=0, keepdims=True)
        return jnp.broadcast_to(jnp.sum(part, axis=-1, keepdims=True), part.shape)

    def key_value(key):
        expo = key >> F32_MANT_BITS
        frac = (key & ((1 << F32_MANT_BITS) - 1)).astype(F32) * (2.0 ** -F32_MANT_BITS)
        frac = jnp.where(expo == 0, frac, 1.0 + frac)
        deficit = F32_EXP_BIAS - jnp.maximum(expo, 1)
        scale = jnp.ones_like(frac)
        for b in range(7):
            scale = scale * jnp.where(((deficit >> b) & 1) == 1, 2.0 ** -(1 << b), 1.0)
        return jnp.where(deficit < 0, 2.0, frac * scale)

    def bisect(_, lohi):
        lo, hi = lohi
        mid = lo + ((hi - lo + 1) >> 1)
        ok = count(probs >= key_value(mid)) >= capacity
        return jnp.where(ok, mid, lo), jnp.where(ok, hi, mid - 1)

    lo0 = jnp.zeros((1, E, CHUNK), jnp.int32)
    hi0 = jnp.full((1, E, CHUNK), 0x7F800000, jnp.int32)
    thr_key, _ = lax.fori_loop(0, 32, bisect, (lo0, hi0))
    thr = key_value(thr_key)
    gt = probs > thr
    eq = probs == thr
    need = capacity - count(gt)[0]

    tri = (lax.broadcasted_iota(jnp.int32, (CHUNK, CHUNK), 0)
           <= lax.broadcasted_iota(jnp.int32, (CHUNK, CHUNK), 1)).astype(BF16)

    def inclusive_prefix(flags):
        f2 = flags.astype(BF16).reshape(nc * E, CHUNK)
        return jnp.dot(f2, tri, preferred_element_type=F32).reshape(nc, E, CHUNK)

    eqf = jnp.where(eq, 1.0, 0.0)
    incl_ref[...] = inclusive_prefix(eqf)
    flag_ref[...] = eqf

    def tie_scan(c, run):
        inc = incl_ref[c]
        e_c = flag_ref[c]
        take = (e_c > 0.0) & ((run + inc - e_c) < need)
        flag_ref[c] = jnp.where(take, 1.0, 0.0)
        return run + jnp.broadcast_to(inc[:, CHUNK - 1:CHUNK], inc.shape)

    lax.fori_loop(0, nc, tie_scan, jnp.zeros((E, CHUNK), F32))
    self = jnp.where(gt, 1.0, flag_ref[...])
    flag_ref[...] = self
    incl_ref[...] = inclusive_prefix(self)

    zpad = jnp.zeros((CHUNK - E, CHUNK), F32)

    def to_token_major(x):
        return jnp.concatenate([x, zpad], axis=0).T[:, :E]

    def pos_scan(c, run):
        inc = incl_ref[c]
        s_c = flag_ref[c]
        sel = s_c > 0.0
        pos = jnp.where(sel, run + inc - s_c, -1.0)
        gate = jnp.where(sel, p3_ref[c], 0.0)
        off = pl.multiple_of(c * CHUNK, CHUNK)
        pos_ref[:, pl.ds(off, CHUNK)] = pos
        post_ref[pl.ds(off, CHUNK), :] = to_token_major(pos)
        gatet_ref[pl.ds(off, CHUNK), :] = to_token_major(gate)
        base_ref[c] = run.astype(jnp.int32)
        return run + jnp.broadcast_to(inc[:, CHUNK - 1:CHUNK], inc.shape)

    lax.fori_loop(0, nc, pos_scan, jnp.zeros((E, CHUNK), F32))


def _topk(probs3, capacity):
    nc = probs3.shape[0]
    nt = nc * CHUNK
    kern = functools.partial(_topk_kernel, capacity=capacity, nc=nc)
    return pl.pallas_call(
        kern,
        out_shape=[
            jax.ShapeDtypeStruct((N_EXPERTS, nt), F32),
            jax.ShapeDtypeStruct((nt, N_EXPERTS), F32),
            jax.ShapeDtypeStruct((nt, N_EXPERTS), F32),
            jax.ShapeDtypeStruct((nc, N_EXPERTS, CHUNK), jnp.int32),
        ],
        scratch_shapes=[
            pltpu.VMEM((nc, N_EXPERTS, CHUNK), F32),
            pltpu.VMEM((nc, N_EXPERTS, CHUNK), F32),
        ],
        compiler_params=pltpu.CompilerParams(vmem_limit_bytes=VMEM_LIMIT),
        name="topk",
    )(probs3)


def _gather_kernel(base_ref, cnt_ref, pos_ref, m_ref, xc_ref, *, nc, sub):
    sb = pl.program_id(1)

    @pl.when(sb == 0)
    def _():
        xc_ref[...] = jnp.zeros_like(xc_ref)

    def expert(ee):
        e = pl.program_id(0) * GATHER_EXPERTS + ee
        first = e * nc + sb * sub

        def copy_rows(u, k, rows):
            erow = lax.broadcasted_iota(jnp.int32, (N_EXPERTS, CHUNK), 0)
            row = lax.broadcasted_iota(jnp.int32, (rows, CHUNK), 0).astype(F32)
            starts = [pl.multiple_of((base_ref[first + u + t] // BF16_ROWS) * BF16_ROWS, BF16_ROWS)
                      for t in range(k)]
            offs = [pl.multiple_of((u + t) * CHUNK, CHUNK) for t in range(k)]
            prow = [jnp.sum(jnp.where(erow == e, pos_ref[:, pl.ds(off, CHUNK)], 0.0), axis=0, keepdims=True)
                    - a.astype(F32) for a, off in zip(starts, offs)]
            onehot = [jnp.where(row == p, 1.0, 0.0).astype(BF16) for p in prow]
            g = [jnp.dot(oh, m_ref[pl.ds(off, CHUNK), :], preferred_element_type=F32)
                 for oh, off in zip(onehot, offs)]
            for a, gt in zip(starts, g):
                xc_ref[ee, pl.ds(a, rows), :] += gt.astype(BF16)

        def single(u):
            few = cnt_ref[first + u] <= FAST_CNT
            pl.when(few)(lambda: copy_rows(u, 1, FAST_ROWS))
            pl.when(jnp.logical_not(few))(lambda: copy_rows(u, 1, WIN_ROWS))

        def group(gi, carry):
            u0 = gi * GATHER_GROUP
            most = functools.reduce(jnp.maximum, [cnt_ref[first + u0 + t] for t in range(GATHER_GROUP)])
            few = most <= FAST_CNT
            pl.when(few)(lambda: copy_rows(u0, GATHER_GROUP, FAST_ROWS))
            pl.when(jnp.logical_not(few))(lambda: copy_rows(u0, GATHER_GROUP, WIN_ROWS))
            return carry

        ngroups = sub // GATHER_GROUP
        lax.fori_loop(0, ngroups, group, 0)
        for u in range(ngroups * GATHER_GROUP, sub):
            single(u)

    for ee in range(GATHER_EXPERTS):
        expert(ee)


def _gather(base_flat, cnt_flat, pos, m, cp):
    nt = pos.shape[1]
    nc = nt // CHUNK
    sub = _largest_divisor(nc, 17)
    ts = sub * CHUNK
    kern = functools.partial(_gather_kernel, nc=nc, sub=sub)
    return pl.pallas_call(
        kern,
        grid_spec=pltpu.PrefetchScalarGridSpec(
            num_scalar_prefetch=2,
            grid=(N_EXPERTS // GATHER_EXPERTS, nc // sub),
            in_specs=[
                pl.BlockSpec((N_EXPERTS, ts), lambda e, s, base, cnt: (0, s)),
                pl.BlockSpec((ts, D_MODEL), lambda e, s, base, cnt: (s, 0)),
            ],
            out_specs=pl.BlockSpec((GATHER_EXPERTS, cp, D_MODEL), lambda e, s, base, cnt: (e, 0, 0)),
        ),
        out_shape=jax.ShapeDtypeStruct((N_EXPERTS, cp, D_MODEL), BF16),
        compiler_params=_cparams(("parallel", "arbitrary")),
        name="gather",
    )(base_flat, cnt_flat, pos, m)


def _ffn_kernel(x_ref, wg_ref, wu_ref, wd_ref, y_ref, acc_ref, *, rows):
    f = pl.program_id(1)
    nf = pl.num_programs(1)

    @pl.when(f == 0)
    def _():
        acc_ref[...] = jnp.zeros_like(acc_ref)

    wg = wg_ref[0].astype(BF16)
    wu = wu_ref[0].astype(BF16)
    wd = wd_ref[0].astype(BF16)
    block = -(-rows // (FFN_ROW_BLOCKS * BF16_ROWS)) * BF16_ROWS
    bounds = list(range(0, rows, block)) + [rows]
    for r0, r1 in zip(bounds[:-1], bounds[1:]):
        x = x_ref[0, r0:r1, :]
        g = lax.dot_general(x, wg, _NT, preferred_element_type=F32)
        u = lax.dot_general(x, wu, _NT, preferred_element_type=F32)
        hid = (g * jax.nn.sigmoid(g) * u).astype(BF16)
        acc_ref[r0:r1, :] += jnp.dot(hid, wd, preferred_element_type=F32)

    @pl.when(f == nf - 1)
    def _():
        y_ref[0, :rows, :] = acc_ref[...].astype(BF16)
        y_ref[0, rows:, :] = jnp.zeros((y_ref.shape[1] - rows, D_MODEL), BF16)


def _ffn(xc, w_gate_t, w_up_t, w_down, rows):
    cp = xc.shape[1]
    kern = functools.partial(_ffn_kernel, rows=rows)
    once = pl.Buffered(1)
    w_spec = pl.BlockSpec((1, FF_TILE, D_MODEL), lambda e, f: (e, f, 0))
    return pl.pallas_call(
        kern,
        grid=(N_EXPERTS, D_FF // FF_TILE),
        in_specs=[pl.BlockSpec((1, cp, D_MODEL), lambda e, f: (e, 0, 0), pipeline_mode=once), w_spec, w_spec, w_spec],
        out_specs=pl.BlockSpec((1, cp, D_MODEL), lambda e, f: (e, 0, 0), pipeline_mode=once),
        out_shape=jax.ShapeDtypeStruct((N_EXPERTS, cp, D_MODEL), BF16),
        scratch_shapes=[pltpu.VMEM((rows, D_MODEL), F32)],
        compiler_params=_cparams(("parallel", "arbitrary")),
        name="ffn",
    )(xc, w_gate_t, w_up_t, w_down)


def _combine_kernel(base_ref, fast_ref, h2_ref, post_ref, gatet_ref, g_ref, expand_ref, yc_ref, o_ref,
                    win_ref, big_ref, acc_ref, carry_ref, sem_ref, big_sem, *, nc, cpb, out_blocks):
    E = N_EXPERTS
    b = pl.program_id(0)
    c = pl.program_id(1)
    step = b * cpb + c
    nsteps = pl.num_programs(0) * cpb
    slot = step % 2

    def window_start(chunk, e):
        return pl.multiple_of((base_ref[e * nc + chunk] // BF16_ROWS) * BF16_ROWS, BF16_ROWS)

    def fast_copy(chunk, e, sl):
        return pltpu.make_async_copy(yc_ref.at[e, pl.ds(window_start(chunk, e), FAST_ROWS), :],
                                     win_ref.at[sl, pl.ds(e * FAST_ROWS, FAST_ROWS), :], sem_ref.at[sl, e])

    @pl.when((step == 0) & (fast_ref[0] == 1))
    def _():
        for e in range(E):
            fast_copy(0, e, 0).start()

    nxt = jnp.minimum(step + 1, nsteps - 1)

    @pl.when((step + 1 < nsteps) & (fast_ref[nxt] == 1))
    def _():
        for e in range(E):
            fast_copy(step + 1, e, 1 - slot).start()

    is_fast = fast_ref[step] == 1

    @pl.when(is_fast)
    def _():
        for e in range(E):
            fast_copy(step, e, slot).wait()
        lane_e = lax.broadcasted_iota(jnp.int32, (1, E), 1)
        a_vec = jnp.zeros((1, E), F32)
        for e in range(E):
            a_vec = jnp.where(lane_e == e, window_start(step, e).astype(F32), a_vec)
        pos = post_ref[...]
        rel = jnp.where(pos >= 0.0, pos - a_vec, -1.0).astype(BF16)
        gate = gatet_ref[...]
        g_hi = gate.astype(BF16)
        g_lo = (gate - g_hi.astype(F32)).astype(BF16)
        expand = expand_ref[...]
        spread = lambda x: jnp.dot(x, expand, preferred_element_type=F32)
        slot_row = (lax.broadcasted_iota(jnp.int32, (1, E * FAST_ROWS), 1) % FAST_ROWS).astype(F32)
        match = spread(rel) == slot_row
        w = win_ref[slot]
        acc_ref[...] = (h2_ref[...]
                        + jnp.dot(jnp.where(match, spread(g_hi), 0.0).astype(BF16), w, preferred_element_type=F32)
                        + jnp.dot(jnp.where(match, spread(g_lo), 0.0).astype(BF16), w, preferred_element_type=F32))

    @pl.when(jnp.logical_not(is_fast))
    def _():
        acc = h2_ref[...]
        col = lax.broadcasted_iota(jnp.int32, (CHUNK, WIN_ROWS), 1).astype(F32)
        for e in range(E):
            a = window_start(step, e)
            copy = pltpu.make_async_copy(yc_ref.at[e, pl.ds(a, WIN_ROWS), :], big_ref, big_sem)
            copy.start()
            copy.wait()
            onehot = jnp.where(col == post_ref[:, e:e + 1] - a.astype(F32), 1.0, 0.0).astype(BF16)
            acc = acc + gatet_ref[:, e:e + 1] * jnp.dot(onehot, big_ref[...], preferred_element_type=F32)
        acc_ref[...] = acc

    acc = acc_ref[...]
    ms = jnp.mean(acc * acc, axis=-1, keepdims=True)
    y = acc * lax.rsqrt(ms + RMS_EPS) * g_ref[...]

    @pl.when((c >= 1) & (c <= out_blocks))
    def _():
        o_ref[0, :CHUNK - N_META, :] = carry_ref[N_META:, :]
        o_ref[0, CHUNK - N_META:, :] = y[:N_META, :]

    carry_ref[...] = y


def _combine(base_flat, fast, h2, post, gatet, g, yc, batch, P, S):
    nt = h2.shape[0]
    nc = nt // CHUNK
    cpb = P // CHUNK
    out_blocks = S // CHUNK
    stacked = N_EXPERTS * FAST_ROWS
    expand = jnp.asarray(np.arange(stacked)[None, :] // FAST_ROWS == np.arange(N_EXPERTS)[:, None], BF16)
    kern = functools.partial(_combine_kernel, nc=nc, cpb=cpb, out_blocks=out_blocks)
    row_blk = lambda b, c, base, fast: (b * cpb + c, 0)
    return pl.pallas_call(
        kern,
        grid_spec=pltpu.PrefetchScalarGridSpec(
            num_scalar_prefetch=2,
            grid=(batch, cpb),
            in_specs=[
                pl.BlockSpec((CHUNK, D_MODEL), row_blk),
                pl.BlockSpec((CHUNK, N_EXPERTS), row_blk),
                pl.BlockSpec((CHUNK, N_EXPERTS), row_blk),
                pl.BlockSpec((1, D_MODEL), lambda b, c, base, fast: (0, 0)),
                pl.BlockSpec((N_EXPERTS, stacked), lambda b, c, base, fast: (0, 0)),
                pl.BlockSpec(memory_space=pl.ANY),
            ],
            out_specs=pl.BlockSpec(
                (1, CHUNK, D_MODEL), lambda b, c, base, fast: (b, jnp.clip(c - 1, 0, out_blocks - 1), 0)),
            scratch_shapes=[
                pltpu.VMEM((2, stacked, D_MODEL), BF16),
                pltpu.VMEM((WIN_ROWS, D_MODEL), BF16),
                pltpu.VMEM((CHUNK, D_MODEL), F32),
                pltpu.VMEM((CHUNK, D_MODEL), F32),
                pltpu.SemaphoreType.DMA((2, N_EXPERTS)),
                pltpu.SemaphoreType.DMA,
            ],
        ),
        out_shape=jax.ShapeDtypeStruct((batch, S, D_MODEL), F32),
        compiler_params=_cparams(("arbitrary", "arbitrary")),
        name="combine",
    )(base_flat, fast, h2, post, gatet, g, expand, yc)


def _swa_head_perm():
    perm = np.arange(SWA_WIDTH).reshape(SWA_KV_HEADS, SWA_GROUP, HEAD_DIM)
    return perm.transpose(1, 0, 2).reshape(-1)


def _prep_params(rel_bias, w_in, w_out):
    perm = _swa_head_perm()
    scale = np.ones((IN_COLS,), np.float32)
    scale[:DIFF_WIDTH] = HEAD_DIM ** -0.5
    scale[3 * DIFF_WIDTH:3 * DIFF_WIDTH + SWA_WIDTH] = HEAD_DIM ** -0.5
    cols = np.arange(IN_COLS)
    cols[3 * DIFF_WIDTH:3 * DIFF_WIDTH + SWA_WIDTH] = 3 * DIFF_WIDTH + perm
    w_in_p = (w_in * scale)[:, cols].astype(BF16)
    rows = np.arange(MIX_WIDTH)
    rows[DIFF_WIDTH:] = DIFF_WIDTH + perm
    w_out_p = w_out[rows, :].astype(BF16)

    T = SEQ_TILE
    table = rel_bias.astype(F32)

    def lookup(idx, cols):
        onehot = (idx[..., None] == jnp.arange(N_BUCKETS)).astype(F32)
        return jnp.einsum("...b,bc->...c", onehot, table[:, cols], precision=lax.Precision.HIGHEST)

    kk = jnp.arange(T)[:, None]
    qq = jnp.arange(T)[None, :]
    thresholds = _bucket_thresholds(4 * T)
    idx = jnp.stack([_rel_bucket(d * T + kk - qq, thresholds) for d in (-2, -1, 0, 1, 2)])
    diff_tiles = lookup(idx, slice(0, 2 * DIFF_HEADS))
    diff_tiles = diff_tiles.transpose(3, 0, 1, 2).reshape(DIFF_HEADS, 2, 5, T, T) * LOG2E
    half = N_BUCKETS // 2
    diff_consts = jnp.stack([table[half - 1, 0:2 * DIFF_HEADS:2], table[N_BUCKETS - 1, 0:2 * DIFF_HEADS:2],
                             table[half - 1, 1:2 * DIFF_HEADS:2], table[N_BUCKETS - 1, 1:2 * DIFF_HEADS:2]],
                            axis=1) * LOG2E
    rel = jnp.arange(SWA_KEYS + T)[None, :] - T - jnp.arange(T)[:, None]
    swa_tiles = lookup(_rel_bucket(rel, thresholds), slice(2 * DIFF_HEADS, None))
    swa_tiles = jnp.where((jnp.abs(rel) <= WINDOW)[:, :, None], swa_tiles * LOG2E, NEG).transpose(2, 0, 1)
    return w_in_p, w_out_p, diff_tiles, diff_consts, swa_tiles


def _plan_rows(seq_len):
    nsub = min((4, 5, 6), key=lambda n: (-(-seq_len // (n * SEQ_TILE)) * n, -n))
    return nsub, -(-seq_len // (nsub * SEQ_TILE)) * nsub * SEQ_TILE


def _trunk(x, meta_tokens, prep, attn_norm_g, lamv, subln_g, swa_sink, ffn_norm_g, w_router,
           w_gate, w_up, w_down, final_norm_g):
    w_in_p, w_out_p, diff_tiles, diff_consts, swa_tiles = prep
    B, S, _ = x.shape
    L = S + N_META
    nsub, P = _plan_rows(L)
    rows = B * P
    capacity = EC_FACTOR * (B * L) // N_EXPERTS
    ffn_rows = -(-capacity // BF16_ROWS) * BF16_ROWS
    cp = -(-(capacity + WIN_ROWS) // BF16_ROWS) * BF16_ROWS

    h0 = jnp.concatenate([jnp.broadcast_to(meta_tokens.astype(x.dtype)[None], (B, N_META, D_MODEL)), x,
                          jnp.zeros((B, P - L, D_MODEL), x.dtype)], axis=1).reshape(rows, D_MODEL)

    tm = SEQ_TILE * _largest_divisor(rows // SEQ_TILE, 2)
    proj, vt = _proj(h0, attn_norm_g.reshape(1, D_MODEL), w_in_p, tm)
    diff_o = _diff_attention(proj, vt, diff_consts, lamv, diff_tiles, subln_g.reshape(LANES, 1), B, P, L, nsub)
    swa_o = _swa_attention(proj, swa_sink * LOG2E, swa_tiles, B, P, L)
    h2, m, probs3 = _out_router(diff_o, swa_o, h0, w_out_p, ffn_norm_g.reshape(1, D_MODEL), w_router, B, P, L)
    pos, post, gatet, base3 = _topk(probs3, capacity)
    base_ec = base3[:, :, 0].T
    nxt = jnp.concatenate([base_ec[:, 1:], jnp.full((N_EXPERTS, 1), capacity, jnp.int32)], axis=1)
    cnt_ec = nxt - base_ec
    fast = jnp.all(cnt_ec <= FAST_CNT, axis=0).astype(jnp.int32)
    base_flat = base_ec.reshape(-1)
    xc = _gather(base_flat, cnt_ec.reshape(-1), pos, m, cp)
    yc = _ffn(xc, w_gate, w_up, w_down, ffn_rows)
    return _combine(base_flat, fast, h2, post, gatet, final_norm_g.reshape(1, D_MODEL), yc, B, P, S)


def kernel(x_prompt, x_sample, meta_tokens, rel_bias, attn_norm_g, w_in, diff_lambda_q1, diff_lambda_k1,
           diff_lambda_q2, diff_lambda_k2, diff_subln_g, swa_sink, w_out, ffn_norm_g, w_router, w_gate, w_up,
           w_down, final_norm_g):
    prep = _prep_params(rel_bias, w_in[0], w_out[0])
    lamv = jnp.stack([diff_lambda_q1[0], diff_lambda_k1[0], diff_lambda_q2[0], diff_lambda_k2[0]]).astype(F32)
    args = (meta_tokens, prep, attn_norm_g[0], lamv, diff_subln_g[0], swa_sink[0].astype(F32), ffn_norm_g[0],
            w_router[0], jnp.swapaxes(w_gate[0], 1, 2), jnp.swapaxes(w_up[0], 1, 2), w_down[0], final_norm_g)
    return (_trunk(x_prompt, *args), _trunk(x_sample, *args))
```

```python
import functools
import math

import jax
import jax.numpy as jnp
import numpy as np
from jax import lax
from jax.experimental import pallas as pl
from jax.experimental.pallas import tpu as pltpu

F32 = jnp.float32
BF16 = jnp.bfloat16

D_MODEL = 1024
HEAD_DIM = 64
N_META = 16
DIFF_HEADS = 4
DIFF_WIDTH = DIFF_HEADS * 2 * HEAD_DIM
SWA_Q_HEADS = 8
SWA_KV_HEADS = 2
SWA_GROUP = SWA_Q_HEADS // SWA_KV_HEADS
SWA_WIDTH = SWA_Q_HEADS * HEAD_DIM
SWA_KV_WIDTH = SWA_KV_HEADS * HEAD_DIM
MIX_WIDTH = DIFF_WIDTH + SWA_WIDTH
IN_COLS = 3 * DIFF_WIDTH + SWA_WIDTH + 2 * SWA_KV_WIDTH
WINDOW = 128
N_BUCKETS = 32
MAX_DISTANCE = 128
N_EXPERTS = 16
EC_FACTOR = 2
D_FF = 2752
RMS_EPS = 1e-6
LAM_INIT = 0.8 - 0.6 * math.exp(-0.3 * 0)

LANES = 128
SUBLANES = 8
F32_MANT_BITS = 23
F32_EXP_BIAS = 127
BF16_ROWS = 16
VMEM_LIMIT = 56 * 1024 * 1024

SEQ_TILE = 256
SWA_KEYS = SEQ_TILE + 2 * WINDOW
FAR_GROUP = 4
MXU_TILE = 256
FF_TILE = 3 * MXU_TILE
FFN_ROW_BLOCKS = 2
CHUNK = LANES
VT_ROWS = LANES + BF16_ROWS
LOG2E = math.log2(math.e)
WIN_ROWS = CHUNK + BF16_ROWS
FAST_ROWS = 64
FAST_CNT = FAST_ROWS - BF16_ROWS
GATHER_GROUP = 4
GATHER_EXPERTS = 2
STALE_MAX_LIMIT = 64.0
NEG = -1e30

_NT = (((1,), (1,)), ((), ()))


def _cparams(sem):
    return pltpu.CompilerParams(dimension_semantics=sem, vmem_limit_bytes=VMEM_LIMIT)


def _largest_divisor(n, cap):
    return max(d for d in range(1, cap + 1) if n % d == 0)


def _bucket_thresholds(max_rel):
    half = N_BUCKETS // 2
    max_exact = half // 2
    n = jnp.arange(max_rel)
    large = max_exact + (jnp.log(jnp.maximum(n, 1).astype(F32) / max_exact)
                         / math.log(MAX_DISTANCE / max_exact) * (half - max_exact)).astype(jnp.int32)
    mag = jnp.where(n < max_exact, n, jnp.minimum(large, half - 1))
    mag = jnp.arange(half)[mag]
    return jnp.sum(mag[None, :] < jnp.arange(1, half)[:, None], axis=1)


def _rel_bucket(rel, thresholds):
    mag = jnp.sum(jnp.abs(rel)[..., None] >= thresholds, axis=-1)
    return jnp.where(rel > 0, N_BUCKETS // 2, 0) + mag


def _proj_kernel(h_ref, g_ref, w_ref, proj_ref, vt_ref):
    x = h_ref[...]
    ms = jnp.mean(x * x, axis=-1, keepdims=True)
    a = (x * lax.rsqrt(ms + RMS_EPS) * g_ref[...]).astype(BF16)
    proj = jnp.dot(a, w_ref[...], preferred_element_type=F32)
    q_lo, q_hi = 3 * DIFF_WIDTH, 3 * DIFF_WIDTH + SWA_WIDTH
    proj_ref[:, :DIFF_WIDTH] = (proj[:, :DIFF_WIDTH] * LOG2E).astype(BF16)
    proj_ref[:, DIFF_WIDTH:q_lo] = proj[:, DIFF_WIDTH:q_lo].astype(BF16)
    proj_ref[:, q_lo:q_hi] = (proj[:, q_lo:q_hi] * LOG2E).astype(BF16)
    proj_ref[:, q_hi:] = proj[:, q_hi:].astype(BF16)
    ones = jnp.ones((VT_ROWS - LANES, x.shape[0]), BF16)
    for h in range(DIFF_HEADS):
        v = proj[:, 2 * DIFF_WIDTH + h * LANES: 2 * DIFF_WIDTH + (h + 1) * LANES]
        vt_ref[h, :LANES, :] = v.T.astype(BF16)
        vt_ref[h, LANES:, :] = ones


def _proj(h0, g, w_in_bf16, tm):
    rows = h0.shape[0]
    return pl.pallas_call(
        _proj_kernel,
        grid=(rows // tm,),
        in_specs=[
            pl.BlockSpec((tm, D_MODEL), lambda i: (i, 0)),
            pl.BlockSpec((1, D_MODEL), lambda i: (0, 0)),
            pl.BlockSpec((D_MODEL, IN_COLS), lambda i: (0, 0)),
        ],
        out_specs=[
            pl.BlockSpec((tm, IN_COLS), lambda i: (i, 0)),
            pl.BlockSpec((DIFF_HEADS, VT_ROWS, tm), lambda i: (0, 0, i)),
        ],
        out_shape=[
            jax.ShapeDtypeStruct((rows, IN_COLS), BF16),
            jax.ShapeDtypeStruct((DIFF_HEADS, VT_ROWS, rows), BF16),
        ],
        compiler_params=_cparams(("parallel",)),
        name="proj",
    )(h0, g, w_in_bf16)


def _diff_kernel(consts_ref, lamv_ref, q_ref, k_ref, vt_ref, bias_ref, g_ref, o_ref,
                 acc_ref, m_ref, excess_ref, *, seq_len, nkv, nsub):
    T = SEQ_TILE
    h = pl.program_id(1)
    i = pl.program_id(2)
    q = q_ref[...]
    lane = lax.broadcasted_iota(jnp.int32, q.shape, 1)
    zero = jnp.zeros_like(q)
    q_maps = (jnp.where(lane < HEAD_DIM, q, zero), jnp.where(lane >= HEAD_DIM, q, zero))

    def chunks(j0, count, *, tile, side, mask, exact):
        maps = range(2)
        c = [0.0, 0.0] if tile else [consts_ref[h, 2 * mp + side] for mp in maps]
        offs = [pl.multiple_of((j0 + t) * T, T) for t in range(count)]
        s = [[lax.dot_general(k_ref[pl.ds(off, T), :], q_maps[mp], _NT, preferred_element_type=F32)
              for mp in maps] for off in offs]
        m_run = [m_ref[mp] for mp in maps]
        lead = list(m_run)
        peak = [None, None]
        pv_sum = [None, None]
        for t, off in enumerate(offs):
            vb = vt_ref[0, :, pl.ds(off, T)]
            if mask:
                kvalid = (off + lax.broadcasted_iota(jnp.int32, (T, 1), 0)) < seq_len
            for mp in maps:
                st = s[t][mp]
                if tile:
                    st = jnp.concatenate(
                        [st[:, u * T:(u + 1) * T] + bias_ref[0, mp, jnp.clip(j0 + t - (i * nsub + u), -2, 2) + 2]
                         for u in range(nsub)], axis=1)
                if mask:
                    st = jnp.where(kvalid, st, NEG)
                top = jnp.max(st, axis=0, keepdims=True) + c[mp]
                if exact:
                    m_new = jnp.maximum(m_run[mp], top)
                    alpha = jnp.exp2(m_run[mp] - m_new)
                    m_run[mp] = m_new
                    p = jnp.exp2(st - (m_new - c[mp])).astype(BF16)
                    acc_ref[mp] = alpha * acc_ref[mp] + jnp.dot(vb, p, preferred_element_type=F32)
                else:
                    p = jnp.exp2(st - (lead[mp] - c[mp])).astype(BF16)
                    pv = jnp.dot(vb, p, preferred_element_type=F32)
                    pv_sum[mp] = pv if t == 0 else pv_sum[mp] + pv
                    peak[mp] = top if t == 0 else jnp.maximum(peak[mp], top)
        for mp in maps:
            if not exact:
                m_run[mp] = jnp.maximum(lead[mp], peak[mp])
                excess_ref[mp] = jnp.maximum(excess_ref[mp], peak[mp] - lead[mp])
                acc_ref[mp] = (acc_ref[mp] + pv_sum[mp]) * jnp.exp2(lead[mp] - m_run[mp])
            m_ref[mp] = m_run[mp]

    def loop(lo, hi, group, **kw):
        ngroups = jnp.maximum(hi - lo, 0) // group

        def grouped(g, carry):
            chunks(lo + g * group, group, **kw)
            return carry

        def single(j, carry):
            chunks(j, 1, **kw)
            return carry

        lax.fori_loop(0, ngroups, grouped, 0)
        if group > 1:
            lax.fori_loop(lo + ngroups * group, hi, single, 0)

    def all_chunks(exact):
        near_lo = jnp.maximum(i * nsub - 1, 0)
        near_hi = jnp.minimum((i + 1) * nsub + 1, nkv)
        loop(0, near_lo, FAR_GROUP, tile=False, side=0, mask=False, exact=exact)
        loop(near_lo, near_hi, 1, tile=True, side=0, mask=True, exact=exact)
        loop(near_hi, nkv - 1, FAR_GROUP, tile=False, side=1, mask=False, exact=exact)
        loop(jnp.maximum(near_hi, nkv - 1), nkv, 1, tile=False, side=1, mask=True, exact=exact)

    acc_ref[...] = jnp.zeros_like(acc_ref)
    excess_ref[...] = jnp.zeros_like(excess_ref)
    for mp in range(2):
        s0 = lax.dot_general(k_ref[pl.ds(0, T), :], q_maps[mp], _NT, preferred_element_type=F32)
        m_ref[mp] = jnp.max(s0, axis=0, keepdims=True) + consts_ref[h, 2 * mp]
    all_chunks(exact=False)

    @pl.when(jnp.max(excess_ref[...]) > STALE_MAX_LIMIT)
    def _():
        acc_ref[...] = jnp.zeros_like(acc_ref)
        m_ref[...] = jnp.full_like(m_ref, NEG)
        all_chunks(exact=True)

    lamv = lamv_ref[...]
    lam = (jnp.exp(jnp.sum(lamv[0:1] * lamv[1:2], axis=-1, keepdims=True))
           - jnp.exp(jnp.sum(lamv[2:3] * lamv[3:4], axis=-1, keepdims=True)) + LAM_INIT)
    o = (acc_ref[0, :LANES] / acc_ref[0, LANES:LANES + 1]
         - lam * (acc_ref[1, :LANES] / acc_ref[1, LANES:LANES + 1]))
    ms = jnp.mean(o * o, axis=0, keepdims=True)
    y = o * lax.rsqrt(ms + RMS_EPS) * g_ref[...] * (1.0 - LAM_INIT)
    o_ref[...] = y.T.astype(BF16)


def _diff_attention(proj, vt, consts, lamv, bias_t, subln_g, batch, P, seq_len, nsub):
    T = SEQ_TILE
    tq = nsub * T
    nq = P // tq
    nkv = pl.cdiv(seq_len, T)
    kern = functools.partial(_diff_kernel, seq_len=seq_len, nkv=nkv, nsub=nsub)
    return pl.pallas_call(
        kern,
        grid=(batch, DIFF_HEADS, nq),
        in_specs=[
            pl.BlockSpec(memory_space=pltpu.SMEM),
            pl.BlockSpec((4, HEAD_DIM), lambda b, h, i: (0, 0)),
            pl.BlockSpec((tq, LANES), lambda b, h, i: (b * nq + i, h)),
            pl.BlockSpec((P, LANES), lambda b, h, i: (b, DIFF_HEADS + h)),
            pl.BlockSpec((1, VT_ROWS, P), lambda b, h, i: (h, 0, b)),
            pl.BlockSpec((1, 2, 5, T, T), lambda b, h, i: (h, 0, 0, 0, 0)),
            pl.BlockSpec((LANES, 1), lambda b, h, i: (0, 0)),
        ],
        out_specs=pl.BlockSpec((tq, LANES), lambda b, h, i: (b * nq + i, h)),
        out_shape=jax.ShapeDtypeStruct((batch * P, DIFF_WIDTH), BF16),
        scratch_shapes=[
            pltpu.VMEM((2, VT_ROWS, tq), F32),
            pltpu.VMEM((2, 1, tq), F32),
            pltpu.VMEM((2, 1, tq), F32),
        ],
        compiler_params=_cparams(("parallel", "parallel", "parallel")),
        name="diff_attn",
    )(consts, lamv, proj, proj, vt, bias_t, subln_g)


def _swa_kernel(sink_ref, q_ref, k_ref, v_ref, bias_ref, o_ref, *, seq_len, P):
    T = SEQ_TILE
    i = pl.program_id(1)
    start = pl.multiple_of(jnp.clip(i * T - WINDOW, 0, P - SWA_KEYS), LANES)
    boff = pl.multiple_of(start - i * T + T, LANES)

    def attend(masked):
        kw = k_ref[pl.ds(start, SWA_KEYS), :]
        vw = v_ref[pl.ds(start, SWA_KEYS), :]
        kvalid = (start + lax.broadcasted_iota(jnp.int32, (1, SWA_KEYS), 1)) < seq_len
        lane = lax.broadcasted_iota(jnp.int32, (T, LANES), 1)
        for j in range(SWA_GROUP):
            qp = q_ref[:, j * LANES:(j + 1) * LANES]
            zero = jnp.zeros_like(qp)
            outs = []
            for kvh in range(SWA_KV_HEADS):
                head = kvh * SWA_GROUP + j
                in_half = (lane < HEAD_DIM) if kvh == 0 else (lane >= HEAD_DIM)
                qe = jnp.where(in_half, qp, zero)
                s = lax.dot_general(qe, kw, _NT, preferred_element_type=F32)
                s = s + bias_ref[head, :, pl.ds(boff, SWA_KEYS)]
                if masked:
                    s = jnp.where(kvalid, s, NEG)
                sink = sink_ref[head]
                m = jnp.maximum(jnp.max(s, axis=-1, keepdims=True), sink)
                p = jnp.exp2(s - m)
                l = jnp.sum(p, axis=-1, keepdims=True) + jnp.exp2(sink - m)
                outs.append(jnp.dot(p.astype(BF16), vw, preferred_element_type=F32) / l)
            o_ref[:, j * LANES:(j + 1) * LANES] = jnp.where(lane < HEAD_DIM, outs[0], outs[1]).astype(BF16)

    has_padding = start + SWA_KEYS > seq_len
    pl.when(has_padding)(lambda: attend(True))
    pl.when(jnp.logical_not(has_padding))(lambda: attend(False))


def _swa_attention(proj, sink, bias_w, batch, P, seq_len):
    T = SEQ_TILE
    nq = P // T
    q_blk = (3 * DIFF_WIDTH) // SWA_WIDTH
    k_blk = (3 * DIFF_WIDTH + SWA_WIDTH) // LANES
    kern = functools.partial(_swa_kernel, seq_len=seq_len, P=P)
    return pl.pallas_call(
        kern,
        grid=(batch, nq),
        in_specs=[
            pl.BlockSpec(memory_space=pltpu.SMEM),
            pl.BlockSpec((T, SWA_WIDTH), lambda b, i: (b * nq + i, q_blk)),
            pl.BlockSpec((P, LANES), lambda b, i: (b, k_blk)),
            pl.BlockSpec((P, LANES), lambda b, i: (b, k_blk + 1)),
            pl.BlockSpec((SWA_Q_HEADS, T, SWA_KEYS + T), lambda b, i: (0, 0, 0)),
        ],
        out_specs=pl.BlockSpec((T, SWA_WIDTH), lambda b, i: (b * nq + i, 0)),
        out_shape=jax.ShapeDtypeStruct((batch * P, SWA_WIDTH), BF16),
        compiler_params=_cparams(("parallel", "parallel")),
        name="swa_attn",
    )(sink, proj, proj, proj, bias_w)


def _out_kernel(do_ref, so_ref, h_ref, w_ref, g_ref, wrh_ref, wrl_ref, h2_ref, m_ref, p3_ref, *, seq_len, tm):
    i = pl.program_id(1)
    h2 = (h_ref[...]
          + jnp.dot(do_ref[...], w_ref[:DIFF_WIDTH, :], preferred_element_type=F32)
          + jnp.dot(so_ref[...], w_ref[DIFF_WIDTH:, :], preferred_element_type=F32))
    h2_ref[...] = h2
    ms = jnp.mean(h2 * h2, axis=-1, keepdims=True)
    mf = h2 * lax.rsqrt(ms + RMS_EPS) * g_ref[...]
    mf_hi = mf.astype(BF16)
    m_ref[...] = mf_hi
    mf_lo = (mf - mf_hi.astype(F32)).astype(BF16)
    logits = (jnp.dot(mf_hi, wrh_ref[...], preferred_element_type=F32)
              + jnp.dot(mf_lo, wrh_ref[...], preferred_element_type=F32)
              + jnp.dot(mf_hi, wrl_ref[...], preferred_element_type=F32))
    lane = lax.broadcasted_iota(jnp.int32, logits.shape, 1)
    logits = jnp.where(lane < N_EXPERTS, logits, NEG)
    e = jnp.exp(logits - jnp.max(logits, axis=-1, keepdims=True))
    probs = (e / jnp.sum(e, axis=-1, keepdims=True)).T[:N_EXPERTS]
    pos = i * tm + lax.broadcasted_iota(jnp.int32, (1, tm), 1)
    probs = jnp.where(pos < seq_len, probs, -1.0)
    for c in range(tm // CHUNK):
        p3_ref[c] = probs[:, c * CHUNK:(c + 1) * CHUNK]


def _out_router(diff_o, swa_o, h0, w_out_bf16, g, w_router, batch, P, seq_len):
    tm = SEQ_TILE
    nq = P // tm
    rows = batch * P
    kern = functools.partial(_out_kernel, seq_len=seq_len, tm=tm)
    wr = jnp.pad(w_router.astype(F32), ((0, 0), (0, LANES - N_EXPERTS)))
    wr_hi = wr.astype(BF16)
    wr_lo = (wr - wr_hi.astype(F32)).astype(BF16)
    return pl.pallas_call(
        kern,
        grid=(batch, nq),
        in_specs=[
            pl.BlockSpec((tm, DIFF_WIDTH), lambda b, i: (b * nq + i, 0)),
            pl.BlockSpec((tm, SWA_WIDTH), lambda b, i: (b * nq + i, 0)),
            pl.BlockSpec((tm, D_MODEL), lambda b, i: (b * nq + i, 0)),
            pl.BlockSpec((MIX_WIDTH, D_MODEL), lambda b, i: (0, 0)),
            pl.BlockSpec((1, D_MODEL), lambda b, i: (0, 0)),
            pl.BlockSpec((D_MODEL, LANES), lambda b, i: (0, 0)),
            pl.BlockSpec((D_MODEL, LANES), lambda b, i: (0, 0)),
        ],
        out_specs=[
            pl.BlockSpec((tm, D_MODEL), lambda b, i: (b * nq + i, 0)),
            pl.BlockSpec((tm, D_MODEL), lambda b, i: (b * nq + i, 0)),
            pl.BlockSpec((tm // CHUNK, N_EXPERTS, CHUNK), lambda b, i: (b * nq + i, 0, 0)),
        ],
        out_shape=[
            jax.ShapeDtypeStruct((rows, D_MODEL), F32),
            jax.ShapeDtypeStruct((rows, D_MODEL), BF16),
            jax.ShapeDtypeStruct((rows // CHUNK, N_EXPERTS, CHUNK), F32),
        ],
        compiler_params=_cparams(("parallel", "parallel")),
        name="out_router",
    )(diff_o, swa_o, h0, w_out_bf16, g, wr_hi, wr_lo)


def _topk_kernel(p3_ref, pos_ref, post_ref, gatet_ref, base_ref, incl_ref, flag_ref, *, capacity, nc):
    E = N_EXPERTS
    probs = p3_ref[...]

    def count(pred):
        part = jnp.sum(jnp.where(pred, 1.0, 0.0), axis=0, keepdims=True)
        return jnp.broadcast_to(jnp.sum(part, axis=-1, keepdims=True), part.shape)

    def key_value(key):
        expo = key >> F32_MANT_BITS
        frac = (key & ((1 << F32_MANT_BITS) - 1)).astype(F32) * (2.0 ** -F32_MANT_BITS)
        frac = jnp.where(expo == 0, frac, 1.0 + frac)
        deficit = F32_EXP_BIAS - jnp.maximum(expo, 1)
        scale = jnp.ones_like(frac)
        for b in range(7):
            scale = scale * jnp.where(((deficit >> b) & 1) == 1, 2.0 ** -(1 << b), 1.0)
        return jnp.where(deficit < 0, 2.0, frac * scale)

    def bisect(_, lohi):
        lo, hi = lohi
        mid = lo + ((hi - lo + 1) >> 1)
        ok = count(probs >= key_value(mid)) >= capacity
        return jnp.where(ok, mid, lo), jnp.where(ok, hi, mid - 1)

    lo0 = jnp.zeros((1, E, CHUNK), jnp.int32)
    hi0 = jnp.full((1, E, CHUNK), 0x7F800000, jnp.int32)
    thr_key, _ = lax.fori_loop(0, 32, bisect, (lo0, hi0))
    thr = key_value(thr_key)
    gt = probs > thr
    eq = probs == thr
    need = capacity - count(gt)[0]

    tri = (lax.broadcasted_iota(jnp.int32, (CHUNK, CHUNK), 0)
           <= lax.broadcasted_iota(jnp.int32, (CHUNK, CHUNK), 1)).astype(BF16)

    def inclusive_prefix(flags):
        f2 = flags.astype(BF16).reshape(nc * E, CHUNK)
        return jnp.dot(f2, tri, preferred_element_type=F32).reshape(nc, E, CHUNK)

    eqf = jnp.where(eq, 1.0, 0.0)
    incl_ref[...] = inclusive_prefix(eqf)
    flag_ref[...] = eqf

    def tie_scan(c, run):
        inc = incl_ref[c]
        e_c = flag_ref[c]
        take = (e_c > 0.0) & ((run + inc - e_c) < need)
        flag_ref[c] = jnp.where(take, 1.0, 0.0)
        return run + jnp.broadcast_to(inc[:, CHUNK - 1:CHUNK], inc.shape)

    lax.fori_loop(0, nc, tie_scan, jnp.zeros((E, CHUNK), F32))
    self = jnp.where(gt, 1.0, flag_ref[...])
    flag_ref[...] = self
    incl_ref[...] = inclusive_prefix(self)

    zpad = jnp.zeros((CHUNK - E, CHUNK), F32)

    def to_token_major(x):
        return jnp.concatenate([x, zpad], axis=0).T[:, :E]

    def pos_scan(c, run):
        inc = incl_ref[c]
        s_c = flag_ref[c]
        sel = s_c > 0.0
        pos = jnp.where(sel, run + inc - s_c, -1.0)
        gate = jnp.where(sel, p3_ref[c], 0.0)
        off = pl.multiple_of(c * CHUNK, CHUNK)
        pos_ref[:, pl.ds(off, CHUNK)] = pos
        post_ref[pl.ds(off, CHUNK), :] = to_token_major(pos)
        gatet_ref[pl.ds(off, CHUNK), :] = to_token_major(gate)
        base_ref[c] = run.astype(jnp.int32)
        return run + jnp.broadcast_to(inc[:, CHUNK - 1:CHUNK], inc.shape)

    lax.fori_loop(0, nc, pos_scan, jnp.zeros((E, CHUNK), F32))


def _topk(probs3, capacity):
    nc = probs3.shape[0]
    nt = nc * CHUNK
    kern = functools.partial(_topk_kernel, capacity=capacity, nc=nc)
    return pl.pallas_call(
        kern,
        out_shape=[
            jax.ShapeDtypeStruct((N_EXPERTS, nt), F32),
            jax.ShapeDtypeStruct((nt, N_EXPERTS), F32),
            jax.ShapeDtypeStruct((nt, N_EXPERTS), F32),
            jax.ShapeDtypeStruct((nc, N_EXPERTS, CHUNK), jnp.int32),
        ],
        scratch_shapes=[
            pltpu.VMEM((nc, N_EXPERTS, CHUNK), F32),
            pltpu.VMEM((nc, N_EXPERTS, CHUNK), F32),
        ],
        compiler_params=pltpu.CompilerParams(vmem_limit_bytes=VMEM_LIMIT),
        name="topk",
    )(probs3)


def _gather_kernel(base_ref, cnt_ref, pos_ref, m_ref, xc_ref, *, nc, sub):
    sb = pl.program_id(1)

    @pl.when(sb == 0)
    def _():
        xc_ref[...] = jnp.zeros_like(xc_ref)

    def expert(ee):
        e = pl.program_id(0) * GATHER_EXPERTS + ee
        first = e * nc + sb * sub

        def copy_rows(u, k, rows):
            erow = lax.broadcasted_iota(jnp.int32, (N_EXPERTS, CHUNK), 0)
            row = lax.broadcasted_iota(jnp.int32, (rows, CHUNK), 0).astype(F32)
            starts = [pl.multiple_of((base_ref[first + u + t] // BF16_ROWS) * BF16_ROWS, BF16_ROWS)
                      for t in range(k)]
            offs = [pl.multiple_of((u + t) * CHUNK, CHUNK) for t in range(k)]
            prow = [jnp.sum(jnp.where(erow == e, pos_ref[:, pl.ds(off, CHUNK)], 0.0), axis=0, keepdims=True)
                    - a.astype(F32) for a, off in zip(starts, offs)]
            onehot = [jnp.where(row == p, 1.0, 0.0).astype(BF16) for p in prow]
            g = [jnp.dot(oh, m_ref[pl.ds(off, CHUNK), :], preferred_element_type=F32)
                 for oh, off in zip(onehot, offs)]
            for a, gt in zip(starts, g):
                xc_ref[ee, pl.ds(a, rows), :] += gt.astype(BF16)

        def single(u):
            few = cnt_ref[first + u] <= FAST_CNT
            pl.when(few)(lambda: copy_rows(u, 1, FAST_ROWS))
            pl.when(jnp.logical_not(few))(lambda: copy_rows(u, 1, WIN_ROWS))

        def group(gi, carry):
            u0 = gi * GATHER_GROUP
            most = functools.reduce(jnp.maximum, [cnt_ref[first + u0 + t] for t in range(GATHER_GROUP)])
            few = most <= FAST_CNT
            pl.when(few)(lambda: copy_rows(u0, GATHER_GROUP, FAST_ROWS))
            pl.when(jnp.logical_not(few))(lambda: copy_rows(u0, GATHER_GROUP, WIN_ROWS))
            return carry

        ngroups = sub // GATHER_GROUP
        lax.fori_loop(0, ngroups, group, 0)
        for u in range(ngroups * GATHER_GROUP, sub):
            single(u)

    for ee in range(GATHER_EXPERTS):
        expert(ee)


def _gather(base_flat, cnt_flat, pos, m, cp):
    nt = pos.shape[1]
    nc = nt // CHUNK
    sub = _largest_divisor(nc, 17)
    ts = sub * CHUNK
    kern = functools.partial(_gather_kernel, nc=nc, sub=sub)
    return pl.pallas_call(
        kern,
        grid_spec=pltpu.PrefetchScalarGridSpec(
            num_scalar_prefetch=2,
            grid=(N_EXPERTS // GATHER_EXPERTS, nc // sub),
            in_specs=[
                pl.BlockSpec((N_EXPERTS, ts), lambda e, s, base, cnt: (0, s)),
                pl.BlockSpec((ts, D_MODEL), lambda e, s, base, cnt: (s, 0)),
            ],
            out_specs=pl.BlockSpec((GATHER_EXPERTS, cp, D_MODEL), lambda e, s, base, cnt: (e, 0, 0)),
        ),
        out_shape=jax.ShapeDtypeStruct((N_EXPERTS, cp, D_MODEL), BF16),
        compiler_params=_cparams(("parallel", "arbitrary")),
        name="gather",
    )(base_flat, cnt_flat, pos, m)


def _ffn_kernel(x_ref, wg_ref, wu_ref, wd_ref, y_ref, acc_ref, *, rows):
    f = pl.program_id(1)
    nf = pl.num_programs(1)

    @pl.when(f == 0)
    def _():
        acc_ref[...] = jnp.zeros_like(acc_ref)

    block = -(-rows // (FFN_ROW_BLOCKS * BF16_ROWS)) * BF16_ROWS
    bounds = list(range(0, rows, block)) + [rows]

    def step(width, valid):
        wg = wg_ref[0, :width, :].astype(BF16)
        wu = wu_ref[0, :width, :].astype(BF16)
        wd = wd_ref[0, :width, :]
        if valid < width:
            wd = jnp.where(lax.broadcasted_iota(jnp.int32, (width, 1), 0) < valid, wd, 0.0)
        wd = wd.astype(BF16)
        for r0, r1 in zip(bounds[:-1], bounds[1:]):
            x = x_ref[0, r0:r1, :]
            g = lax.dot_general(x, wg, _NT, preferred_element_type=F32)
            u = lax.dot_general(x, wu, _NT, preferred_element_type=F32)
            hid = g * jax.nn.sigmoid(g) * u
            if valid < width:
                hid = jnp.where(lax.broadcasted_iota(jnp.int32, (1, width), 1) < valid, hid, 0.0)
            acc_ref[r0:r1, :] += jnp.dot(hid.astype(BF16), wd, preferred_element_type=F32)

    last_valid = D_FF - (D_FF // FF_TILE) * FF_TILE
    last_width = -(-last_valid // MXU_TILE) * MXU_TILE
    pl.when(f < nf - 1)(lambda: step(FF_TILE, FF_TILE))
    pl.when(f == nf - 1)(lambda: step(last_width, last_valid))

    @pl.when(f == nf - 1)
    def _():
        y_ref[0, :rows, :] = acc_ref[...].astype(BF16)
        y_ref[0, rows:, :] = jnp.zeros((y_ref.shape[1] - rows, D_MODEL), BF16)


def _ffn(xc, w_gate_t, w_up_t, w_down, rows):
    cp = xc.shape[1]
    kern = functools.partial(_ffn_kernel, rows=rows)
    once = pl.Buffered(1)
    w_spec = pl.BlockSpec((1, FF_TILE, D_MODEL), lambda e, f: (e, f, 0))
    return pl.pallas_call(
        kern,
        grid=(N_EXPERTS, pl.cdiv(D_FF, FF_TILE)),
        in_specs=[pl.BlockSpec((1, cp, D_MODEL), lambda e, f: (e, 0, 0), pipeline_mode=once), w_spec, w_spec, w_spec],
        out_specs=pl.BlockSpec((1, cp, D_MODEL), lambda e, f: (e, 0, 0), pipeline_mode=once),
        out_shape=jax.ShapeDtypeStruct((N_EXPERTS, cp, D_MODEL), BF16),
        scratch_shapes=[pltpu.VMEM((rows, D_MODEL), F32)],
        compiler_params=_cparams(("parallel", "arbitrary")),
        name="ffn",
    )(xc, w_gate_t, w_up_t, w_down)


def _combine_kernel(base_ref, fast_ref, h2_ref, post_ref, gatet_ref, g_ref, expand_ref, yc_ref, o_ref,
                    win_ref, big_ref, acc_ref, carry_ref, sem_ref, big_sem, *, nc, cpb, out_blocks):
    E = N_EXPERTS
    b = pl.program_id(0)
    c = pl.program_id(1)
    step = b * cpb + c
    nsteps = pl.num_programs(0) * cpb
    slot = step % 2

    def window_start(chunk, e):
        return pl.multiple_of((base_ref[e * nc + chunk] // BF16_ROWS) * BF16_ROWS, BF16_ROWS)

    def fast_copy(chunk, e, sl):
        return pltpu.make_async_copy(yc_ref.at[e, pl.ds(window_start(chunk, e), FAST_ROWS), :],
                                     win_ref.at[sl, pl.ds(e * FAST_ROWS, FAST_ROWS), :], sem_ref.at[sl, e])

    @pl.when((step == 0) & (fast_ref[0] == 1))
    def _():
        for e in range(E):
            fast_copy(0, e, 0).start()

    nxt = jnp.minimum(step + 1, nsteps - 1)

    @pl.when((step + 1 < nsteps) & (fast_ref[nxt] == 1))
    def _():
        for e in range(E):
            fast_copy(step + 1, e, 1 - slot).start()

    is_fast = fast_ref[step] == 1

    @pl.when(is_fast)
    def _():
        for e in range(E):
            fast_copy(step, e, slot).wait()
        lane_e = lax.broadcasted_iota(jnp.int32, (1, E), 1)
        a_vec = jnp.zeros((1, E), F32)
        for e in range(E):
            a_vec = jnp.where(lane_e == e, window_start(step, e).astype(F32), a_vec)
        pos = post_ref[...]
        rel = jnp.where(pos >= 0.0, pos - a_vec, -1.0).astype(BF16)
        gate = gatet_ref[...]
        g_hi = gate.astype(BF16)
        g_lo = (gate - g_hi.astype(F32)).astype(BF16)
        expand = expand_ref[...]
        spread = lambda x: jnp.dot(x, expand, preferred_element_type=F32)
        slot_row = (lax.broadcasted_iota(jnp.int32, (1, E * FAST_ROWS), 1) % FAST_ROWS).astype(F32)
        match = spread(rel) == slot_row
        w = win_ref[slot]
        acc_ref[...] = (h2_ref[...]
                        + jnp.dot(jnp.where(match, spread(g_hi), 0.0).astype(BF16), w, preferred_element_type=F32)
                        + jnp.dot(jnp.where(match, spread(g_lo), 0.0).astype(BF16), w, preferred_element_type=F32))

    @pl.when(jnp.logical_not(is_fast))
    def _():
        acc = h2_ref[...]
        col = lax.broadcasted_iota(jnp.int32, (CHUNK, WIN_ROWS), 1).astype(F32)
        for e in range(E):
            a = window_start(step, e)
            copy = pltpu.make_async_copy(yc_ref.at[e, pl.ds(a, WIN_ROWS), :], big_ref, big_sem)
            copy.start()
            copy.wait()
            onehot = jnp.where(col == post_ref[:, e:e + 1] - a.astype(F32), 1.0, 0.0).astype(BF16)
            acc = acc + gatet_ref[:, e:e + 1] * jnp.dot(onehot, big_ref[...], preferred_element_type=F32)
        acc_ref[...] = acc

    acc = acc_ref[...]
    ms = jnp.mean(acc * acc, axis=-1, keepdims=True)
    y = acc * lax.rsqrt(ms + RMS_EPS) * g_ref[...]

    @pl.when((c >= 1) & (c <= out_blocks))
    def _():
        o_ref[0, :CHUNK - N_META, :] = carry_ref[N_META:, :]
        o_ref[0, CHUNK - N_META:, :] = y[:N_META, :]

    carry_ref[...] = y


def _combine(base_flat, fast, h2, post, gatet, g, yc, batch, P, S):
    nt = h2.shape[0]
    nc = nt // CHUNK
    cpb = P // CHUNK
    out_blocks = S // CHUNK
    stacked = N_EXPERTS * FAST_ROWS
    expand = jnp.asarray(np.arange(stacked)[None, :] // FAST_ROWS == np.arange(N_EXPERTS)[:, None], BF16)
    kern = functools.partial(_combine_kernel, nc=nc, cpb=cpb, out_blocks=out_blocks)
    row_blk = lambda b, c, base, fast: (b * cpb + c, 0)
    return pl.pallas_call(
        kern,
        grid_spec=pltpu.PrefetchScalarGridSpec(
            num_scalar_prefetch=2,
            grid=(batch, cpb),
            in_specs=[
                pl.BlockSpec((CHUNK, D_MODEL), row_blk),
                pl.BlockSpec((CHUNK, N_EXPERTS), row_blk),
                pl.BlockSpec((CHUNK, N_EXPERTS), row_blk),
                pl.BlockSpec((1, D_MODEL), lambda b, c, base, fast: (0, 0)),
                pl.BlockSpec((N_EXPERTS, stacked), lambda b, c, base, fast: (0, 0)),
                pl.BlockSpec(memory_space=pl.ANY),
            ],
            out_specs=pl.BlockSpec(
                (1, CHUNK, D_MODEL), lambda b, c, base, fast: (b, jnp.clip(c - 1, 0, out_blocks - 1), 0)),
            scratch_shapes=[
                pltpu.VMEM((2, stacked, D_MODEL), BF16),
                pltpu.VMEM((WIN_ROWS, D_MODEL), BF16),
                pltpu.VMEM((CHUNK, D_MODEL), F32),
                pltpu.VMEM((CHUNK, D_MODEL), F32),
                pltpu.SemaphoreType.DMA((2, N_EXPERTS)),
                pltpu.SemaphoreType.DMA,
            ],
        ),
        out_shape=jax.ShapeDtypeStruct((batch, S, D_MODEL), F32),
        compiler_params=_cparams(("arbitrary", "arbitrary")),
        name="combine",
    )(base_flat, fast, h2, post, gatet, g, expand, yc)


def _swa_head_perm():
    perm = np.arange(SWA_WIDTH).reshape(SWA_KV_HEADS, SWA_GROUP, HEAD_DIM)
    return perm.transpose(1, 0, 2).reshape(-1)


def _prep_params(rel_bias, w_in, w_out):
    perm = _swa_head_perm()
    scale = np.ones((IN_COLS,), np.float32)
    scale[:DIFF_WIDTH] = HEAD_DIM ** -0.5
    scale[3 * DIFF_WIDTH:3 * DIFF_WIDTH + SWA_WIDTH] = HEAD_DIM ** -0.5
    cols = np.arange(IN_COLS)
    cols[3 * DIFF_WIDTH:3 * DIFF_WIDTH + SWA_WIDTH] = 3 * DIFF_WIDTH + perm
    w_in_p = (w_in * scale)[:, cols].astype(BF16)
    rows = np.arange(MIX_WIDTH)
    rows[DIFF_WIDTH:] = DIFF_WIDTH + perm
    w_out_p = w_out[rows, :].astype(BF16)

    T = SEQ_TILE
    table = rel_bias.astype(F32)

    def lookup(idx, cols):
        onehot = (idx[..., None] == jnp.arange(N_BUCKETS)).astype(F32)
        return jnp.einsum("...b,bc->...c", onehot, table[:, cols], precision=lax.Precision.HIGHEST)

    kk = jnp.arange(T)[:, None]
    qq = jnp.arange(T)[None, :]
    thresholds = _bucket_thresholds(4 * T)
    idx = jnp.stack([_rel_bucket(d * T + kk - qq, thresholds) for d in (-2, -1, 0, 1, 2)])
    diff_tiles = lookup(idx, slice(0, 2 * DIFF_HEADS))
    diff_tiles = diff_tiles.transpose(3, 0, 1, 2).reshape(DIFF_HEADS, 2, 5, T, T) * LOG2E
    half = N_BUCKETS // 2
    diff_consts = jnp.stack([table[half - 1, 0:2 * DIFF_HEADS:2], table[N_BUCKETS - 1, 0:2 * DIFF_HEADS:2],
                             table[half - 1, 1:2 * DIFF_HEADS:2], table[N_BUCKETS - 1, 1:2 * DIFF_HEADS:2]],
                            axis=1) * LOG2E
    rel = jnp.arange(SWA_KEYS + T)[None, :] - T - jnp.arange(T)[:, None]
    swa_tiles = lookup(_rel_bucket(rel, thresholds), slice(2 * DIFF_HEADS, None))
    swa_tiles = jnp.where((jnp.abs(rel) <= WINDOW)[:, :, None], swa_tiles * LOG2E, NEG).transpose(2, 0, 1)
    return w_in_p, w_out_p, diff_tiles, diff_consts, swa_tiles


def _plan_rows(seq_len):
    nsub = min((4, 5, 6), key=lambda n: (-(-seq_len // (n * SEQ_TILE)) * n, -n))
    return nsub, -(-seq_len // (nsub * SEQ_TILE)) * nsub * SEQ_TILE


def _trunk(x, meta_tokens, prep, attn_norm_g, lamv, subln_g, swa_sink, ffn_norm_g, w_router,
           w_gate, w_up, w_down, final_norm_g):
    w_in_p, w_out_p, diff_tiles, diff_consts, swa_tiles = prep
    B, S, _ = x.shape
    L = S + N_META
    nsub, P = _plan_rows(L)
    rows = B * P
    capacity = EC_FACTOR * (B * L) // N_EXPERTS
    ffn_rows = -(-capacity // BF16_ROWS) * BF16_ROWS
    cp = -(-(capacity + WIN_ROWS) // BF16_ROWS) * BF16_ROWS

    h0 = jnp.concatenate([jnp.broadcast_to(meta_tokens.astype(x.dtype)[None], (B, N_META, D_MODEL)), x,
                          jnp.zeros((B, P - L, D_MODEL), x.dtype)], axis=1).reshape(rows, D_MODEL)

    tm = SEQ_TILE * _largest_divisor(rows // SEQ_TILE, 2)
    proj, vt = _proj(h0, attn_norm_g.reshape(1, D_MODEL), w_in_p, tm)
    diff_o = _diff_attention(proj, vt, diff_consts, lamv, diff_tiles, subln_g.reshape(LANES, 1), B, P, L, nsub)
    swa_o = _swa_attention(proj, swa_sink * LOG2E, swa_tiles, B, P, L)
    h2, m, probs3 = _out_router(diff_o, swa_o, h0, w_out_p, ffn_norm_g.reshape(1, D_MODEL), w_router, B, P, L)
    pos, post, gatet, base3 = _topk(probs3, capacity)
    base_ec = base3[:, :, 0].T
    nxt = jnp.concatenate([base_ec[:, 1:], jnp.full((N_EXPERTS, 1), capacity, jnp.int32)], axis=1)
    cnt_ec = nxt - base_ec
    fast = jnp.all(cnt_ec <= FAST_CNT, axis=0).astype(jnp.int32)
    base_flat = base_ec.reshape(-1)
    xc = _gather(base_flat, cnt_ec.reshape(-1), pos, m, cp)
    yc = _ffn(xc, w_gate, w_up, w_down, ffn_rows)
    return _combine(base_flat, fast, h2, post, gatet, final_norm_g.reshape(1, D_MODEL), yc, B, P, S)


def kernel(x_prompt, x_sample, meta_tokens, rel_bias, attn_norm_g, w_in, diff_lambda_q1, diff_lambda_k1,
           diff_lambda_q2, diff_lambda_k2, diff_subln_g, swa_sink, w_out, ffn_norm_g, w_router, w_gate, w_up,
           w_down, final_norm_g):
    prep = _prep_params(rel_bias, w_in[0], w_out[0])
    lamv = jnp.stack([diff_lambda_q1[0], diff_lambda_k1[0], diff_lambda_q2[0], diff_lambda_k2[0]]).astype(F32)
    args = (meta_tokens, prep, attn_norm_g[0], lamv, diff_subln_g[0], swa_sink[0].astype(F32), ffn_norm_g[0],
            w_router[0], jnp.swapaxes(w_gate[0], 1, 2), jnp.swapaxes(w_up[0], 1, 2), w_down[0], final_norm_g)
    return (_trunk(x_prompt, *args), _trunk(x_sample, *args))
```

```python
import functools
import math

import jax
import jax.numpy as jnp
import numpy as np
from jax import lax
from jax.experimental import pallas as pl
from jax.experimental.pallas import tpu as pltpu

F32 = jnp.float32
BF16 = jnp.bfloat16

D_MODEL = 1024
HEAD_DIM = 64
N_META = 16
DIFF_HEADS = 4
DIFF_WIDTH = DIFF_HEADS * 2 * HEAD_DIM
SWA_Q_HEADS = 8
SWA_KV_HEADS = 2
SWA_GROUP = SWA_Q_HEADS // SWA_KV_HEADS
SWA_WIDTH = SWA_Q_HEADS * HEAD_DIM
SWA_KV_WIDTH = SWA_KV_HEADS * HEAD_DIM
MIX_WIDTH = DIFF_WIDTH + SWA_WIDTH
IN_COLS = 3 * DIFF_WIDTH + SWA_WIDTH + 2 * SWA_KV_WIDTH
WINDOW = 128
N_BUCKETS = 32
MAX_DISTANCE = 128
N_EXPERTS = 16
EC_FACTOR = 2
D_FF = 2752
RMS_EPS = 1e-6
LAM_INIT = 0.8 - 0.6 * math.exp(-0.3 * 0)

LANES = 128
SUBLANES = 8
F32_MANT_BITS = 23
F32_EXP_BIAS = 127
BF16_ROWS = 16
VMEM_LIMIT = 56 * 1024 * 1024

SEQ_TILE = 256
SWA_KEYS = SEQ_TILE + 2 * WINDOW
NEAR_GROUP = 2
FAR_GROUP = 4
MXU_TILE = 256
FF_TILE = 3 * MXU_TILE
FFN_ROW_BLOCKS = 2
CHUNK = LANES
VT_ROWS = LANES + BF16_ROWS
LOG2E = math.log2(math.e)
WIN_ROWS = CHUNK + BF16_ROWS
FAST_ROWS = 64
FAST_CNT = FAST_ROWS - BF16_ROWS
GATHER_GROUP = 4
GATHER_EXPERTS = 2
STALE_MAX_LIMIT = 64.0
NEG = -1e30

_NT = (((1,), (1,)), ((), ()))


def _cparams(sem):
    return pltpu.CompilerParams(dimension_semantics=sem, vmem_limit_bytes=VMEM_LIMIT)


def _largest_divisor(n, cap):
    return max(d for d in range(1, cap + 1) if n % d == 0)


def _bucket_thresholds(max_rel):
    half = N_BUCKETS // 2
    max_exact = half // 2
    n = jnp.arange(max_rel)
    large = max_exact + (jnp.log(jnp.maximum(n, 1).astype(F32) / max_exact)
                         / math.log(MAX_DISTANCE / max_exact) * (half - max_exact)).astype(jnp.int32)
    mag = jnp.where(n < max_exact, n, jnp.minimum(large, half - 1))
    mag = jnp.arange(half)[mag]
    return jnp.sum(mag[None, :] < jnp.arange(1, half)[:, None], axis=1)


def _rel_bucket(rel, thresholds):
    mag = jnp.sum(jnp.abs(rel)[..., None] >= thresholds, axis=-1)
    return jnp.where(rel > 0, N_BUCKETS // 2, 0) + mag


def _proj_kernel(h_ref, g_ref, w_ref, proj_ref, vt_ref):
    x = h_ref[...]
    ms = jnp.mean(x * x, axis=-1, keepdims=True)
    a = (x * lax.rsqrt(ms + RMS_EPS) * g_ref[...]).astype(BF16)
    proj = jnp.dot(a, w_ref[...], preferred_element_type=F32)
    q_lo, q_hi = 3 * DIFF_WIDTH, 3 * DIFF_WIDTH + SWA_WIDTH
    proj_ref[:, :DIFF_WIDTH] = (proj[:, :DIFF_WIDTH] * LOG2E).astype(BF16)
    proj_ref[:, DIFF_WIDTH:q_lo] = proj[:, DIFF_WIDTH:q_lo].astype(BF16)
    proj_ref[:, q_lo:q_hi] = (proj[:, q_lo:q_hi] * LOG2E).astype(BF16)
    proj_ref[:, q_hi:] = proj[:, q_hi:].astype(BF16)
    ones = jnp.ones((VT_ROWS - LANES, x.shape[0]), BF16)
    for h in range(DIFF_HEADS):
        v = proj[:, 2 * DIFF_WIDTH + h * LANES: 2 * DIFF_WIDTH + (h + 1) * LANES]
        vt_ref[h, :LANES, :] = v.T.astype(BF16)
        vt_ref[h, LANES:, :] = ones


def _proj(h0, g, w_in_bf16, tm):
    rows = h0.shape[0]
    return pl.pallas_call(
        _proj_kernel,
        grid=(rows // tm,),
        in_specs=[
            pl.BlockSpec((tm, D_MODEL), lambda i: (i, 0)),
            pl.BlockSpec((1, D_MODEL), lambda i: (0, 0)),
            pl.BlockSpec((D_MODEL, IN_COLS), lambda i: (0, 0)),
        ],
        out_specs=[
            pl.BlockSpec((tm, IN_COLS), lambda i: (i, 0)),
            pl.BlockSpec((DIFF_HEADS, VT_ROWS, tm), lambda i: (0, 0, i)),
        ],
        out_shape=[
            jax.ShapeDtypeStruct((rows, IN_COLS), BF16),
            jax.ShapeDtypeStruct((DIFF_HEADS, VT_ROWS, rows), BF16),
        ],
        compiler_params=_cparams(("parallel",)),
        name="proj",
    )(h0, g, w_in_bf16)


def _diff_kernel(consts_ref, lamv_ref, q_ref, k_ref, vt_ref, bias_ref, g_ref, o_ref,
                 acc_ref, m_ref, excess_ref, *, seq_len, nkv, nsub):
    T = SEQ_TILE
    h = pl.program_id(1)
    i = pl.program_id(2)
    q = q_ref[...]
    lane = lax.broadcasted_iota(jnp.int32, q.shape, 1)
    zero = jnp.zeros_like(q)
    q_maps = (jnp.where(lane < HEAD_DIM, q, zero), jnp.where(lane >= HEAD_DIM, q, zero))

    def chunks(j0, count, *, tile, side, mask, exact):
        maps = range(2)
        c = [0.0, 0.0] if tile else [consts_ref[h, 2 * mp + side] for mp in maps]
        offs = [pl.multiple_of((j0 + t) * T, T) for t in range(count)]
        s = [[lax.dot_general(k_ref[pl.ds(off, T), :], q_maps[mp], _NT, preferred_element_type=F32)
              for mp in maps] for off in offs]
        m_run = [m_ref[mp] for mp in maps]
        lead = list(m_run)
        peak = [None, None]
        pv_sum = [None, None]
        for t, off in enumerate(offs):
            vb = vt_ref[0, :, pl.ds(off, T)]
            if mask:
                kvalid = (off + lax.broadcasted_iota(jnp.int32, (T, 1), 0)) < seq_len
            for mp in maps:
                st = s[t][mp]
                if tile:
                    st = jnp.concatenate(
                        [st[:, u * T:(u + 1) * T] + bias_ref[0, mp, jnp.clip(j0 + t - (i * nsub + u), -2, 2) + 2]
                         for u in range(nsub)], axis=1)
                if mask:
                    st = jnp.where(kvalid, st, NEG)
                top = jnp.max(st, axis=0, keepdims=True) + c[mp]
                if exact:
                    m_new = jnp.maximum(m_run[mp], top)
                    alpha = jnp.exp2(m_run[mp] - m_new)
                    m_run[mp] = m_new
                    p = jnp.exp2(st - (m_new - c[mp])).astype(BF16)
                    acc_ref[mp] = alpha * acc_ref[mp] + jnp.dot(vb, p, preferred_element_type=F32)
                else:
                    p = jnp.exp2(st - (lead[mp] - c[mp])).astype(BF16)
                    pv = jnp.dot(vb, p, preferred_element_type=F32)
                    pv_sum[mp] = pv if t == 0 else pv_sum[mp] + pv
                    peak[mp] = top if t == 0 else jnp.maximum(peak[mp], top)
        for mp in maps:
            if not exact:
                m_run[mp] = jnp.maximum(lead[mp], peak[mp])
                excess_ref[mp] = jnp.maximum(excess_ref[mp], peak[mp] - lead[mp])
                acc_ref[mp] = (acc_ref[mp] + pv_sum[mp]) * jnp.exp2(lead[mp] - m_run[mp])
            m_ref[mp] = m_run[mp]

    def loop(lo, hi, group, **kw):
        ngroups = jnp.maximum(hi - lo, 0) // group

        def grouped(g, carry):
            chunks(lo + g * group, group, **kw)
            return carry

        def single(j, carry):
            chunks(j, 1, **kw)
            return carry

        lax.fori_loop(0, ngroups, grouped, 0)
        if group > 1:
            lax.fori_loop(lo + ngroups * group, hi, single, 0)

    def all_chunks(exact):
        near_lo = jnp.maximum(i * nsub - 1, 0)
        near_hi = jnp.minimum((i + 1) * nsub + 1, nkv)
        loop(0, near_lo, FAR_GROUP, tile=False, side=0, mask=False, exact=exact)
        loop(near_lo, near_hi, NEAR_GROUP if not exact else 1, tile=True, side=0, mask=True, exact=exact)
        loop(near_hi, nkv - 1, FAR_GROUP, tile=False, side=1, mask=False, exact=exact)
        loop(jnp.maximum(near_hi, nkv - 1), nkv, 1, tile=False, side=1, mask=True, exact=exact)

    acc_ref[...] = jnp.zeros_like(acc_ref)
    excess_ref[...] = jnp.zeros_like(excess_ref)
    for mp in range(2):
        s0 = lax.dot_general(k_ref[pl.ds(0, T), :], q_maps[mp], _NT, preferred_element_type=F32)
        m_ref[mp] = jnp.max(s0, axis=0, keepdims=True) + consts_ref[h, 2 * mp]
    all_chunks(exact=False)

    @pl.when(jnp.max(excess_ref[...]) > STALE_MAX_LIMIT)
    def _():
        acc_ref[...] = jnp.zeros_like(acc_ref)
        m_ref[...] = jnp.full_like(m_ref, NEG)
        all_chunks(exact=True)

    lamv = lamv_ref[...]
    lam = (jnp.exp(jnp.sum(lamv[0:1] * lamv[1:2], axis=-1, keepdims=True))
           - jnp.exp(jnp.sum(lamv[2:3] * lamv[3:4], axis=-1, keepdims=True)) + LAM_INIT)
    o = (acc_ref[0, :LANES] / acc_ref[0, LANES:LANES + 1]
         - lam * (acc_ref[1, :LANES] / acc_ref[1, LANES:LANES + 1]))
    ms = jnp.mean(o * o, axis=0, keepdims=True)
    y = o * lax.rsqrt(ms + RMS_EPS) * g_ref[...] * (1.0 - LAM_INIT)
    o_ref[...] = y.T.astype(BF16)


def _diff_attention(proj, vt, consts, lamv, bias_t, subln_g, batch, P, seq_len, nsub):
    T = SEQ_TILE
    tq = nsub * T
    nq = P // tq
    nkv = pl.cdiv(seq_len, T)
    kern = functools.partial(_diff_kernel, seq_len=seq_len, nkv=nkv, nsub=nsub)
    return pl.pallas_call(
        kern,
        grid=(batch, DIFF_HEADS, nq),
        in_specs=[
            pl.BlockSpec(memory_space=pltpu.SMEM),
            pl.BlockSpec((4, HEAD_DIM), lambda b, h, i: (0, 0)),
            pl.BlockSpec((tq, LANES), lambda b, h, i: (b * nq + i, h)),
            pl.BlockSpec((P, LANES), lambda b, h, i: (b, DIFF_HEADS + h)),
            pl.BlockSpec((1, VT_ROWS, P), lambda b, h, i: (h, 0, b)),
            pl.BlockSpec((1, 2, 5, T, T), lambda b, h, i: (h, 0, 0, 0, 0)),
            pl.BlockSpec((LANES, 1), lambda b, h, i: (0, 0)),
        ],
        out_specs=pl.BlockSpec((tq, LANES), lambda b, h, i: (b * nq + i, h)),
        out_shape=jax.ShapeDtypeStruct((batch * P, DIFF_WIDTH), BF16),
        scratch_shapes=[
            pltpu.VMEM((2, VT_ROWS, tq), F32),
            pltpu.VMEM((2, 1, tq), F32),
            pltpu.VMEM((2, 1, tq), F32),
        ],
        compiler_params=_cparams(("parallel", "parallel", "parallel")),
        name="diff_attn",
    )(consts, lamv, proj, proj, vt, bias_t, subln_g)


def _swa_kernel(sink_ref, q_ref, k_ref, v_ref, bias_ref, o_ref, *, seq_len, P):
    T = SEQ_TILE
    i = pl.program_id(1)
    start = pl.multiple_of(jnp.clip(i * T - WINDOW, 0, P - SWA_KEYS), LANES)
    boff = pl.multiple_of(start - i * T + T, LANES)

    def attend(masked):
        kw = k_ref[pl.ds(start, SWA_KEYS), :]
        vw = v_ref[pl.ds(start, SWA_KEYS), :]
        kvalid = (start + lax.broadcasted_iota(jnp.int32, (1, SWA_KEYS), 1)) < seq_len
        lane = lax.broadcasted_iota(jnp.int32, (T, LANES), 1)
        for j in range(SWA_GROUP):
            qp = q_ref[:, j * LANES:(j + 1) * LANES]
            zero = jnp.zeros_like(qp)
            outs = []
            for kvh in range(SWA_KV_HEADS):
                head = kvh * SWA_GROUP + j
                in_half = (lane < HEAD_DIM) if kvh == 0 else (lane >= HEAD_DIM)
                qe = jnp.where(in_half, qp, zero)
                s = lax.dot_general(qe, kw, _NT, preferred_element_type=F32)
                s = s + bias_ref[head, :, pl.ds(boff, SWA_KEYS)]
                if masked:
                    s = jnp.where(kvalid, s, NEG)
                sink = sink_ref[head]
                m = jnp.maximum(jnp.max(s, axis=-1, keepdims=True), sink)
                p = jnp.exp2(s - m)
                l = jnp.sum(p, axis=-1, keepdims=True) + jnp.exp2(sink - m)
                outs.append(jnp.dot(p.astype(BF16), vw, preferred_element_type=F32) / l)
            o_ref[:, j * LANES:(j + 1) * LANES] = jnp.where(lane < HEAD_DIM, outs[0], outs[1]).astype(BF16)

    has_padding = start + SWA_KEYS > seq_len
    pl.when(has_padding)(lambda: attend(True))
    pl.when(jnp.logical_not(has_padding))(lambda: attend(False))


def _swa_attention(proj, sink, bias_w, batch, P, seq_len):
    T = SEQ_TILE
    nq = P // T
    q_blk = (3 * DIFF_WIDTH) // SWA_WIDTH
    k_blk = (3 * DIFF_WIDTH + SWA_WIDTH) // LANES
    kern = functools.partial(_swa_kernel, seq_len=seq_len, P=P)
    return pl.pallas_call(
        kern,
        grid=(batch, nq),
        in_specs=[
            pl.BlockSpec(memory_space=pltpu.SMEM),
            pl.BlockSpec((T, SWA_WIDTH), lambda b, i: (b * nq + i, q_blk)),
            pl.BlockSpec((P, LANES), lambda b, i: (b, k_blk)),
            pl.BlockSpec((P, LANES), lambda b, i: (b, k_blk + 1)),
            pl.BlockSpec((SWA_Q_HEADS, T, SWA_KEYS + T), lambda b, i: (0, 0, 0)),
        ],
        out_specs=pl.BlockSpec((T, SWA_WIDTH), lambda b, i: (b * nq + i, 0)),
        out_shape=jax.ShapeDtypeStruct((batch * P, SWA_WIDTH), BF16),
        compiler_params=_cparams(("parallel", "parallel")),
        name="swa_attn",
    )(sink, proj, proj, proj, bias_w)


def _out_kernel(do_ref, so_ref, h_ref, w_ref, g_ref, wrh_ref, wrl_ref, h2_ref, m_ref, p3_ref, *, seq_len, tm):
    i = pl.program_id(1)
    h2 = (h_ref[...]
          + jnp.dot(do_ref[...], w_ref[:DIFF_WIDTH, :], preferred_element_type=F32)
          + jnp.dot(so_ref[...], w_ref[DIFF_WIDTH:, :], preferred_element_type=F32))
    h2_ref[...] = h2
    ms = jnp.mean(h2 * h2, axis=-1, keepdims=True)
    mf = h2 * lax.rsqrt(ms + RMS_EPS) * g_ref[...]
    mf_hi = mf.astype(BF16)
    m_ref[...] = mf_hi
    mf_lo = (mf - mf_hi.astype(F32)).astype(BF16)
    logits = (jnp.dot(mf_hi, wrh_ref[...], preferred_element_type=F32)
              + jnp.dot(mf_lo, wrh_ref[...], preferred_element_type=F32)
              + jnp.dot(mf_hi, wrl_ref[...], preferred_element_type=F32))
    lane = lax.broadcasted_iota(jnp.int32, logits.shape, 1)
    logits = jnp.where(lane < N_EXPERTS, logits, NEG)
    e = jnp.exp(logits - jnp.max(logits, axis=-1, keepdims=True))
    probs = (e / jnp.sum(e, axis=-1, keepdims=True)).T[:N_EXPERTS]
    pos = i * tm + lax.broadcasted_iota(jnp.int32, (1, tm), 1)
    probs = jnp.where(pos < seq_len, probs, -1.0)
    for c in range(tm // CHUNK):
        p3_ref[c] = probs[:, c * CHUNK:(c + 1) * CHUNK]


def _out_router(diff_o, swa_o, h0, w_out_bf16, g, w_router, batch, P, seq_len):
    tm = SEQ_TILE
    nq = P // tm
    rows = batch * P
    kern = functools.partial(_out_kernel, seq_len=seq_len, tm=tm)
    wr = jnp.pad(w_router.astype(F32), ((0, 0), (0, LANES - N_EXPERTS)))
    wr_hi = wr.astype(BF16)
    wr_lo = (wr - wr_hi.astype(F32)).astype(BF16)
    return pl.pallas_call(
        kern,
        grid=(batch, nq),
        in_specs=[
            pl.BlockSpec((tm, DIFF_WIDTH), lambda b, i: (b * nq + i, 0)),
            pl.BlockSpec((tm, SWA_WIDTH), lambda b, i: (b * nq + i, 0)),
            pl.BlockSpec((tm, D_MODEL), lambda b, i: (b * nq + i, 0)),
            pl.BlockSpec((MIX_WIDTH, D_MODEL), lambda b, i: (0, 0)),
            pl.BlockSpec((1, D_MODEL), lambda b, i: (0, 0)),
            pl.BlockSpec((D_MODEL, LANES), lambda b, i: (0, 0)),
            pl.BlockSpec((D_MODEL, LANES), lambda b, i: (0, 0)),
        ],
        out_specs=[
            pl.BlockSpec((tm, D_MODEL), lambda b, i: (b * nq + i, 0)),
            pl.BlockSpec((tm, D_MODEL), lambda b, i: (b * nq + i, 0)),
            pl.BlockSpec((tm // CHUNK, N_EXPERTS, CHUNK), lambda b, i: (b * nq + i, 0, 0)),
        ],
        out_shape=[
            jax.ShapeDtypeStruct((rows, D_MODEL), F32),
            jax.ShapeDtypeStruct((rows, D_MODEL), BF16),
            jax.ShapeDtypeStruct((rows // CHUNK, N_EXPERTS, CHUNK), F32),
        ],
        compiler_params=_cparams(("parallel", "parallel")),
        name="out_router",
    )(diff_o, swa_o, h0, w_out_bf16, g, wr_hi, wr_lo)


def _topk_kernel(p3_ref, pos_ref, post_ref, gatet_ref, base_ref, incl_ref, flag_ref, *, capacity, nc):
    E = N_EXPERTS
    probs = p3_ref[...]

    def count(pred):
        part = jnp.sum(jnp.where(pred, 1.0, 0.0), axis=0, keepdims=True)
        return jnp.broadcast_to(jnp.sum(part, axis=-1, keepdims=True), part.shape)

    def key_value(key):
        expo = key >> F32_MANT_BITS
        frac = (key & ((1 << F32_MANT_BITS) - 1)).astype(F32) * (2.0 ** -F32_MANT_BITS)
        frac = jnp.where(expo == 0, frac, 1.0 + frac)
        deficit = F32_EXP_BIAS - jnp.maximum(expo, 1)
        scale = jnp.ones_like(frac)
        for b in range(7):
            scale = scale * jnp.where(((deficit >> b) & 1) == 1, 2.0 ** -(1 << b), 1.0)
        return jnp.where(deficit < 0, 2.0, frac * scale)

    def bisect(_, lohi):
        lo, hi = lohi
        mid = lo + ((hi - lo + 1) >> 1)
        ok = count(probs >= key_value(mid)) >= capacity
        return jnp.where(ok, mid, lo), jnp.where(ok, hi, mid - 1)

    lo0 = jnp.zeros((1, E, CHUNK), jnp.int32)
    hi0 = jnp.full((1, E, CHUNK), 0x7F800000, jnp.int32)
    thr_key, _ = lax.fori_loop(0, 32, bisect, (lo0, hi0))
    thr = key_value(thr_key)
    gt = probs > thr
    eq = probs == thr
    need = capacity - count(gt)[0]

    tri = (lax.broadcasted_iota(jnp.int32, (CHUNK, CHUNK), 0)
           <= lax.broadcasted_iota(jnp.int32, (CHUNK, CHUNK), 1)).astype(BF16)

    def inclusive_prefix(flags):
        f2 = flags.astype(BF16).reshape(nc * E, CHUNK)
        return jnp.dot(f2, tri, preferred_element_type=F32).reshape(nc, E, CHUNK)

    eqf = jnp.where(eq, 1.0, 0.0)
    incl_ref[...] = inclusive_prefix(eqf)
    flag_ref[...] = eqf

    def tie_scan(c, run):
        inc = incl_ref[c]
        e_c = flag_ref[c]
        take = (e_c > 0.0) & ((run + inc - e_c) < need)
        flag_ref[c] = jnp.where(take, 1.0, 0.0)
        return run + jnp.broadcast_to(inc[:, CHUNK - 1:CHUNK], inc.shape)

    lax.fori_loop(0, nc, tie_scan, jnp.zeros((E, CHUNK), F32))
    self = jnp.where(gt, 1.0, flag_ref[...])
    flag_ref[...] = self
    incl_ref[...] = inclusive_prefix(self)

    zpad = jnp.zeros((CHUNK - E, CHUNK), F32)

    def to_token_major(x):
        return jnp.concatenate([x, zpad], axis=0).T[:, :E]

    def pos_scan(c, run):
        inc = incl_ref[c]
        s_c = flag_ref[c]
        sel = s_c > 0.0
        pos = jnp.where(sel, run + inc - s_c, -1.0)
        gate = jnp.where(sel, p3_ref[c], 0.0)
        off = pl.multiple_of(c * CHUNK, CHUNK)
        pos_ref[:, pl.ds(off, CHUNK)] = pos
        post_ref[pl.ds(off, CHUNK), :] = to_token_major(pos)
        gatet_ref[pl.ds(off, CHUNK), :] = to_token_major(gate)
        base_ref[c] = run.astype(jnp.int32)
        return run + jnp.broadcast_to(inc[:, CHUNK - 1:CHUNK], inc.shape)

    lax.fori_loop(0, nc, pos_scan, jnp.zeros((E, CHUNK), F32))


def _topk(probs3, capacity):
    nc = probs3.shape[0]
    nt = nc * CHUNK
    kern = functools.partial(_topk_kernel, capacity=capacity, nc=nc)
    return pl.pallas_call(
        kern,
        out_shape=[
            jax.ShapeDtypeStruct((N_EXPERTS, nt), F32),
            jax.ShapeDtypeStruct((nt, N_EXPERTS), F32),
            jax.ShapeDtypeStruct((nt, N_EXPERTS), F32),
            jax.ShapeDtypeStruct((nc, N_EXPERTS, CHUNK), jnp.int32),
        ],
        scratch_shapes=[
            pltpu.VMEM((nc, N_EXPERTS, CHUNK), F32),
            pltpu.VMEM((nc, N_EXPERTS, CHUNK), F32),
        ],
        compiler_params=pltpu.CompilerParams(vmem_limit_bytes=VMEM_LIMIT),
        name="topk",
    )(probs3)


def _gather_kernel(base_ref, cnt_ref, pos_ref, gatet_ref, m_ref, xc_ref, gc_ref, *, nc, sub):
    sb = pl.program_id(1)

    @pl.when(sb == 0)
    def _():
        xc_ref[...] = jnp.zeros_like(xc_ref)
        gc_ref[...] = jnp.zeros_like(gc_ref)

    def expert(ee):
        e = pl.program_id(0) * GATHER_EXPERTS + ee
        first = e * nc + sb * sub

        def copy_rows(u, k, rows):
            erow = lax.broadcasted_iota(jnp.int32, (N_EXPERTS, CHUNK), 0)
            row = lax.broadcasted_iota(jnp.int32, (rows, CHUNK), 0).astype(F32)
            starts = [pl.multiple_of((base_ref[first + u + t] // BF16_ROWS) * BF16_ROWS, BF16_ROWS)
                      for t in range(k)]
            offs = [pl.multiple_of((u + t) * CHUNK, CHUNK) for t in range(k)]
            prow = [jnp.sum(jnp.where(erow == e, pos_ref[:, pl.ds(off, CHUNK)], 0.0), axis=0, keepdims=True)
                    - a.astype(F32) for a, off in zip(starts, offs)]
            onehot = [jnp.where(row == p, 1.0, 0.0).astype(BF16) for p in prow]
            g = [jnp.dot(oh, m_ref[pl.ds(off, CHUNK), :], preferred_element_type=F32)
                 for oh, off in zip(onehot, offs)]
            ecol = lax.broadcasted_iota(jnp.int32, (CHUNK, N_EXPERTS), 1)
            lane = lax.broadcasted_iota(jnp.int32, (CHUNK, LANES), 1)
            gg = []
            for oh, off in zip(onehot, offs):
                gate = jnp.sum(jnp.where(ecol == e, gatet_ref[pl.ds(off, CHUNK), :], 0.0), axis=1, keepdims=True)
                hi = gate.astype(BF16).astype(F32)
                mid = (gate - hi).astype(BF16).astype(F32)
                lo = gate - hi - mid
                terms = jnp.where(lane == 0, hi, jnp.where(lane == 1, mid, jnp.where(lane == 2, lo, 0.0)))
                gg.append(jnp.dot(oh, terms.astype(BF16), preferred_element_type=F32))
            for a, gt, gs in zip(starts, g, gg):
                xc_ref[ee, pl.ds(a, rows), :] += gt.astype(BF16)
                gc_ref[ee, pl.ds(a, rows), :] += gs

        def single(u):
            few = cnt_ref[first + u] <= FAST_CNT
            pl.when(few)(lambda: copy_rows(u, 1, FAST_ROWS))
            pl.when(jnp.logical_not(few))(lambda: copy_rows(u, 1, WIN_ROWS))

        def group(gi, carry):
            u0 = gi * GATHER_GROUP
            most = functools.reduce(jnp.maximum, [cnt_ref[first + u0 + t] for t in range(GATHER_GROUP)])
            few = most <= FAST_CNT
            pl.when(few)(lambda: copy_rows(u0, GATHER_GROUP, FAST_ROWS))
            pl.when(jnp.logical_not(few))(lambda: copy_rows(u0, GATHER_GROUP, WIN_ROWS))
            return carry

        ngroups = sub // GATHER_GROUP
        lax.fori_loop(0, ngroups, group, 0)
        for u in range(ngroups * GATHER_GROUP, sub):
            single(u)

    for ee in range(GATHER_EXPERTS):
        expert(ee)


def _gather(base_flat, cnt_flat, pos, gatet, m, cp):
    nt = pos.shape[1]
    nc = nt // CHUNK
    sub = _largest_divisor(nc, 17)
    ts = sub * CHUNK
    kern = functools.partial(_gather_kernel, nc=nc, sub=sub)
    return pl.pallas_call(
        kern,
        grid_spec=pltpu.PrefetchScalarGridSpec(
            num_scalar_prefetch=2,
            grid=(N_EXPERTS // GATHER_EXPERTS, nc // sub),
            in_specs=[
                pl.BlockSpec((N_EXPERTS, ts), lambda e, s, base, cnt: (0, s)),
                pl.BlockSpec((ts, N_EXPERTS), lambda e, s, base, cnt: (s, 0)),
                pl.BlockSpec((ts, D_MODEL), lambda e, s, base, cnt: (s, 0)),
            ],
            out_specs=[
                pl.BlockSpec((GATHER_EXPERTS, cp, D_MODEL), lambda e, s, base, cnt: (e, 0, 0)),
                pl.BlockSpec((GATHER_EXPERTS, cp, LANES), lambda e, s, base, cnt: (e, 0, 0)),
            ],
        ),
        out_shape=[
            jax.ShapeDtypeStruct((N_EXPERTS, cp, D_MODEL), BF16),
            jax.ShapeDtypeStruct((N_EXPERTS, cp, LANES), F32),
        ],
        compiler_params=_cparams(("parallel", "arbitrary")),
        name="gather",
    )(base_flat, cnt_flat, pos, gatet, m)


def _ffn_kernel(x_ref, gc_ref, wg_ref, wu_ref, wd_ref, y_ref, acc_ref, *, rows):
    f = pl.program_id(1)
    nf = pl.num_programs(1)

    @pl.when(f == 0)
    def _():
        acc_ref[...] = jnp.zeros_like(acc_ref)

    block = -(-rows // (FFN_ROW_BLOCKS * BF16_ROWS)) * BF16_ROWS
    bounds = list(range(0, rows, block)) + [rows]

    def step(width, valid):
        wg = wg_ref[0, :width, :].astype(BF16)
        wu = wu_ref[0, :width, :].astype(BF16)
        wd = wd_ref[0, :width, :]
        if valid < width:
            wd = jnp.where(lax.broadcasted_iota(jnp.int32, (width, 1), 0) < valid, wd, 0.0)
        wd = wd.astype(BF16)
        for r0, r1 in zip(bounds[:-1], bounds[1:]):
            x = x_ref[0, r0:r1, :]
            g = lax.dot_general(x, wg, _NT, preferred_element_type=F32)
            u = lax.dot_general(x, wu, _NT, preferred_element_type=F32)
            hid = g * jax.nn.sigmoid(g) * u
            if valid < width:
                hid = jnp.where(lax.broadcasted_iota(jnp.int32, (1, width), 1) < valid, hid, 0.0)
            acc_ref[r0:r1, :] += jnp.dot(hid.astype(BF16), wd, preferred_element_type=F32)

    last_valid = D_FF - (D_FF // FF_TILE) * FF_TILE
    last_width = -(-last_valid // MXU_TILE) * MXU_TILE
    pl.when(f < nf - 1)(lambda: step(FF_TILE, FF_TILE))
    pl.when(f == nf - 1)(lambda: step(last_width, last_valid))

    @pl.when(f == nf - 1)
    def _():
        gc = gc_ref[0, :rows, :]
        gate = gc[:, 0:1] + gc[:, 1:2] + gc[:, 2:3]
        y_ref[0, :rows, :] = (acc_ref[...] * gate).astype(BF16)
        y_ref[0, rows:, :] = jnp.zeros((y_ref.shape[1] - rows, D_MODEL), BF16)


def _ffn(xc, gc, w_gate_t, w_up_t, w_down, rows):
    cp = xc.shape[1]
    kern = functools.partial(_ffn_kernel, rows=rows)
    once = pl.Buffered(1)
    w_spec = pl.BlockSpec((1, FF_TILE, D_MODEL), lambda e, f: (e, f, 0))
    return pl.pallas_call(
        kern,
        grid=(N_EXPERTS, pl.cdiv(D_FF, FF_TILE)),
        in_specs=[pl.BlockSpec((1, cp, D_MODEL), lambda e, f: (e, 0, 0), pipeline_mode=once),
                  pl.BlockSpec((1, cp, LANES), lambda e, f: (e, 0, 0), pipeline_mode=once), w_spec, w_spec, w_spec],
        out_specs=pl.BlockSpec((1, cp, D_MODEL), lambda e, f: (e, 0, 0), pipeline_mode=once),
        out_shape=jax.ShapeDtypeStruct((N_EXPERTS, cp, D_MODEL), BF16),
        scratch_shapes=[pltpu.VMEM((rows, D_MODEL), F32)],
        compiler_params=_cparams(("parallel", "arbitrary")),
        name="ffn",
    )(xc, gc, w_gate_t, w_up_t, w_down)


def _combine_kernel(base_ref, fast_ref, h2_ref, post_ref, g_ref, expand_ref, yc_ref, o_ref,
                    win_ref, big_ref, acc_ref, carry_ref, sem_ref, big_sem, *, nc, cpb, out_blocks):
    E = N_EXPERTS
    b = pl.program_id(0)
    c = pl.program_id(1)
    step = b * cpb + c
    nsteps = pl.num_programs(0) * cpb
    slot = step % 2

    def window_start(chunk, e):
        return pl.multiple_of((base_ref[e * nc + chunk] // BF16_ROWS) * BF16_ROWS, BF16_ROWS)

    def fast_copy(chunk, e, sl):
        return pltpu.make_async_copy(yc_ref.at[e, pl.ds(window_start(chunk, e), FAST_ROWS), :],
                                     win_ref.at[sl, pl.ds(e * FAST_ROWS, FAST_ROWS), :], sem_ref.at[sl, e])

    @pl.when((step == 0) & (fast_ref[0] == 1))
    def _():
        for e in range(E):
            fast_copy(0, e, 0).start()

    nxt = jnp.minimum(step + 1, nsteps - 1)

    @pl.when((step + 1 < nsteps) & (fast_ref[nxt] == 1))
    def _():
        for e in range(E):
            fast_copy(step + 1, e, 1 - slot).start()

    is_fast = fast_ref[step] == 1

    @pl.when(is_fast)
    def _():
        for e in range(E):
            fast_copy(step, e, slot).wait()
        lane_e = lax.broadcasted_iota(jnp.int32, (1, E), 1)
        a_vec = jnp.zeros((1, E), F32)
        for e in range(E):
            a_vec = jnp.where(lane_e == e, window_start(step, e).astype(F32), a_vec)
        pos = post_ref[...]
        rel = jnp.where(pos >= 0.0, pos - a_vec, -1.0).astype(BF16)
        spread = jnp.dot(rel, expand_ref[...], preferred_element_type=F32)
        slot_row = (lax.broadcasted_iota(jnp.int32, (1, E * FAST_ROWS), 1) % FAST_ROWS).astype(F32)
        onehot = jnp.where(spread == slot_row, 1.0, 0.0).astype(BF16)
        acc_ref[...] = h2_ref[...] + jnp.dot(onehot, win_ref[slot], preferred_element_type=F32)

    @pl.when(jnp.logical_not(is_fast))
    def _():
        acc = h2_ref[...]
        col = lax.broadcasted_iota(jnp.int32, (CHUNK, WIN_ROWS), 1).astype(F32)
        for e in range(E):
            a = window_start(step, e)
            copy = pltpu.make_async_copy(yc_ref.at[e, pl.ds(a, WIN_ROWS), :], big_ref, big_sem)
            copy.start()
            copy.wait()
            onehot = jnp.where(col == post_ref[:, e:e + 1] - a.astype(F32), 1.0, 0.0).astype(BF16)
            acc = acc + jnp.dot(onehot, big_ref[...], preferred_element_type=F32)
        acc_ref[...] = acc

    acc = acc_ref[...]
    ms = jnp.mean(acc * acc, axis=-1, keepdims=True)
    y = acc * lax.rsqrt(ms + RMS_EPS) * g_ref[...]

    @pl.when((c >= 1) & (c <= out_blocks))
    def _():
        o_ref[0, :CHUNK - N_META, :] = carry_ref[N_META:, :]
        o_ref[0, CHUNK - N_META:, :] = y[:N_META, :]

    carry_ref[...] = y


def _combine(base_flat, fast, h2, post, g, yc, batch, P, S):
    nt = h2.shape[0]
    nc = nt // CHUNK
    cpb = P // CHUNK
    out_blocks = S // CHUNK
    stacked = N_EXPERTS * FAST_ROWS
    expand = jnp.asarray(np.arange(stacked)[None, :] // FAST_ROWS == np.arange(N_EXPERTS)[:, None], BF16)
    kern = functools.partial(_combine_kernel, nc=nc, cpb=cpb, out_blocks=out_blocks)
    row_blk = lambda b, c, base, fast: (b * cpb + c, 0)
    return pl.pallas_call(
        kern,
        grid_spec=pltpu.PrefetchScalarGridSpec(
            num_scalar_prefetch=2,
            grid=(batch, cpb),
            in_specs=[
                pl.BlockSpec((CHUNK, D_MODEL), row_blk),
                pl.BlockSpec((CHUNK, N_EXPERTS), row_blk),
                pl.BlockSpec((1, D_MODEL), lambda b, c, base, fast: (0, 0)),
                pl.BlockSpec((N_EXPERTS, stacked), lambda b, c, base, fast: (0, 0)),
                pl.BlockSpec(memory_space=pl.ANY),
            ],
            out_specs=pl.BlockSpec(
                (1, CHUNK, D_MODEL), lambda b, c, base, fast: (b, jnp.clip(c - 1, 0, out_blocks - 1), 0)),
            scratch_shapes=[
                pltpu.VMEM((2, stacked, D_MODEL), BF16),
                pltpu.VMEM((WIN_ROWS, D_MODEL), BF16),
                pltpu.VMEM((CHUNK, D_MODEL), F32),
                pltpu.VMEM((CHUNK, D_MODEL), F32),
                pltpu.SemaphoreType.DMA((2, N_EXPERTS)),
                pltpu.SemaphoreType.DMA,
            ],
        ),
        out_shape=jax.ShapeDtypeStruct((batch, S, D_MODEL), F32),
        compiler_params=_cparams(("arbitrary", "arbitrary")),
        name="combine",
    )(base_flat, fast, h2, post, g, expand, yc)


def _swa_head_perm():
    perm = np.arange(SWA_WIDTH).reshape(SWA_KV_HEADS, SWA_GROUP, HEAD_DIM)
    return perm.transpose(1, 0, 2).reshape(-1)


def _prep_params(rel_bias, w_in, w_out):
    perm = _swa_head_perm()
    scale = np.ones((IN_COLS,), np.float32)
    scale[:DIFF_WIDTH] = HEAD_DIM ** -0.5
    scale[3 * DIFF_WIDTH:3 * DIFF_WIDTH + SWA_WIDTH] = HEAD_DIM ** -0.5
    cols = np.arange(IN_COLS)
    cols[3 * DIFF_WIDTH:3 * DIFF_WIDTH + SWA_WIDTH] = 3 * DIFF_WIDTH + perm
    w_in_p = (w_in * scale)[:, cols].astype(BF16)
    rows = np.arange(MIX_WIDTH)
    rows[DIFF_WIDTH:] = DIFF_WIDTH + perm
    w_out_p = w_out[rows, :].astype(BF16)

    T = SEQ_TILE
    table = rel_bias.astype(F32)

    def lookup(idx, cols):
        onehot = (idx[..., None] == jnp.arange(N_BUCKETS)).astype(F32)
        return jnp.einsum("...b,bc->...c", onehot, table[:, cols], precision=lax.Precision.HIGHEST)

    kk = jnp.arange(T)[:, None]
    qq = jnp.arange(T)[None, :]
    thresholds = _bucket_thresholds(4 * T)
    idx = jnp.stack([_rel_bucket(d * T + kk - qq, thresholds) for d in (-2, -1, 0, 1, 2)])
    diff_tiles = lookup(idx, slice(0, 2 * DIFF_HEADS))
    diff_tiles = diff_tiles.transpose(3, 0, 1, 2).reshape(DIFF_HEADS, 2, 5, T, T) * LOG2E
    half = N_BUCKETS // 2
    diff_consts = jnp.stack([table[half - 1, 0:2 * DIFF_HEADS:2], table[N_BUCKETS - 1, 0:2 * DIFF_HEADS:2],
                             table[half - 1, 1:2 * DIFF_HEADS:2], table[N_BUCKETS - 1, 1:2 * DIFF_HEADS:2]],
                            axis=1) * LOG2E
    rel = jnp.arange(SWA_KEYS + T)[None, :] - T - jnp.arange(T)[:, None]
    swa_tiles = lookup(_rel_bucket(rel, thresholds), slice(2 * DIFF_HEADS, None))
    swa_tiles = jnp.where((jnp.abs(rel) <= WINDOW)[:, :, None], swa_tiles * LOG2E, NEG).transpose(2, 0, 1)
    return w_in_p, w_out_p, diff_tiles, diff_consts, swa_tiles


def _plan_rows(seq_len):
    nsub = min((4, 5, 6), key=lambda n: (-(-seq_len // (n * SEQ_TILE)) * n, -n))
    return nsub, -(-seq_len // (nsub * SEQ_TILE)) * nsub * SEQ_TILE


def _trunk(x, meta_tokens, prep, attn_norm_g, lamv, subln_g, swa_sink, ffn_norm_g, w_router,
           w_gate, w_up, w_down, final_norm_g):
    w_in_p, w_out_p, diff_tiles, diff_consts, swa_tiles = prep
    B, S, _ = x.shape
    L = S + N_META
    nsub, P = _plan_rows(L)
    rows = B * P
    capacity = EC_FACTOR * (B * L) // N_EXPERTS
    ffn_rows = -(-capacity // BF16_ROWS) * BF16_ROWS
    cp = -(-(capacity + WIN_ROWS) // BF16_ROWS) * BF16_ROWS

    h0 = jnp.concatenate([jnp.broadcast_to(meta_tokens.astype(x.dtype)[None], (B, N_META, D_MODEL)), x,
                          jnp.zeros((B, P - L, D_MODEL), x.dtype)], axis=1).reshape(rows, D_MODEL)

    tm = SEQ_TILE * _largest_divisor(rows // SEQ_TILE, 2)
    proj, vt = _proj(h0, attn_norm_g.reshape(1, D_MODEL), w_in_p, tm)
    diff_o = _diff_attention(proj, vt, diff_consts, lamv, diff_tiles, subln_g.reshape(LANES, 1), B, P, L, nsub)
    swa_o = _swa_attention(proj, swa_sink * LOG2E, swa_tiles, B, P, L)
    h2, m, probs3 = _out_router(diff_o, swa_o, h0, w_out_p, ffn_norm_g.reshape(1, D_MODEL), w_router, B, P, L)
    pos, post, gatet, base3 = _topk(probs3, capacity)
    base_ec = base3[:, :, 0].T
    nxt = jnp.concatenate([base_ec[:, 1:], jnp.full((N_EXPERTS, 1), capacity, jnp.int32)], axis=1)
    cnt_ec = nxt - base_ec
    fast = jnp.all(cnt_ec <= FAST_CNT, axis=0).astype(jnp.int32)
    base_flat = base_ec.reshape(-1)
    xc, gc = _gather(base_flat, cnt_ec.reshape(-1), pos, gatet, m, cp)
    yc = _ffn(xc, gc, w_gate, w_up, w_down, ffn_rows)
    return _combine(base_flat, fast, h2, post, final_norm_g.reshape(1, D_MODEL), yc, B, P, S)


def kernel(x_prompt, x_sample, meta_tokens, rel_bias, attn_norm_g, w_in, diff_lambda_q1, diff_lambda_k1,
           diff_lambda_q2, diff_lambda_k2, diff_subln_g, swa_sink, w_out, ffn_norm_g, w_router, w_gate, w_up,
           w_down, final_norm_g):
    prep = _prep_params(rel_bias, w_in[0], w_out[0])
    lamv = jnp.stack([diff_lambda_q1[0], diff_lambda_k1[0], diff_lambda_q2[0], diff_lambda_k2[0]]).astype(F32)
    args = (meta_tokens, prep, attn_norm_g[0], lamv, diff_subln_g[0], swa_sink[0].astype(F32), ffn_norm_g[0],
            w_router[0], jnp.swapaxes(w_gate[0], 1, 2), jnp.swapaxes(w_up[0], 1, 2), w_down[0], final_norm_g)
    return (_trunk(x_prompt, *args), _trunk(x_sample, *args))
```

```python
import functools
import math

import jax
import jax.numpy as jnp
import numpy as np
from jax import lax
from jax.experimental import pallas as pl
from jax.experimental.pallas import tpu as pltpu

F32 = jnp.float32
BF16 = jnp.bfloat16

D_MODEL = 1024
HEAD_DIM = 64
N_META = 16
DIFF_HEADS = 4
DIFF_WIDTH = DIFF_HEADS * 2 * HEAD_DIM
SWA_Q_HEADS = 8
SWA_KV_HEADS = 2
SWA_GROUP = SWA_Q_HEADS // SWA_KV_HEADS
SWA_WIDTH = SWA_Q_HEADS * HEAD_DIM
SWA_KV_WIDTH = SWA_KV_HEADS * HEAD_DIM
MIX_WIDTH = DIFF_WIDTH + SWA_WIDTH
IN_COLS = 3 * DIFF_WIDTH + SWA_WIDTH + 2 * SWA_KV_WIDTH
WINDOW = 128
N_BUCKETS = 32
MAX_DISTANCE = 128
N_EXPERTS = 16
EC_FACTOR = 2
D_FF = 2752
RMS_EPS = 1e-6
LAM_INIT = 0.8 - 0.6 * math.exp(-0.3 * 0)

LANES = 128
SUBLANES = 8
F32_MANT_BITS = 23
F32_EXP_BIAS = 127
BF16_ROWS = 16
VMEM_LIMIT = 56 * 1024 * 1024

SEQ_TILE = 256
SWA_KEYS = SEQ_TILE + 2 * WINDOW
NEAR_GROUP = 2
FAR_GROUP = 4
MXU_TILE = 256
FF_TILE = 3 * MXU_TILE
FFN_ROW_BLOCKS = 2
CHUNK = LANES
VT_ROWS = LANES + BF16_ROWS
LOG2E = math.log2(math.e)
WIN_ROWS = CHUNK + BF16_ROWS
FAST_ROWS = 64
FAST_CNT = FAST_ROWS - BF16_ROWS
GATHER_GROUP = 4
GATHER_EXPERTS = 2
STALE_MAX_LIMIT = 64.0
NEG = -1e30

_NT = (((1,), (1,)), ((), ()))


def _cparams(sem):
    return pltpu.CompilerParams(dimension_semantics=sem, vmem_limit_bytes=VMEM_LIMIT)


def _largest_divisor(n, cap):
    return max(d for d in range(1, cap + 1) if n % d == 0)


def _bucket_thresholds(max_rel):
    half = N_BUCKETS // 2
    max_exact = half // 2
    n = jnp.arange(max_rel)
    large = max_exact + (jnp.log(jnp.maximum(n, 1).astype(F32) / max_exact)
                         / math.log(MAX_DISTANCE / max_exact) * (half - max_exact)).astype(jnp.int32)
    mag = jnp.where(n < max_exact, n, jnp.minimum(large, half - 1))
    mag = jnp.arange(half)[mag]
    return jnp.sum(mag[None, :] < jnp.arange(1, half)[:, None], axis=1)


def _rel_bucket(rel, thresholds):
    mag = jnp.sum(jnp.abs(rel)[..., None] >= thresholds, axis=-1)
    return jnp.where(rel > 0, N_BUCKETS // 2, 0) + mag


def _proj_kernel(h_ref, g_ref, w_ref, proj_ref, vt_ref):
    x = h_ref[...]
    ms = jnp.mean(x * x, axis=-1, keepdims=True)
    a = (x * lax.rsqrt(ms + RMS_EPS) * g_ref[...]).astype(BF16)
    proj = jnp.dot(a, w_ref[...], preferred_element_type=F32)
    q_lo, q_hi = 3 * DIFF_WIDTH, 3 * DIFF_WIDTH + SWA_WIDTH
    proj_ref[:, :DIFF_WIDTH] = (proj[:, :DIFF_WIDTH] * LOG2E).astype(BF16)
    proj_ref[:, DIFF_WIDTH:q_lo] = proj[:, DIFF_WIDTH:q_lo].astype(BF16)
    proj_ref[:, q_lo:q_hi] = (proj[:, q_lo:q_hi] * LOG2E).astype(BF16)
    proj_ref[:, q_hi:] = proj[:, q_hi:].astype(BF16)
    ones = jnp.ones((VT_ROWS - LANES, x.shape[0]), BF16)
    for h in range(DIFF_HEADS):
        v = proj[:, 2 * DIFF_WIDTH + h * LANES: 2 * DIFF_WIDTH + (h + 1) * LANES]
        vt_ref[h, :LANES, :] = v.T.astype(BF16)
        vt_ref[h, LANES:, :] = ones


def _proj(h0, g, w_in_bf16, tm):
    rows = h0.shape[0]
    return pl.pallas_call(
        _proj_kernel,
        grid=(rows // tm,),
        in_specs=[
            pl.BlockSpec((tm, D_MODEL), lambda i: (i, 0)),
            pl.BlockSpec((1, D_MODEL), lambda i: (0, 0)),
            pl.BlockSpec((D_MODEL, IN_COLS), lambda i: (0, 0)),
        ],
        out_specs=[
            pl.BlockSpec((tm, IN_COLS), lambda i: (i, 0)),
            pl.BlockSpec((DIFF_HEADS, VT_ROWS, tm), lambda i: (0, 0, i)),
        ],
        out_shape=[
            jax.ShapeDtypeStruct((rows, IN_COLS), BF16),
            jax.ShapeDtypeStruct((DIFF_HEADS, VT_ROWS, rows), BF16),
        ],
        compiler_params=_cparams(("parallel",)),
        name="proj",
    )(h0, g, w_in_bf16)


def _diff_kernel(consts_ref, lamv_ref, q_ref, k_ref, vt_ref, bias_ref, g_ref, o_ref,
                 acc_ref, m_ref, excess_ref, *, seq_len, nkv, nsub):
    T = SEQ_TILE
    h = pl.program_id(1)
    i = pl.program_id(2)
    q = q_ref[...]
    lane = lax.broadcasted_iota(jnp.int32, q.shape, 1)
    zero = jnp.zeros_like(q)
    q_maps = (jnp.where(lane < HEAD_DIM, q, zero), jnp.where(lane >= HEAD_DIM, q, zero))

    def chunks(j0, count, *, tile, side, mask, exact):
        maps = range(2)
        c = [0.0, 0.0] if tile else [consts_ref[h, 2 * mp + side] for mp in maps]
        offs = [pl.multiple_of((j0 + t) * T, T) for t in range(count)]
        s = [[lax.dot_general(k_ref[pl.ds(off, T), :], q_maps[mp], _NT, preferred_element_type=F32)
              for mp in maps] for off in offs]
        m_run = [m_ref[mp] for mp in maps]
        lead = list(m_run)
        peak = [None, None]
        pv_sum = [None, None]
        for t, off in enumerate(offs):
            vb = vt_ref[0, :, pl.ds(off, T)]
            if mask:
                kvalid = (off + lax.broadcasted_iota(jnp.int32, (T, 1), 0)) < seq_len
            for mp in maps:
                st = s[t][mp]
                if tile:
                    st = jnp.concatenate(
                        [st[:, u * T:(u + 1) * T] + bias_ref[0, mp, jnp.clip(j0 + t - (i * nsub + u), -2, 2) + 2]
                         for u in range(nsub)], axis=1)
                if mask:
                    st = jnp.where(kvalid, st, NEG)
                top = jnp.max(st, axis=0, keepdims=True) + c[mp]
                if exact:
                    m_new = jnp.maximum(m_run[mp], top)
                    alpha = jnp.exp2(m_run[mp] - m_new)
                    m_run[mp] = m_new
                    p = jnp.exp2(st - (m_new - c[mp])).astype(BF16)
                    acc_ref[mp] = alpha * acc_ref[mp] + jnp.dot(vb, p, preferred_element_type=F32)
                else:
                    p = jnp.exp2(st - (lead[mp] - c[mp])).astype(BF16)
                    pv = jnp.dot(vb, p, preferred_element_type=F32)
                    pv_sum[mp] = pv if t == 0 else pv_sum[mp] + pv
                    peak[mp] = top if t == 0 else jnp.maximum(peak[mp], top)
        for mp in maps:
            if not exact:
                m_run[mp] = jnp.maximum(lead[mp], peak[mp])
                excess_ref[mp] = jnp.maximum(excess_ref[mp], peak[mp] - lead[mp])
                acc_ref[mp] = (acc_ref[mp] + pv_sum[mp]) * jnp.exp2(lead[mp] - m_run[mp])
            m_ref[mp] = m_run[mp]

    def loop(lo, hi, group, **kw):
        ngroups = jnp.maximum(hi - lo, 0) // group

        def grouped(g, carry):
            chunks(lo + g * group, group, **kw)
            return carry

        def single(j, carry):
            chunks(j, 1, **kw)
            return carry

        lax.fori_loop(0, ngroups, grouped, 0)
        if group > 1:
            lax.fori_loop(lo + ngroups * group, hi, single, 0)

    def all_chunks(exact):
        near_lo = jnp.maximum(i * nsub - 1, 0)
        near_hi = jnp.minimum((i + 1) * nsub + 1, nkv)
        loop(0, near_lo, FAR_GROUP, tile=False, side=0, mask=False, exact=exact)
        loop(near_lo, near_hi, NEAR_GROUP if not exact else 1, tile=True, side=0, mask=True, exact=exact)
        loop(near_hi, nkv - 1, FAR_GROUP, tile=False, side=1, mask=False, exact=exact)
        loop(jnp.maximum(near_hi, nkv - 1), nkv, 1, tile=False, side=1, mask=True, exact=exact)

    acc_ref[...] = jnp.zeros_like(acc_ref)
    excess_ref[...] = jnp.zeros_like(excess_ref)
    for mp in range(2):
        s0 = lax.dot_general(k_ref[pl.ds(0, T), :], q_maps[mp], _NT, preferred_element_type=F32)
        m_ref[mp] = jnp.max(s0, axis=0, keepdims=True) + consts_ref[h, 2 * mp]
    all_chunks(exact=False)

    @pl.when(jnp.max(excess_ref[...]) > STALE_MAX_LIMIT)
    def _():
        acc_ref[...] = jnp.zeros_like(acc_ref)
        m_ref[...] = jnp.full_like(m_ref, NEG)
        all_chunks(exact=True)

    lamv = lamv_ref[...]
    lam = (jnp.exp(jnp.sum(lamv[0:1] * lamv[1:2], axis=-1, keepdims=True))
           - jnp.exp(jnp.sum(lamv[2:3] * lamv[3:4], axis=-1, keepdims=True)) + LAM_INIT)
    o = (acc_ref[0, :LANES] / acc_ref[0, LANES:LANES + 1]
         - lam * (acc_ref[1, :LANES] / acc_ref[1, LANES:LANES + 1]))
    ms = jnp.mean(o * o, axis=0, keepdims=True)
    y = o * lax.rsqrt(ms + RMS_EPS) * g_ref[...] * (1.0 - LAM_INIT)
    o_ref[...] = y.T.astype(BF16)


def _diff_attention(proj, vt, consts, lamv, bias_t, subln_g, batch, P, seq_len, nsub):
    T = SEQ_TILE
    tq = nsub * T
    nq = P // tq
    nkv = pl.cdiv(seq_len, T)
    kern = functools.partial(_diff_kernel, seq_len=seq_len, nkv=nkv, nsub=nsub)
    return pl.pallas_call(
        kern,
        grid=(batch, DIFF_HEADS, nq),
        in_specs=[
            pl.BlockSpec(memory_space=pltpu.SMEM),
            pl.BlockSpec((4, HEAD_DIM), lambda b, h, i: (0, 0)),
            pl.BlockSpec((tq, LANES), lambda b, h, i: (b * nq + i, h)),
            pl.BlockSpec((P, LANES), lambda b, h, i: (b, DIFF_HEADS + h)),
            pl.BlockSpec((1, VT_ROWS, P), lambda b, h, i: (h, 0, b)),
            pl.BlockSpec((1, 2, 5, T, T), lambda b, h, i: (h, 0, 0, 0, 0)),
            pl.BlockSpec((LANES, 1), lambda b, h, i: (0, 0)),
        ],
        out_specs=pl.BlockSpec((tq, LANES), lambda b, h, i: (b * nq + i, h)),
        out_shape=jax.ShapeDtypeStruct((batch * P, DIFF_WIDTH), BF16),
        scratch_shapes=[
            pltpu.VMEM((2, VT_ROWS, tq), F32),
            pltpu.VMEM((2, 1, tq), F32),
            pltpu.VMEM((2, 1, tq), F32),
        ],
        compiler_params=_cparams(("parallel", "parallel", "parallel")),
        name="diff_attn",
    )(consts, lamv, proj, proj, vt, bias_t, subln_g)


def _swa_kernel(sink_ref, q_ref, k_ref, v_ref, bias_ref, o_ref, *, seq_len, P):
    T = SEQ_TILE
    i = pl.program_id(1)
    start = pl.multiple_of(jnp.clip(i * T - WINDOW, 0, P - SWA_KEYS), LANES)
    boff = pl.multiple_of(start - i * T + T, LANES)

    def attend(masked):
        kw = k_ref[pl.ds(start, SWA_KEYS), :]
        vw = v_ref[pl.ds(start, SWA_KEYS), :]
        kvalid = (start + lax.broadcasted_iota(jnp.int32, (1, SWA_KEYS), 1)) < seq_len
        lane = lax.broadcasted_iota(jnp.int32, (T, LANES), 1)
        for j in range(SWA_GROUP):
            qp = q_ref[:, j * LANES:(j + 1) * LANES]
            zero = jnp.zeros_like(qp)
            outs = []
            for kvh in range(SWA_KV_HEADS):
                head = kvh * SWA_GROUP + j
                in_half = (lane < HEAD_DIM) if kvh == 0 else (lane >= HEAD_DIM)
                qe = jnp.where(in_half, qp, zero)
                s = lax.dot_general(qe, kw, _NT, preferred_element_type=F32)
                s = s + bias_ref[head, :, pl.ds(boff, SWA_KEYS)]
                if masked:
                    s = jnp.where(kvalid, s, NEG)
                sink = sink_ref[head]
                m = jnp.maximum(jnp.max(s, axis=-1, keepdims=True), sink)
                p = jnp.exp2(s - m)
                l = jnp.sum(p, axis=-1, keepdims=True) + jnp.exp2(sink - m)
                outs.append(jnp.dot(p.astype(BF16), vw, preferred_element_type=F32) / l)
            o_ref[:, j * LANES:(j + 1) * LANES] = jnp.where(lane < HEAD_DIM, outs[0], outs[1]).astype(BF16)

    has_padding = start + SWA_KEYS > seq_len
    pl.when(has_padding)(lambda: attend(True))
    pl.when(jnp.logical_not(has_padding))(lambda: attend(False))


def _swa_attention(proj, sink, bias_w, batch, P, seq_len):
    T = SEQ_TILE
    nq = P // T
    q_blk = (3 * DIFF_WIDTH) // SWA_WIDTH
    k_blk = (3 * DIFF_WIDTH + SWA_WIDTH) // LANES
    kern = functools.partial(_swa_kernel, seq_len=seq_len, P=P)
    return pl.pallas_call(
        kern,
        grid=(batch, nq),
        in_specs=[
            pl.BlockSpec(memory_space=pltpu.SMEM),
            pl.BlockSpec((T, SWA_WIDTH), lambda b, i: (b * nq + i, q_blk)),
            pl.BlockSpec((P, LANES), lambda b, i: (b, k_blk)),
            pl.BlockSpec((P, LANES), lambda b, i: (b, k_blk + 1)),
            pl.BlockSpec((SWA_Q_HEADS, T, SWA_KEYS + T), lambda b, i: (0, 0, 0)),
        ],
        out_specs=pl.BlockSpec((T, SWA_WIDTH), lambda b, i: (b * nq + i, 0)),
        out_shape=jax.ShapeDtypeStruct((batch * P, SWA_WIDTH), BF16),
        compiler_params=_cparams(("parallel", "parallel")),
        name="swa_attn",
    )(sink, proj, proj, proj, bias_w)


def _out_kernel(do_ref, so_ref, h_ref, w_ref, g_ref, wrh_ref, wrl_ref, h2_ref, m_ref, p3_ref, *, seq_len, tm):
    i = pl.program_id(1)
    h2 = (h_ref[...]
          + jnp.dot(do_ref[...], w_ref[:DIFF_WIDTH, :], preferred_element_type=F32)
          + jnp.dot(so_ref[...], w_ref[DIFF_WIDTH:, :], preferred_element_type=F32))
    h2_ref[...] = h2
    ms = jnp.mean(h2 * h2, axis=-1, keepdims=True)
    mf = h2 * lax.rsqrt(ms + RMS_EPS) * g_ref[...]
    mf_hi = mf.astype(BF16)
    m_ref[...] = mf_hi
    mf_lo = (mf - mf_hi.astype(F32)).astype(BF16)
    logits = (jnp.dot(mf_hi, wrh_ref[...], preferred_element_type=F32)
              + jnp.dot(mf_lo, wrh_ref[...], preferred_element_type=F32)
              + jnp.dot(mf_hi, wrl_ref[...], preferred_element_type=F32))
    lane = lax.broadcasted_iota(jnp.int32, logits.shape, 1)
    logits = jnp.where(lane < N_EXPERTS, logits, NEG)
    e = jnp.exp(logits - jnp.max(logits, axis=-1, keepdims=True))
    probs = (e / jnp.sum(e, axis=-1, keepdims=True)).T[:N_EXPERTS]
    pos = i * tm + lax.broadcasted_iota(jnp.int32, (1, tm), 1)
    probs = jnp.where(pos < seq_len, probs, -1.0)
    for c in range(tm // CHUNK):
        p3_ref[c] = probs[:, c * CHUNK:(c + 1) * CHUNK]


def _out_router(diff_o, swa_o, h0, w_out_bf16, g, w_router, batch, P, seq_len):
    tm = SEQ_TILE
    nq = P // tm
    rows = batch * P
    kern = functools.partial(_out_kernel, seq_len=seq_len, tm=tm)
    wr = jnp.pad(w_router.astype(F32), ((0, 0), (0, LANES - N_EXPERTS)))
    wr_hi = wr.astype(BF16)
    wr_lo = (wr - wr_hi.astype(F32)).astype(BF16)
    return pl.pallas_call(
        kern,
        grid=(batch, nq),
        in_specs=[
            pl.BlockSpec((tm, DIFF_WIDTH), lambda b, i: (b * nq + i, 0)),
            pl.BlockSpec((tm, SWA_WIDTH), lambda b, i: (b * nq + i, 0)),
            pl.BlockSpec((tm, D_MODEL), lambda b, i: (b * nq + i, 0)),
            pl.BlockSpec((MIX_WIDTH, D_MODEL), lambda b, i: (0, 0)),
            pl.BlockSpec((1, D_MODEL), lambda b, i: (0, 0)),
            pl.BlockSpec((D_MODEL, LANES), lambda b, i: (0, 0)),
            pl.BlockSpec((D_MODEL, LANES), lambda b, i: (0, 0)),
        ],
        out_specs=[
            pl.BlockSpec((tm, D_MODEL), lambda b, i: (b * nq + i, 0)),
            pl.BlockSpec((tm, D_MODEL), lambda b, i: (b * nq + i, 0)),
            pl.BlockSpec((tm // CHUNK, N_EXPERTS, CHUNK), lambda b, i: (b * nq + i, 0, 0)),
        ],
        out_shape=[
            jax.ShapeDtypeStruct((rows, D_MODEL), F32),
            jax.ShapeDtypeStruct((rows, D_MODEL), BF16),
            jax.ShapeDtypeStruct((rows // CHUNK, N_EXPERTS, CHUNK), F32),
        ],
        compiler_params=_cparams(("parallel", "parallel")),
        name="out_router",
    )(diff_o, swa_o, h0, w_out_bf16, g, wr_hi, wr_lo)


def _topk_kernel(p3_ref, pos_ref, post_ref, gate_ref, base_ref, incl_ref, flag_ref, *, capacity, nc):
    E = N_EXPERTS
    probs = p3_ref[...]

    def count(pred):
        part = jnp.sum(jnp.where(pred, 1.0, 0.0), axis=0, keepdims=True)
        return jnp.broadcast_to(jnp.sum(part, axis=-1, keepdims=True), part.shape)

    def key_value(key):
        expo = key >> F32_MANT_BITS
        frac = (key & ((1 << F32_MANT_BITS) - 1)).astype(F32) * (2.0 ** -F32_MANT_BITS)
        frac = jnp.where(expo == 0, frac, 1.0 + frac)
        deficit = F32_EXP_BIAS - jnp.maximum(expo, 1)
        scale = jnp.ones_like(frac)
        for b in range(7):
            scale = scale * jnp.where(((deficit >> b) & 1) == 1, 2.0 ** -(1 << b), 1.0)
        return jnp.where(deficit < 0, 2.0, frac * scale)

    def bisect(_, lohi):
        lo, hi = lohi
        mid = lo + ((hi - lo + 1) >> 1)
        ok = count(probs >= key_value(mid)) >= capacity
        return jnp.where(ok, mid, lo), jnp.where(ok, hi, mid - 1)

    lo0 = jnp.zeros((1, E, CHUNK), jnp.int32)
    hi0 = jnp.full((1, E, CHUNK), 0x7F800000, jnp.int32)
    thr_key, _ = lax.fori_loop(0, 32, bisect, (lo0, hi0))
    thr = key_value(thr_key)
    gt = probs > thr
    eq = probs == thr
    need = capacity - count(gt)[0]

    tri = (lax.broadcasted_iota(jnp.int32, (CHUNK, CHUNK), 0)
           <= lax.broadcasted_iota(jnp.int32, (CHUNK, CHUNK), 1)).astype(BF16)

    def inclusive_prefix(flags):
        f2 = flags.astype(BF16).reshape(nc * E, CHUNK)
        return jnp.dot(f2, tri, preferred_element_type=F32).reshape(nc, E, CHUNK)

    eqf = jnp.where(eq, 1.0, 0.0)
    incl_ref[...] = inclusive_prefix(eqf)
    flag_ref[...] = eqf

    def tie_scan(c, run):
        inc = incl_ref[c]
        e_c = flag_ref[c]
        take = (e_c > 0.0) & ((run + inc - e_c) < need)
        flag_ref[c] = jnp.where(take, 1.0, 0.0)
        return run + jnp.broadcast_to(inc[:, CHUNK - 1:CHUNK], inc.shape)

    lax.fori_loop(0, nc, tie_scan, jnp.zeros((E, CHUNK), F32))
    self = jnp.where(gt, 1.0, flag_ref[...])
    flag_ref[...] = self
    incl_ref[...] = inclusive_prefix(self)

    zpad = jnp.zeros((CHUNK - E, CHUNK), F32)

    def to_token_major(x):
        return jnp.concatenate([x, zpad], axis=0).T[:, :E]

    def pos_scan(c, run):
        inc = incl_ref[c]
        s_c = flag_ref[c]
        sel = s_c > 0.0
        pos = jnp.where(sel, run + inc - s_c, -1.0)
        gate = jnp.where(sel, p3_ref[c], 0.0)
        off = pl.multiple_of(c * CHUNK, CHUNK)
        pos_ref[:, pl.ds(off, CHUNK)] = pos
        post_ref[pl.ds(off, CHUNK), :] = to_token_major(pos)
        gate_ref[:, pl.ds(off, CHUNK)] = gate
        base_ref[c] = run.astype(jnp.int32)
        return run + jnp.broadcast_to(inc[:, CHUNK - 1:CHUNK], inc.shape)

    lax.fori_loop(0, nc, pos_scan, jnp.zeros((E, CHUNK), F32))


def _topk(probs3, capacity):
    nc = probs3.shape[0]
    nt = nc * CHUNK
    kern = functools.partial(_topk_kernel, capacity=capacity, nc=nc)
    return pl.pallas_call(
        kern,
        out_shape=[
            jax.ShapeDtypeStruct((N_EXPERTS, nt), F32),
            jax.ShapeDtypeStruct((nt, N_EXPERTS), F32),
            jax.ShapeDtypeStruct((N_EXPERTS, nt), F32),
            jax.ShapeDtypeStruct((nc, N_EXPERTS, CHUNK), jnp.int32),
        ],
        scratch_shapes=[
            pltpu.VMEM((nc, N_EXPERTS, CHUNK), F32),
            pltpu.VMEM((nc, N_EXPERTS, CHUNK), F32),
        ],
        compiler_params=pltpu.CompilerParams(vmem_limit_bytes=VMEM_LIMIT),
        name="topk",
    )(probs3)


def _gather_kernel(base_ref, cnt_ref, pos_ref, gate_ref, m_ref, xc_ref, gc_ref, *, nc, sub):
    sb = pl.program_id(1)

    @pl.when(sb == 0)
    def _():
        xc_ref[...] = jnp.zeros_like(xc_ref)
        gc_ref[...] = jnp.zeros_like(gc_ref)

    def expert(ee):
        e = pl.program_id(0) * GATHER_EXPERTS + ee
        first = e * nc + sb * sub

        def copy_rows(u, k, rows):
            erow = lax.broadcasted_iota(jnp.int32, (N_EXPERTS, CHUNK), 0)
            row = lax.broadcasted_iota(jnp.int32, (rows, CHUNK), 0).astype(F32)
            starts = [pl.multiple_of((base_ref[first + u + t] // BF16_ROWS) * BF16_ROWS, BF16_ROWS)
                      for t in range(k)]
            offs = [pl.multiple_of((u + t) * CHUNK, CHUNK) for t in range(k)]
            prow = [jnp.sum(jnp.where(erow == e, pos_ref[:, pl.ds(off, CHUNK)], 0.0), axis=0, keepdims=True)
                    - a.astype(F32) for a, off in zip(starts, offs)]
            hit = [row == p for p in prow]
            onehot = [jnp.where(ht, 1.0, 0.0).astype(BF16) for ht in hit]
            g = [jnp.dot(oh, m_ref[pl.ds(off, CHUNK), :], preferred_element_type=F32)
                 for oh, off in zip(onehot, offs)]
            ones = jnp.ones((CHUNK, LANES), BF16)
            gg = []
            for ht, off in zip(hit, offs):
                gate = jnp.sum(jnp.where(erow == e, gate_ref[:, pl.ds(off, CHUNK)], 0.0), axis=0, keepdims=True)
                hi = gate.astype(BF16).astype(F32)
                lo = gate - hi
                lhs = jnp.concatenate([jnp.where(ht, hi, 0.0), jnp.where(ht, lo, 0.0)], axis=0).astype(BF16)
                both = jnp.dot(lhs, ones, preferred_element_type=F32)
                gg.append(both[:rows] + both[rows:])
            for a, gt, gs in zip(starts, g, gg):
                xc_ref[ee, pl.ds(a, rows), :] += gt.astype(BF16)
                gc_ref[ee, pl.ds(a, rows), :] += gs

        def single(u):
            few = cnt_ref[first + u] <= FAST_CNT
            pl.when(few)(lambda: copy_rows(u, 1, FAST_ROWS))
            pl.when(jnp.logical_not(few))(lambda: copy_rows(u, 1, WIN_ROWS))

        def group(gi, carry):
            u0 = gi * GATHER_GROUP
            most = functools.reduce(jnp.maximum, [cnt_ref[first + u0 + t] for t in range(GATHER_GROUP)])
            few = most <= FAST_CNT
            pl.when(few)(lambda: copy_rows(u0, GATHER_GROUP, FAST_ROWS))
            pl.when(jnp.logical_not(few))(lambda: copy_rows(u0, GATHER_GROUP, WIN_ROWS))
            return carry

        ngroups = sub // GATHER_GROUP
        lax.fori_loop(0, ngroups, group, 0)
        for u in range(ngroups * GATHER_GROUP, sub):
            single(u)

    for ee in range(GATHER_EXPERTS):
        expert(ee)


def _gather(base_flat, cnt_flat, pos, gate, m, cp):
    nt = pos.shape[1]
    nc = nt // CHUNK
    sub = _largest_divisor(nc, 17)
    ts = sub * CHUNK
    kern = functools.partial(_gather_kernel, nc=nc, sub=sub)
    return pl.pallas_call(
        kern,
        grid_spec=pltpu.PrefetchScalarGridSpec(
            num_scalar_prefetch=2,
            grid=(N_EXPERTS // GATHER_EXPERTS, nc // sub),
            in_specs=[
                pl.BlockSpec((N_EXPERTS, ts), lambda e, s, base, cnt: (0, s)),
                pl.BlockSpec((N_EXPERTS, ts), lambda e, s, base, cnt: (0, s)),
                pl.BlockSpec((ts, D_MODEL), lambda e, s, base, cnt: (s, 0)),
            ],
            out_specs=[
                pl.BlockSpec((GATHER_EXPERTS, cp, D_MODEL), lambda e, s, base, cnt: (e, 0, 0)),
                pl.BlockSpec((GATHER_EXPERTS, cp, LANES), lambda e, s, base, cnt: (e, 0, 0)),
            ],
        ),
        out_shape=[
            jax.ShapeDtypeStruct((N_EXPERTS, cp, D_MODEL), BF16),
            jax.ShapeDtypeStruct((N_EXPERTS, cp, LANES), F32),
        ],
        compiler_params=_cparams(("parallel", "arbitrary")),
        name="gather",
    )(base_flat, cnt_flat, pos, gate, m)


def _ffn_kernel(x_ref, gc_ref, wg_ref, wu_ref, wd_ref, y_ref, acc_ref, *, rows):
    f = pl.program_id(1)
    nf = pl.num_programs(1)

    @pl.when(f == 0)
    def _():
        acc_ref[...] = jnp.zeros_like(acc_ref)

    block = -(-rows // (FFN_ROW_BLOCKS * BF16_ROWS)) * BF16_ROWS
    bounds = list(range(0, rows, block)) + [rows]

    def step(width, valid):
        wg = wg_ref[0, :width, :].astype(BF16)
        wu = wu_ref[0, :width, :].astype(BF16)
        wd = wd_ref[0, :width, :]
        if valid < width:
            wd = jnp.where(lax.broadcasted_iota(jnp.int32, (width, 1), 0) < valid, wd, 0.0)
        wd = wd.astype(BF16)
        for r0, r1 in zip(bounds[:-1], bounds[1:]):
            x = x_ref[0, r0:r1, :]
            g = lax.dot_general(x, wg, _NT, preferred_element_type=F32)
            u = lax.dot_general(x, wu, _NT, preferred_element_type=F32)
            hid = g * jax.nn.sigmoid(g) * u
            if valid < width:
                hid = jnp.where(lax.broadcasted_iota(jnp.int32, (1, width), 1) < valid, hid, 0.0)
            acc_ref[r0:r1, :] += jnp.dot(hid.astype(BF16), wd, preferred_element_type=F32)

    last_valid = D_FF - (D_FF // FF_TILE) * FF_TILE
    last_width = -(-last_valid // MXU_TILE) * MXU_TILE
    pl.when(f < nf - 1)(lambda: step(FF_TILE, FF_TILE))
    pl.when(f == nf - 1)(lambda: step(last_width, last_valid))

    @pl.when(f == nf - 1)
    def _():
        gate = gc_ref[0, :rows, 0:1]
        y_ref[0, :rows, :] = (acc_ref[...] * gate).astype(BF16)
        y_ref[0, rows:, :] = jnp.zeros((y_ref.shape[1] - rows, D_MODEL), BF16)


def _ffn(xc, gc, w_gate_t, w_up_t, w_down, rows):
    cp = xc.shape[1]
    kern = functools.partial(_ffn_kernel, rows=rows)
    once = pl.Buffered(1)
    w_spec = pl.BlockSpec((1, FF_TILE, D_MODEL), lambda e, f: (e, f, 0))
    return pl.pallas_call(
        kern,
        grid=(N_EXPERTS, pl.cdiv(D_FF, FF_TILE)),
        in_specs=[pl.BlockSpec((1, cp, D_MODEL), lambda e, f: (e, 0, 0), pipeline_mode=once),
                  pl.BlockSpec((1, cp, LANES), lambda e, f: (e, 0, 0), pipeline_mode=once), w_spec, w_spec, w_spec],
        out_specs=pl.BlockSpec((1, cp, D_MODEL), lambda e, f: (e, 0, 0), pipeline_mode=once),
        out_shape=jax.ShapeDtypeStruct((N_EXPERTS, cp, D_MODEL), BF16),
        scratch_shapes=[pltpu.VMEM((rows, D_MODEL), F32)],
        compiler_params=_cparams(("parallel", "arbitrary")),
        name="ffn",
    )(xc, gc, w_gate_t, w_up_t, w_down)


def _combine_kernel(base_ref, fast_ref, h2_ref, post_ref, g_ref, expand_ref, yc_ref, o_ref,
                    win_ref, big_ref, acc_ref, carry_ref, sem_ref, big_sem, *, nc, cpb, out_blocks):
    E = N_EXPERTS
    b = pl.program_id(0)
    c = pl.program_id(1)
    step = b * cpb + c
    nsteps = pl.num_programs(0) * cpb
    slot = step % 2

    def window_start(chunk, e):
        return pl.multiple_of((base_ref[e * nc + chunk] // BF16_ROWS) * BF16_ROWS, BF16_ROWS)

    def fast_copy(chunk, e, sl):
        return pltpu.make_async_copy(yc_ref.at[e, pl.ds(window_start(chunk, e), FAST_ROWS), :],
                                     win_ref.at[sl, pl.ds(e * FAST_ROWS, FAST_ROWS), :], sem_ref.at[sl, e])

    @pl.when((step == 0) & (fast_ref[0] == 1))
    def _():
        for e in range(E):
            fast_copy(0, e, 0).start()

    nxt = jnp.minimum(step + 1, nsteps - 1)

    @pl.when((step + 1 < nsteps) & (fast_ref[nxt] == 1))
    def _():
        for e in range(E):
            fast_copy(step + 1, e, 1 - slot).start()

    is_fast = fast_ref[step] == 1

    @pl.when(is_fast)
    def _():
        for e in range(E):
            fast_copy(step, e, slot).wait()
        lane_e = lax.broadcasted_iota(jnp.int32, (1, E), 1)
        a_vec = jnp.zeros((1, E), F32)
        for e in range(E):
            a_vec = jnp.where(lane_e == e, window_start(step, e).astype(F32), a_vec)
        pos = post_ref[...]
        rel = jnp.where(pos >= 0.0, pos - a_vec, -1.0).astype(BF16)
        spread = jnp.dot(rel, expand_ref[...], preferred_element_type=F32)
        slot_row = (lax.broadcasted_iota(jnp.int32, (1, E * FAST_ROWS), 1) % FAST_ROWS).astype(F32)
        onehot = jnp.where(spread == slot_row, 1.0, 0.0).astype(BF16)
        acc_ref[...] = h2_ref[...] + jnp.dot(onehot, win_ref[slot], preferred_element_type=F32)

    @pl.when(jnp.logical_not(is_fast))
    def _():
        acc = h2_ref[...]
        col = lax.broadcasted_iota(jnp.int32, (CHUNK, WIN_ROWS), 1).astype(F32)
        for e in range(E):
            a = window_start(step, e)
            copy = pltpu.make_async_copy(yc_ref.at[e, pl.ds(a, WIN_ROWS), :], big_ref, big_sem)
            copy.start()
            copy.wait()
            onehot = jnp.where(col == post_ref[:, e:e + 1] - a.astype(F32), 1.0, 0.0).astype(BF16)
            acc = acc + jnp.dot(onehot, big_ref[...], preferred_element_type=F32)
        acc_ref[...] = acc

    acc = acc_ref[...]
    ms = jnp.mean(acc * acc, axis=-1, keepdims=True)
    y = acc * lax.rsqrt(ms + RMS_EPS) * g_ref[...]

    @pl.when((c >= 1) & (c <= out_blocks))
    def _():
        o_ref[0, :CHUNK - N_META, :] = carry_ref[N_META:, :]
        o_ref[0, CHUNK - N_META:, :] = y[:N_META, :]

    carry_ref[...] = y


def _combine(base_flat, fast, h2, post, g, yc, batch, P, S):
    nt = h2.shape[0]
    nc = nt // CHUNK
    cpb = P // CHUNK
    out_blocks = S // CHUNK
    stacked = N_EXPERTS * FAST_ROWS
    expand = jnp.asarray(np.arange(stacked)[None, :] // FAST_ROWS == np.arange(N_EXPERTS)[:, None], BF16)
    kern = functools.partial(_combine_kernel, nc=nc, cpb=cpb, out_blocks=out_blocks)
    row_blk = lambda b, c, base, fast: (b * cpb + c, 0)
    return pl.pallas_call(
        kern,
        grid_spec=pltpu.PrefetchScalarGridSpec(
            num_scalar_prefetch=2,
            grid=(batch, cpb),
            in_specs=[
                pl.BlockSpec((CHUNK, D_MODEL), row_blk),
                pl.BlockSpec((CHUNK, N_EXPERTS), row_blk),
                pl.BlockSpec((1, D_MODEL), lambda b, c, base, fast: (0, 0)),
                pl.BlockSpec((N_EXPERTS, stacked), lambda b, c, base, fast: (0, 0)),
                pl.BlockSpec(memory_space=pl.ANY),
            ],
            out_specs=pl.BlockSpec(
                (1, CHUNK, D_MODEL), lambda b, c, base, fast: (b, jnp.clip(c - 1, 0, out_blocks - 1), 0)),
            scratch_shapes=[
                pltpu.VMEM((2, stacked, D_MODEL), BF16),
                pltpu.VMEM((WIN_ROWS, D_MODEL), BF16),
                pltpu.VMEM((CHUNK, D_MODEL), F32),
                pltpu.VMEM((CHUNK, D_MODEL), F32),
                pltpu.SemaphoreType.DMA((2, N_EXPERTS)),
                pltpu.SemaphoreType.DMA,
            ],
        ),
        out_shape=jax.ShapeDtypeStruct((batch, S, D_MODEL), F32),
        compiler_params=_cparams(("arbitrary", "arbitrary")),
        name="combine",
    )(base_flat, fast, h2, post, g, expand, yc)


def _swa_head_perm():
    perm = np.arange(SWA_WIDTH).reshape(SWA_KV_HEADS, SWA_GROUP, HEAD_DIM)
    return perm.transpose(1, 0, 2).reshape(-1)


def _prep_params(rel_bias, w_in, w_out):
    perm = _swa_head_perm()
    scale = np.ones((IN_COLS,), np.float32)
    scale[:DIFF_WIDTH] = HEAD_DIM ** -0.5
    scale[3 * DIFF_WIDTH:3 * DIFF_WIDTH + SWA_WIDTH] = HEAD_DIM ** -0.5
    cols = np.arange(IN_COLS)
    cols[3 * DIFF_WIDTH:3 * DIFF_WIDTH + SWA_WIDTH] = 3 * DIFF_WIDTH + perm
    w_in_p = (w_in * scale)[:, cols].astype(BF16)
    rows = np.arange(MIX_WIDTH)
    rows[DIFF_WIDTH:] = DIFF_WIDTH + perm
    w_out_p = w_out[rows, :].astype(BF16)

    T = SEQ_TILE
    table = rel_bias.astype(F32)

    def lookup(idx, cols):
        onehot = (idx[..., None] == jnp.arange(N_BUCKETS)).astype(F32)
        return jnp.einsum("...b,bc->...c", onehot, table[:, cols], precision=lax.Precision.HIGHEST)

    kk = jnp.arange(T)[:, None]
    qq = jnp.arange(T)[None, :]
    thresholds = _bucket_thresholds(4 * T)
    idx = jnp.stack([_rel_bucket(d * T + kk - qq, thresholds) for d in (-2, -1, 0, 1, 2)])
    diff_tiles = lookup(idx, slice(0, 2 * DIFF_HEADS))
    diff_tiles = diff_tiles.transpose(3, 0, 1, 2).reshape(DIFF_HEADS, 2, 5, T, T) * LOG2E
    half = N_BUCKETS // 2
    diff_consts = jnp.stack([table[half - 1, 0:2 * DIFF_HEADS:2], table[N_BUCKETS - 1, 0:2 * DIFF_HEADS:2],
                             table[half - 1, 1:2 * DIFF_HEADS:2], table[N_BUCKETS - 1, 1:2 * DIFF_HEADS:2]],
                            axis=1) * LOG2E
    rel = jnp.arange(SWA_KEYS + T)[None, :] - T - jnp.arange(T)[:, None]
    swa_tiles = lookup(_rel_bucket(rel, thresholds), slice(2 * DIFF_HEADS, None))
    swa_tiles = jnp.where((jnp.abs(rel) <= WINDOW)[:, :, None], swa_tiles * LOG2E, NEG).transpose(2, 0, 1)
    return w_in_p, w_out_p, diff_tiles, diff_consts, swa_tiles


def _plan_rows(seq_len):
    nsub = min((4, 5, 6), key=lambda n: (-(-seq_len // (n * SEQ_TILE)) * n, -n))
    return nsub, -(-seq_len // (nsub * SEQ_TILE)) * nsub * SEQ_TILE


def _trunk(x, meta_tokens, prep, attn_norm_g, lamv, subln_g, swa_sink, ffn_norm_g, w_router,
           w_gate, w_up, w_down, final_norm_g):
    w_in_p, w_out_p, diff_tiles, diff_consts, swa_tiles = prep
    B, S, _ = x.shape
    L = S + N_META
    nsub, P = _plan_rows(L)
    rows = B * P
    capacity = EC_FACTOR * (B * L) // N_EXPERTS
    ffn_rows = -(-capacity // BF16_ROWS) * BF16_ROWS
    cp = -(-(capacity + WIN_ROWS) // BF16_ROWS) * BF16_ROWS

    h0 = jnp.concatenate([jnp.broadcast_to(meta_tokens.astype(x.dtype)[None], (B, N_META, D_MODEL)), x,
                          jnp.zeros((B, P - L, D_MODEL), x.dtype)], axis=1).reshape(rows, D_MODEL)

    tm = SEQ_TILE * _largest_divisor(rows // SEQ_TILE, 2)
    proj, vt = _proj(h0, attn_norm_g.reshape(1, D_MODEL), w_in_p, tm)
    diff_o = _diff_attention(proj, vt, diff_consts, lamv, diff_tiles, subln_g.reshape(LANES, 1), B, P, L, nsub)
    swa_o = _swa_attention(proj, swa_sink * LOG2E, swa_tiles, B, P, L)
    h2, m, probs3 = _out_router(diff_o, swa_o, h0, w_out_p, ffn_norm_g.reshape(1, D_MODEL), w_router, B, P, L)
    pos, post, gate, base3 = _topk(probs3, capacity)
    base_ec = base3[:, :, 0].T
    nxt = jnp.concatenate([base_ec[:, 1:], jnp.full((N_EXPERTS, 1), capacity, jnp.int32)], axis=1)
    cnt_ec = nxt - base_ec
    fast = jnp.all(cnt_ec <= FAST_CNT, axis=0).astype(jnp.int32)
    base_flat = base_ec.reshape(-1)
    xc, gc = _gather(base_flat, cnt_ec.reshape(-1), pos, gate, m, cp)
    yc = _ffn(xc, gc, w_gate, w_up, w_down, ffn_rows)
    return _combine(base_flat, fast, h2, post, final_norm_g.reshape(1, D_MODEL), yc, B, P, S)


def kernel(x_prompt, x_sample, meta_tokens, rel_bias, attn_norm_g, w_in, diff_lambda_q1, diff_lambda_k1,
           diff_lambda_q2, diff_lambda_k2, diff_subln_g, swa_sink, w_out, ffn_norm_g, w_router, w_gate, w_up,
           w_down, final_norm_g):
    prep = _prep_params(rel_bias, w_in[0], w_out[0])
    lamv = jnp.stack([diff_lambda_q1[0], diff_lambda_k1[0], diff_lambda_q2[0], diff_lambda_k2[0]]).astype(F32)
    args = (meta_tokens, prep, attn_norm_g[0], lamv, diff_subln_g[0], swa_sink[0].astype(F32), ffn_norm_g[0],
            w_router[0], jnp.swapaxes(w_gate[0], 1, 2), jnp.swapaxes(w_up[0], 1, 2), w_down[0], final_norm_g)
    return (_trunk(x_prompt, *args), _trunk(x_sample, *args))
```

```python
import functools
import math

import jax
import jax.numpy as jnp
import numpy as np
from jax import lax
from jax.experimental import pallas as pl
from jax.experimental.pallas import tpu as pltpu

F32 = jnp.float32
BF16 = jnp.bfloat16

D_MODEL = 1024
HEAD_DIM = 64
N_META = 16
DIFF_HEADS = 4
DIFF_WIDTH = DIFF_HEADS * 2 * HEAD_DIM
SWA_Q_HEADS = 8
SWA_KV_HEADS = 2
SWA_GROUP = SWA_Q_HEADS // SWA_KV_HEADS
SWA_WIDTH = SWA_Q_HEADS * HEAD_DIM
SWA_KV_WIDTH = SWA_KV_HEADS * HEAD_DIM
MIX_WIDTH = DIFF_WIDTH + SWA_WIDTH
IN_COLS = 3 * DIFF_WIDTH + SWA_WIDTH + 2 * SWA_KV_WIDTH
WINDOW = 128
N_BUCKETS = 32
MAX_DISTANCE = 128
N_EXPERTS = 16
EC_FACTOR = 2
D_FF = 2752
RMS_EPS = 1e-6
LAM_INIT = 0.8 - 0.6 * math.exp(-0.3 * 0)

LANES = 128
SUBLANES = 8
F32_MANT_BITS = 23
F32_EXP_BIAS = 127
BF16_ROWS = 16
VMEM_LIMIT = 56 * 1024 * 1024

SEQ_TILE = 256
SWA_KEYS = SEQ_TILE + 2 * WINDOW
NEAR_GROUP = 2
FAR_GROUP = 4
MXU_TILE = 256
FF_TILE = 3 * MXU_TILE
FFN_ROW_BLOCKS = 2
CHUNK = LANES
VT_ROWS = LANES + BF16_ROWS
LOG2E = math.log2(math.e)
WIN_ROWS = CHUNK + BF16_ROWS
FAST_ROWS = 64
FAST_CNT = FAST_ROWS - BF16_ROWS
GATHER_GROUP = 4
GATHER_EXPERTS = 2
STALE_MAX_LIMIT = 64.0
NEG = -1e30

_NT = (((1,), (1,)), ((), ()))


def _cparams(sem):
    return pltpu.CompilerParams(dimension_semantics=sem, vmem_limit_bytes=VMEM_LIMIT)


def _align_down_bf16_rows(count):
    shift = BF16_ROWS.bit_length() - 1
    return lax.shift_left(lax.shift_right_logical(count, shift), shift)


def _largest_divisor(n, cap):
    return max(d for d in range(1, cap + 1) if n % d == 0)


def _bucket_thresholds(max_rel):
    half = N_BUCKETS // 2
    max_exact = half // 2
    n = jnp.arange(max_rel)
    large = max_exact + (jnp.log(jnp.maximum(n, 1).astype(F32) / max_exact)
                         / math.log(MAX_DISTANCE / max_exact) * (half - max_exact)).astype(jnp.int32)
    mag = jnp.where(n < max_exact, n, jnp.minimum(large, half - 1))
    mag = jnp.arange(half)[mag]
    return jnp.sum(mag[None, :] < jnp.arange(1, half)[:, None], axis=1)


def _rel_bucket(rel, thresholds):
    mag = jnp.sum(jnp.abs(rel)[..., None] >= thresholds, axis=-1)
    return jnp.where(rel > 0, N_BUCKETS // 2, 0) + mag


def _proj_kernel(h_ref, g_ref, w_ref, proj_ref, vt_ref):
    x = h_ref[...]
    ms = jnp.mean(x * x, axis=-1, keepdims=True)
    a = (x * lax.rsqrt(ms + RMS_EPS) * g_ref[...]).astype(BF16)
    proj = jnp.dot(a, w_ref[...], preferred_element_type=F32)
    q_lo, q_hi = 3 * DIFF_WIDTH, 3 * DIFF_WIDTH + SWA_WIDTH
    proj_ref[:, :DIFF_WIDTH] = (proj[:, :DIFF_WIDTH] * LOG2E).astype(BF16)
    proj_ref[:, DIFF_WIDTH:q_lo] = proj[:, DIFF_WIDTH:q_lo].astype(BF16)
    proj_ref[:, q_lo:q_hi] = (proj[:, q_lo:q_hi] * LOG2E).astype(BF16)
    proj_ref[:, q_hi:] = proj[:, q_hi:].astype(BF16)
    ones = jnp.ones((VT_ROWS - LANES, x.shape[0]), BF16)
    for h in range(DIFF_HEADS):
        v = proj[:, 2 * DIFF_WIDTH + h * LANES: 2 * DIFF_WIDTH + (h + 1) * LANES]
        vt_ref[h, :LANES, :] = v.T.astype(BF16)
        vt_ref[h, LANES:, :] = ones


def _proj(h0, g, w_in_bf16, tm):
    rows = h0.shape[0]
    return pl.pallas_call(
        _proj_kernel,
        grid=(rows // tm,),
        in_specs=[
            pl.BlockSpec((tm, D_MODEL), lambda i: (i, 0)),
            pl.BlockSpec((1, D_MODEL), lambda i: (0, 0)),
            pl.BlockSpec((D_MODEL, IN_COLS), lambda i: (0, 0)),
        ],
        out_specs=[
            pl.BlockSpec((tm, IN_COLS), lambda i: (i, 0)),
            pl.BlockSpec((DIFF_HEADS, VT_ROWS, tm), lambda i: (0, 0, i)),
        ],
        out_shape=[
            jax.ShapeDtypeStruct((rows, IN_COLS), BF16),
            jax.ShapeDtypeStruct((DIFF_HEADS, VT_ROWS, rows), BF16),
        ],
        compiler_params=_cparams(("parallel",)),
        name="proj",
    )(h0, g, w_in_bf16)


def _diff_kernel(consts_ref, lamv_ref, q_ref, k_ref, vt_ref, bias_ref, g_ref, o_ref,
                 acc_ref, m_ref, excess_ref, *, seq_len, nkv, nsub):
    T = SEQ_TILE
    h = pl.program_id(1)
    i = pl.program_id(2)
    q = q_ref[...]
    lane = lax.broadcasted_iota(jnp.int32, q.shape, 1)
    zero = jnp.zeros_like(q)
    q_maps = (jnp.where(lane < HEAD_DIM, q, zero), jnp.where(lane >= HEAD_DIM, q, zero))

    def chunks(j0, count, *, tile, side, mask, exact):
        maps = range(2)
        c = [0.0, 0.0] if tile else [consts_ref[h, 2 * mp + side] for mp in maps]
        offs = [pl.multiple_of((j0 + t) * T, T) for t in range(count)]
        s = [[lax.dot_general(k_ref[pl.ds(off, T), :], q_maps[mp], _NT, preferred_element_type=F32)
              for mp in maps] for off in offs]
        m_run = [m_ref[mp] for mp in maps]
        lead = list(m_run)
        peak = [None, None]
        pv_sum = [None, None]
        for t, off in enumerate(offs):
            vb = vt_ref[0, :, pl.ds(off, T)]
            if mask:
                kvalid = (off + lax.broadcasted_iota(jnp.int32, (T, 1), 0)) < seq_len
            for mp in maps:
                st = s[t][mp]
                if tile:
                    st = jnp.concatenate(
                        [st[:, u * T:(u + 1) * T] + bias_ref[0, mp, jnp.clip(j0 + t - (i * nsub + u), -2, 2) + 2]
                         for u in range(nsub)], axis=1)
                if mask:
                    st = jnp.where(kvalid, st, NEG)
                top = jnp.max(st, axis=0, keepdims=True) + c[mp]
                if exact:
                    m_new = jnp.maximum(m_run[mp], top)
                    alpha = jnp.exp2(m_run[mp] - m_new)
                    m_run[mp] = m_new
                    p = jnp.exp2(st - (m_new - c[mp])).astype(BF16)
                    acc_ref[mp] = alpha * acc_ref[mp] + jnp.dot(vb, p, preferred_element_type=F32)
                else:
                    p = jnp.exp2(st - (lead[mp] - c[mp])).astype(BF16)
                    pv = jnp.dot(vb, p, preferred_element_type=F32)
                    pv_sum[mp] = pv if t == 0 else pv_sum[mp] + pv
                    peak[mp] = top if t == 0 else jnp.maximum(peak[mp], top)
        for mp in maps:
            if not exact:
                m_run[mp] = jnp.maximum(lead[mp], peak[mp])
                excess_ref[mp] = jnp.maximum(excess_ref[mp], peak[mp] - lead[mp])
                acc_ref[mp] = (acc_ref[mp] + pv_sum[mp]) * jnp.exp2(lead[mp] - m_run[mp])
            m_ref[mp] = m_run[mp]

    def loop(lo, hi, group, **kw):
        ngroups = jnp.maximum(hi - lo, 0) // group

        def grouped(g, carry):
            chunks(lo + g * group, group, **kw)
            return carry

        def single(j, carry):
            chunks(j, 1, **kw)
            return carry

        lax.fori_loop(0, ngroups, grouped, 0)
        if group > 1:
            lax.fori_loop(lo + ngroups * group, hi, single, 0)

    def all_chunks(exact):
        near_lo = jnp.maximum(i * nsub - 1, 0)
        near_hi = jnp.minimum((i + 1) * nsub + 1, nkv)
        loop(0, near_lo, FAR_GROUP, tile=False, side=0, mask=False, exact=exact)
        loop(near_lo, near_hi, NEAR_GROUP if not exact else 1, tile=True, side=0, mask=True, exact=exact)
        loop(near_hi, nkv - 1, FAR_GROUP, tile=False, side=1, mask=False, exact=exact)
        loop(jnp.maximum(near_hi, nkv - 1), nkv, 1, tile=False, side=1, mask=True, exact=exact)

    acc_ref[...] = jnp.zeros_like(acc_ref)
    excess_ref[...] = jnp.zeros_like(excess_ref)
    for mp in range(2):
        s0 = lax.dot_general(k_ref[pl.ds(0, T), :], q_maps[mp], _NT, preferred_element_type=F32)
        m_ref[mp] = jnp.max(s0, axis=0, keepdims=True) + consts_ref[h, 2 * mp]
    all_chunks(exact=False)

    @pl.when(jnp.max(excess_ref[...]) > STALE_MAX_LIMIT)
    def _():
        acc_ref[...] = jnp.zeros_like(acc_ref)
        m_ref[...] = jnp.full_like(m_ref, NEG)
        all_chunks(exact=True)

    lamv = lamv_ref[...]
    lam = (jnp.exp(jnp.sum(lamv[0:1] * lamv[1:2], axis=-1, keepdims=True))
           - jnp.exp(jnp.sum(lamv[2:3] * lamv[3:4], axis=-1, keepdims=True)) + LAM_INIT)
    o = (acc_ref[0, :LANES] / acc_ref[0, LANES:LANES + 1]
         - lam * (acc_ref[1, :LANES] / acc_ref[1, LANES:LANES + 1]))
    ms = jnp.mean(o * o, axis=0, keepdims=True)
    y = o * lax.rsqrt(ms + RMS_EPS) * g_ref[...] * (1.0 - LAM_INIT)
    o_ref[...] = y.T.astype(BF16)


def _diff_attention(proj, vt, consts, lamv, bias_t, subln_g, batch, P, seq_len, nsub):
    T = SEQ_TILE
    tq = nsub * T
    nq = P // tq
    nkv = pl.cdiv(seq_len, T)
    kern = functools.partial(_diff_kernel, seq_len=seq_len, nkv=nkv, nsub=nsub)
    return pl.pallas_call(
        kern,
        grid=(batch, DIFF_HEADS, nq),
        in_specs=[
            pl.BlockSpec(memory_space=pltpu.SMEM),
            pl.BlockSpec((4, HEAD_DIM), lambda b, h, i: (0, 0)),
            pl.BlockSpec((tq, LANES), lambda b, h, i: (b * nq + i, h)),
            pl.BlockSpec((P, LANES), lambda b, h, i: (b, DIFF_HEADS + h)),
            pl.BlockSpec((1, VT_ROWS, P), lambda b, h, i: (h, 0, b)),
            pl.BlockSpec((1, 2, 5, T, T), lambda b, h, i: (h, 0, 0, 0, 0)),
            pl.BlockSpec((LANES, 1), lambda b, h, i: (0, 0)),
        ],
        out_specs=pl.BlockSpec((tq, LANES), lambda b, h, i: (b * nq + i, h)),
        out_shape=jax.ShapeDtypeStruct((batch * P, DIFF_WIDTH), BF16),
        scratch_shapes=[
            pltpu.VMEM((2, VT_ROWS, tq), F32),
            pltpu.VMEM((2, 1, tq), F32),
            pltpu.VMEM((2, 1, tq), F32),
        ],
        compiler_params=_cparams(("parallel", "parallel", "parallel")),
        name="diff_attn",
    )(consts, lamv, proj, proj, vt, bias_t, subln_g)


def _swa_kernel(sink_ref, q_ref, k_ref, v_ref, bias_ref, o_ref, *, seq_len, P):
    T = SEQ_TILE
    i = pl.program_id(1)
    start = pl.multiple_of(jnp.clip(i * T - WINDOW, 0, P - SWA_KEYS), LANES)
    boff = pl.multiple_of(start - i * T + T, LANES)

    def attend(masked):
        kw = k_ref[pl.ds(start, SWA_KEYS), :]
        vw = v_ref[pl.ds(start, SWA_KEYS), :]
        kvalid = (start + lax.broadcasted_iota(jnp.int32, (1, SWA_KEYS), 1)) < seq_len
        lane = lax.broadcasted_iota(jnp.int32, (T, LANES), 1)
        for j in range(SWA_GROUP):
            qp = q_ref[:, j * LANES:(j + 1) * LANES]
            zero = jnp.zeros_like(qp)
            outs = []
            for kvh in range(SWA_KV_HEADS):
                head = kvh * SWA_GROUP + j
                in_half = (lane < HEAD_DIM) if kvh == 0 else (lane >= HEAD_DIM)
                qe = jnp.where(in_half, qp, zero)
                s = lax.dot_general(qe, kw, _NT, preferred_element_type=F32)
                s = s + bias_ref[head, :, pl.ds(boff, SWA_KEYS)]
                if masked:
                    s = jnp.where(kvalid, s, NEG)
                sink = sink_ref[head]
                m = jnp.maximum(jnp.max(s, axis=-1, keepdims=True), sink)
                p = jnp.exp2(s - m)
                l = jnp.sum(p, axis=-1, keepdims=True) + jnp.exp2(sink - m)
                outs.append(jnp.dot(p.astype(BF16), vw, preferred_element_type=F32) / l)
            o_ref[:, j * LANES:(j + 1) * LANES] = jnp.where(lane < HEAD_DIM, outs[0], outs[1]).astype(BF16)

    has_padding = start + SWA_KEYS > seq_len
    pl.when(has_padding)(lambda: attend(True))
    pl.when(jnp.logical_not(has_padding))(lambda: attend(False))


def _swa_attention(proj, sink, bias_w, batch, P, seq_len):
    T = SEQ_TILE
    nq = P // T
    q_blk = (3 * DIFF_WIDTH) // SWA_WIDTH
    k_blk = (3 * DIFF_WIDTH + SWA_WIDTH) // LANES
    kern = functools.partial(_swa_kernel, seq_len=seq_len, P=P)
    return pl.pallas_call(
        kern,
        grid=(batch, nq),
        in_specs=[
            pl.BlockSpec(memory_space=pltpu.SMEM),
            pl.BlockSpec((T, SWA_WIDTH), lambda b, i: (b * nq + i, q_blk)),
            pl.BlockSpec((P, LANES), lambda b, i: (b, k_blk)),
            pl.BlockSpec((P, LANES), lambda b, i: (b, k_blk + 1)),
            pl.BlockSpec((SWA_Q_HEADS, T, SWA_KEYS + T), lambda b, i: (0, 0, 0)),
        ],
        out_specs=pl.BlockSpec((T, SWA_WIDTH), lambda b, i: (b * nq + i, 0)),
        out_shape=jax.ShapeDtypeStruct((batch * P, SWA_WIDTH), BF16),
        compiler_params=_cparams(("parallel", "parallel")),
        name="swa_attn",
    )(sink, proj, proj, proj, bias_w)


def _out_kernel(do_ref, so_ref, h_ref, w_ref, g_ref, wrh_ref, wrl_ref, h2_ref, m_ref, p3_ref, *, seq_len, tm):
    i = pl.program_id(1)
    h2 = (h_ref[...]
          + jnp.dot(do_ref[...], w_ref[:DIFF_WIDTH, :], preferred_element_type=F32)
          + jnp.dot(so_ref[...], w_ref[DIFF_WIDTH:, :], preferred_element_type=F32))
    h2_ref[...] = h2
    ms = jnp.mean(h2 * h2, axis=-1, keepdims=True)
    mf = h2 * lax.rsqrt(ms + RMS_EPS) * g_ref[...]
    mf_hi = mf.astype(BF16)
    m_ref[...] = mf_hi
    mf_lo = (mf - mf_hi.astype(F32)).astype(BF16)
    logits = (jnp.dot(mf_hi, wrh_ref[...], preferred_element_type=F32)
              + jnp.dot(mf_lo, wrh_ref[...], preferred_element_type=F32)
              + jnp.dot(mf_hi, wrl_ref[...], preferred_element_type=F32))
    lane = lax.broadcasted_iota(jnp.int32, logits.shape, 1)
    logits = jnp.where(lane < N_EXPERTS, logits, NEG)
    e = jnp.exp(logits - jnp.max(logits, axis=-1, keepdims=True))
    probs = (e / jnp.sum(e, axis=-1, keepdims=True)).T[:N_EXPERTS]
    pos = i * tm + lax.broadcasted_iota(jnp.int32, (1, tm), 1)
    probs = jnp.where(pos < seq_len, probs, -1.0)
    for c in range(tm // CHUNK):
        p3_ref[c] = probs[:, c * CHUNK:(c + 1) * CHUNK]


def _out_router(diff_o, swa_o, h0, w_out_bf16, g, w_router, batch, P, seq_len):
    tm = SEQ_TILE
    nq = P // tm
    rows = batch * P
    kern = functools.partial(_out_kernel, seq_len=seq_len, tm=tm)
    wr = jnp.pad(w_router.astype(F32), ((0, 0), (0, LANES - N_EXPERTS)))
    wr_hi = wr.astype(BF16)
    wr_lo = (wr - wr_hi.astype(F32)).astype(BF16)
    return pl.pallas_call(
        kern,
        grid=(batch, nq),
        in_specs=[
            pl.BlockSpec((tm, DIFF_WIDTH), lambda b, i: (b * nq + i, 0)),
            pl.BlockSpec((tm, SWA_WIDTH), lambda b, i: (b * nq + i, 0)),
            pl.BlockSpec((tm, D_MODEL), lambda b, i: (b * nq + i, 0)),
            pl.BlockSpec((MIX_WIDTH, D_MODEL), lambda b, i: (0, 0)),
            pl.BlockSpec((1, D_MODEL), lambda b, i: (0, 0)),
            pl.BlockSpec((D_MODEL, LANES), lambda b, i: (0, 0)),
            pl.BlockSpec((D_MODEL, LANES), lambda b, i: (0, 0)),
        ],
        out_specs=[
            pl.BlockSpec((tm, D_MODEL), lambda b, i: (b * nq + i, 0)),
            pl.BlockSpec((tm, D_MODEL), lambda b, i: (b * nq + i, 0)),
            pl.BlockSpec((tm // CHUNK, N_EXPERTS, CHUNK), lambda b, i: (b * nq + i, 0, 0)),
        ],
        out_shape=[
            jax.ShapeDtypeStruct((rows, D_MODEL), F32),
            jax.ShapeDtypeStruct((rows, D_MODEL), BF16),
            jax.ShapeDtypeStruct((rows // CHUNK, N_EXPERTS, CHUNK), F32),
        ],
        compiler_params=_cparams(("parallel", "parallel")),
        name="out_router",
    )(diff_o, swa_o, h0, w_out_bf16, g, wr_hi, wr_lo)


def _topk_kernel(p3_ref, pos_ref, post_ref, gate_ref, base_ref, incl_ref, flag_ref, *, capacity, nc):
    E = N_EXPERTS
    probs = p3_ref[...]

    def count(pred):
        part = jnp.sum(jnp.where(pred, 1.0, 0.0), axis=0, keepdims=True)
        return jnp.broadcast_to(jnp.sum(part, axis=-1, keepdims=True), part.shape)

    def key_value(key):
        expo = key >> F32_MANT_BITS
        frac = (key & ((1 << F32_MANT_BITS) - 1)).astype(F32) * (2.0 ** -F32_MANT_BITS)
        frac = jnp.where(expo == 0, frac, 1.0 + frac)
        deficit = F32_EXP_BIAS - jnp.maximum(expo, 1)
        scale = jnp.ones_like(frac)
        for b in range(7):
            scale = scale * jnp.where(((deficit >> b) & 1) == 1, 2.0 ** -(1 << b), 1.0)
        return jnp.where(deficit < 0, 2.0, frac * scale)

    def bisect(_, lohi):
        lo, hi = lohi
        mid = lo + ((hi - lo + 1) >> 1)
        ok = count(probs >= key_value(mid)) >= capacity
        return jnp.where(ok, mid, lo), jnp.where(ok, hi, mid - 1)

    lo0 = jnp.zeros((1, E, CHUNK), jnp.int32)
    hi0 = jnp.full((1, E, CHUNK), 0x7F800000, jnp.int32)
    thr_key, _ = lax.fori_loop(0, 32, bisect, (lo0, hi0))
    thr = key_value(thr_key)
    gt = probs > thr
    eq = probs == thr
    need = capacity - count(gt)[0]

    tri = (lax.broadcasted_iota(jnp.int32, (CHUNK, CHUNK), 0)
           <= lax.broadcasted_iota(jnp.int32, (CHUNK, CHUNK), 1)).astype(BF16)

    def inclusive_prefix(flags):
        f2 = flags.astype(BF16).reshape(nc * E, CHUNK)
        return jnp.dot(f2, tri, preferred_element_type=F32).reshape(nc, E, CHUNK)

    eqf = jnp.where(eq, 1.0, 0.0)
    incl_ref[...] = inclusive_prefix(eqf)
    flag_ref[...] = eqf

    def tie_scan(c, run):
        inc = incl_ref[c]
        e_c = flag_ref[c]
        take = (e_c > 0.0) & ((run + inc - e_c) < need)
        flag_ref[c] = jnp.where(take, 1.0, 0.0)
        return run + jnp.broadcast_to(inc[:, CHUNK - 1:CHUNK], inc.shape)

    lax.fori_loop(0, nc, tie_scan, jnp.zeros((E, CHUNK), F32))
    self = jnp.where(gt, 1.0, flag_ref[...])
    flag_ref[...] = self
    incl_ref[...] = inclusive_prefix(self)

    zpad = jnp.zeros((CHUNK - E, CHUNK), F32)

    def to_token_major(x):
        return jnp.concatenate([x, zpad], axis=0).T[:, :E]

    def pos_scan(c, run):
        inc = incl_ref[c]
        s_c = flag_ref[c]
        sel = s_c > 0.0
        pos = jnp.where(sel, run + inc - s_c, -1.0)
        gate = jnp.where(sel, p3_ref[c], 0.0)
        off = pl.multiple_of(c * CHUNK, CHUNK)
        pos_ref[:, pl.ds(off, CHUNK)] = pos
        post_ref[pl.ds(off, CHUNK), :] = to_token_major(pos)
        gate_ref[:, pl.ds(off, CHUNK)] = gate
        base_ref[c] = run.astype(jnp.int32)
        return run + jnp.broadcast_to(inc[:, CHUNK - 1:CHUNK], inc.shape)

    lax.fori_loop(0, nc, pos_scan, jnp.zeros((E, CHUNK), F32))


def _topk(probs3, capacity):
    nc = probs3.shape[0]
    nt = nc * CHUNK
    kern = functools.partial(_topk_kernel, capacity=capacity, nc=nc)
    return pl.pallas_call(
        kern,
        out_shape=[
            jax.ShapeDtypeStruct((N_EXPERTS, nt), F32),
            jax.ShapeDtypeStruct((nt, N_EXPERTS), F32),
            jax.ShapeDtypeStruct((N_EXPERTS, nt), F32),
            jax.ShapeDtypeStruct((nc, N_EXPERTS, CHUNK), jnp.int32),
        ],
        scratch_shapes=[
            pltpu.VMEM((nc, N_EXPERTS, CHUNK), F32),
            pltpu.VMEM((nc, N_EXPERTS, CHUNK), F32),
        ],
        compiler_params=pltpu.CompilerParams(vmem_limit_bytes=VMEM_LIMIT),
        name="topk",
    )(probs3)


def _gather_kernel(base_ref, cnt_ref, pos_ref, gate_ref, m_ref, xc_ref, gc_ref, *, nc, sub):
    sb = pl.program_id(1)

    @pl.when(sb == 0)
    def _():
        xc_ref[...] = jnp.zeros_like(xc_ref)
        gc_ref[...] = jnp.zeros_like(gc_ref)

    def expert(ee):
        e = pl.program_id(0) * GATHER_EXPERTS + ee
        first = e * nc + sb * sub

        def copy_rows(u, k, rows):
            erow = lax.broadcasted_iota(jnp.int32, (N_EXPERTS, CHUNK), 0)
            row = lax.broadcasted_iota(jnp.int32, (rows, CHUNK), 0).astype(F32)
            starts = [pl.multiple_of(_align_down_bf16_rows(base_ref[first + u + t]), BF16_ROWS)
                      for t in range(k)]
            offs = [pl.multiple_of((u + t) * CHUNK, CHUNK) for t in range(k)]
            prow = [jnp.sum(jnp.where(erow == e, pos_ref[:, pl.ds(off, CHUNK)], 0.0), axis=0, keepdims=True)
                    - a.astype(F32) for a, off in zip(starts, offs)]
            hit = [row == p for p in prow]
            onehot = [jnp.where(ht, 1.0, 0.0).astype(BF16) for ht in hit]
            g = [jnp.dot(oh, m_ref[pl.ds(off, CHUNK), :], preferred_element_type=F32)
                 for oh, off in zip(onehot, offs)]
            ones = jnp.ones((CHUNK, LANES), BF16)
            gg = []
            for ht, off in zip(hit, offs):
                gate = jnp.sum(jnp.where(erow == e, gate_ref[:, pl.ds(off, CHUNK)], 0.0), axis=0, keepdims=True)
                hi = gate.astype(BF16).astype(F32)
                lo = gate - hi
                lhs = jnp.concatenate([jnp.where(ht, hi, 0.0), jnp.where(ht, lo, 0.0)], axis=0).astype(BF16)
                both = jnp.dot(lhs, ones, preferred_element_type=F32)
                gg.append(both[:rows] + both[rows:])
            for a, gt, gs in zip(starts, g, gg):
                xc_ref[ee, pl.ds(a, rows), :] += gt.astype(BF16)
                gc_ref[ee, pl.ds(a, rows), :] += gs

        def single(u):
            few = cnt_ref[first + u] <= FAST_CNT
            pl.when(few)(lambda: copy_rows(u, 1, FAST_ROWS))
            pl.when(jnp.logical_not(few))(lambda: copy_rows(u, 1, WIN_ROWS))

        def group(gi, carry):
            u0 = gi * GATHER_GROUP
            most = functools.reduce(jnp.maximum, [cnt_ref[first + u0 + t] for t in range(GATHER_GROUP)])
            few = most <= FAST_CNT
            pl.when(few)(lambda: copy_rows(u0, GATHER_GROUP, FAST_ROWS))
            pl.when(jnp.logical_not(few))(lambda: copy_rows(u0, GATHER_GROUP, WIN_ROWS))
            return carry

        ngroups = sub // GATHER_GROUP
        lax.fori_loop(0, ngroups, group, 0)
        for u in range(ngroups * GATHER_GROUP, sub):
            single(u)

    for ee in range(GATHER_EXPERTS):
        expert(ee)


def _gather(base_flat, cnt_flat, pos, gate, m, cp):
    nt = pos.shape[1]
    nc = nt // CHUNK
    sub = _largest_divisor(nc, 17)
    ts = sub * CHUNK
    kern = functools.partial(_gather_kernel, nc=nc, sub=sub)
    return pl.pallas_call(
        kern,
        grid_spec=pltpu.PrefetchScalarGridSpec(
            num_scalar_prefetch=2,
            grid=(N_EXPERTS // GATHER_EXPERTS, nc // sub),
            in_specs=[
                pl.BlockSpec((N_EXPERTS, ts), lambda e, s, base, cnt: (0, s)),
                pl.BlockSpec((N_EXPERTS, ts), lambda e, s, base, cnt: (0, s)),
                pl.BlockSpec((ts, D_MODEL), lambda e, s, base, cnt: (s, 0)),
            ],
            out_specs=[
                pl.BlockSpec((GATHER_EXPERTS, cp, D_MODEL), lambda e, s, base, cnt: (e, 0, 0)),
                pl.BlockSpec((GATHER_EXPERTS, cp, LANES), lambda e, s, base, cnt: (e, 0, 0)),
            ],
        ),
        out_shape=[
            jax.ShapeDtypeStruct((N_EXPERTS, cp, D_MODEL), BF16),
            jax.ShapeDtypeStruct((N_EXPERTS, cp, LANES), F32),
        ],
        compiler_params=_cparams(("parallel", "arbitrary")),
        name="gather",
    )(base_flat, cnt_flat, pos, gate, m)


def _ffn_kernel(x_ref, gc_ref, wg_ref, wu_ref, wd_ref, y_ref, acc_ref, *, rows):
    f = pl.program_id(1)
    nf = pl.num_programs(1)

    @pl.when(f == 0)
    def _():
        acc_ref[...] = jnp.zeros_like(acc_ref)

    block = -(-rows // (FFN_ROW_BLOCKS * BF16_ROWS)) * BF16_ROWS
    bounds = list(range(0, rows, block)) + [rows]

    def step(width, valid):
        wg = wg_ref[0, :width, :].astype(BF16)
        wu = wu_ref[0, :width, :].astype(BF16)
        wd = wd_ref[0, :width, :]
        if valid < width:
            wd = jnp.where(lax.broadcasted_iota(jnp.int32, (width, 1), 0) < valid, wd, 0.0)
        wd = wd.astype(BF16)
        for r0, r1 in zip(bounds[:-1], bounds[1:]):
            x = x_ref[0, r0:r1, :]
            g = lax.dot_general(x, wg, _NT, preferred_element_type=F32)
            u = lax.dot_general(x, wu, _NT, preferred_element_type=F32)
            hid = g * jax.nn.sigmoid(g) * u
            if valid < width:
                hid = jnp.where(lax.broadcasted_iota(jnp.int32, (1, width), 1) < valid, hid, 0.0)
            acc_ref[r0:r1, :] += jnp.dot(hid.astype(BF16), wd, preferred_element_type=F32)

    last_valid = D_FF - (D_FF // FF_TILE) * FF_TILE
    last_width = -(-last_valid // MXU_TILE) * MXU_TILE
    pl.when(f < nf - 1)(lambda: step(FF_TILE, FF_TILE))
    pl.when(f == nf - 1)(lambda: step(last_width, last_valid))

    @pl.when(f == nf - 1)
    def _():
        gate = gc_ref[0, :rows, 0:1]
        y_ref[0, :rows, :] = (acc_ref[...] * gate).astype(BF16)
        y_ref[0, rows:, :] = jnp.zeros((y_ref.shape[1] - rows, D_MODEL), BF16)


def _ffn(xc, gc, w_gate_t, w_up_t, w_down, rows):
    cp = xc.shape[1]
    kern = functools.partial(_ffn_kernel, rows=rows)
    once = pl.Buffered(1)
    w_spec = pl.BlockSpec((1, FF_TILE, D_MODEL), lambda e, f: (e, f, 0))
    return pl.pallas_call(
        kern,
        grid=(N_EXPERTS, pl.cdiv(D_FF, FF_TILE)),
        in_specs=[pl.BlockSpec((1, cp, D_MODEL), lambda e, f: (e, 0, 0), pipeline_mode=once),
                  pl.BlockSpec((1, cp, LANES), lambda e, f: (e, 0, 0), pipeline_mode=once), w_spec, w_spec, w_spec],
        out_specs=pl.BlockSpec((1, cp, D_MODEL), lambda e, f: (e, 0, 0), pipeline_mode=once),
        out_shape=jax.ShapeDtypeStruct((N_EXPERTS, cp, D_MODEL), BF16),
        scratch_shapes=[pltpu.VMEM((rows, D_MODEL), F32)],
        compiler_params=_cparams(("parallel", "arbitrary")),
        name="ffn",
    )(xc, gc, w_gate_t, w_up_t, w_down)


def _combine_kernel(base_ref, fast_ref, h2_ref, post_ref, g_ref, expand_ref, yc_ref, o_ref,
                    win_ref, big_ref, acc_ref, carry_ref, sem_ref, big_sem, *, nc, cpb, out_blocks):
    E = N_EXPERTS
    b = pl.program_id(0)
    c = pl.program_id(1)
    step = b * cpb + c
    nsteps = pl.num_programs(0) * cpb
    slot = step % 2

    def window_start(chunk, e):
        return pl.multiple_of(_align_down_bf16_rows(base_ref[e * nc + chunk]), BF16_ROWS)

    def fast_copy(chunk, e, sl):
        return pltpu.make_async_copy(yc_ref.at[e, pl.ds(window_start(chunk, e), FAST_ROWS), :],
                                     win_ref.at[sl, pl.ds(e * FAST_ROWS, FAST_ROWS), :], sem_ref.at[sl, e])

    @pl.when((step == 0) & (fast_ref[0] == 1))
    def _():
        for e in range(E):
            fast_copy(0, e, 0).start()

    nxt = jnp.minimum(step + 1, nsteps - 1)

    @pl.when((step + 1 < nsteps) & (fast_ref[nxt] == 1))
    def _():
        for e in range(E):
            fast_copy(step + 1, e, 1 - slot).start()

    is_fast = fast_ref[step] == 1

    @pl.when(is_fast)
    def _():
        for e in range(E):
            fast_copy(step, e, slot).wait()
        lane_e = lax.broadcasted_iota(jnp.int32, (1, E), 1)
        a_vec = jnp.zeros((1, E), F32)
        for e in range(E):
            a_vec = jnp.where(lane_e == e, window_start(step, e).astype(F32), a_vec)
        pos = post_ref[...]
        rel = jnp.where(pos >= 0.0, pos - a_vec, -1.0).astype(BF16)
        spread = jnp.dot(rel, expand_ref[...], preferred_element_type=F32)
        slot_row = (lax.broadcasted_iota(jnp.int32, (1, E * FAST_ROWS), 1) % FAST_ROWS).astype(F32)
        onehot = jnp.where(spread == slot_row, 1.0, 0.0).astype(BF16)
        acc_ref[...] = h2_ref[...] + jnp.dot(onehot, win_ref[slot], preferred_element_type=F32)

    @pl.when(jnp.logical_not(is_fast))
    def _():
        acc = h2_ref[...]
        col = lax.broadcasted_iota(jnp.int32, (CHUNK, WIN_ROWS), 1).astype(F32)
        for e in range(E):
            a = window_start(step, e)
            copy = pltpu.make_async_copy(yc_ref.at[e, pl.ds(a, WIN_ROWS), :], big_ref, big_sem)
            copy.start()
            copy.wait()
            onehot = jnp.where(col == post_ref[:, e:e + 1] - a.astype(F32), 1.0, 0.0).astype(BF16)
            acc = acc + jnp.dot(onehot, big_ref[...], preferred_element_type=F32)
        acc_ref[...] = acc

    acc = acc_ref[...]
    ms = jnp.mean(acc * acc, axis=-1, keepdims=True)
    y = acc * lax.rsqrt(ms + RMS_EPS) * g_ref[...]

    @pl.when((c >= 1) & (c <= out_blocks))
    def _():
        o_ref[0, :CHUNK - N_META, :] = carry_ref[N_META:, :]
        o_ref[0, CHUNK - N_META:, :] = y[:N_META, :]

    carry_ref[...] = y


def _combine(base_flat, fast, h2, post, g, yc, batch, P, S):
    nt = h2.shape[0]
    nc = nt // CHUNK
    cpb = P // CHUNK
    out_blocks = S // CHUNK
    stacked = N_EXPERTS * FAST_ROWS
    expand = jnp.asarray(np.arange(stacked)[None, :] // FAST_ROWS == np.arange(N_EXPERTS)[:, None], BF16)
    kern = functools.partial(_combine_kernel, nc=nc, cpb=cpb, out_blocks=out_blocks)
    row_blk = lambda b, c, base, fast: (b * cpb + c, 0)
    return pl.pallas_call(
        kern,
        grid_spec=pltpu.PrefetchScalarGridSpec(
            num_scalar_prefetch=2,
            grid=(batch, cpb),
            in_specs=[
                pl.BlockSpec((CHUNK, D_MODEL), row_blk),
                pl.BlockSpec((CHUNK, N_EXPERTS), row_blk),
                pl.BlockSpec((1, D_MODEL), lambda b, c, base, fast: (0, 0)),
                pl.BlockSpec((N_EXPERTS, stacked), lambda b, c, base, fast: (0, 0)),
                pl.BlockSpec(memory_space=pl.ANY),
            ],
            out_specs=pl.BlockSpec(
                (1, CHUNK, D_MODEL), lambda b, c, base, fast: (b, jnp.clip(c - 1, 0, out_blocks - 1), 0)),
            scratch_shapes=[
                pltpu.VMEM((2, stacked, D_MODEL), BF16),
                pltpu.VMEM((WIN_ROWS, D_MODEL), BF16),
                pltpu.VMEM((CHUNK, D_MODEL), F32),
                pltpu.VMEM((CHUNK, D_MODEL), F32),
                pltpu.SemaphoreType.DMA((2, N_EXPERTS)),
                pltpu.SemaphoreType.DMA,
            ],
        ),
        out_shape=jax.ShapeDtypeStruct((batch, S, D_MODEL), F32),
        compiler_params=_cparams(("arbitrary", "arbitrary")),
        name="combine",
    )(base_flat, fast, h2, post, g, expand, yc)


def _swa_head_perm():
    perm = np.arange(SWA_WIDTH).reshape(SWA_KV_HEADS, SWA_GROUP, HEAD_DIM)
    return perm.transpose(1, 0, 2).reshape(-1)


def _prep_params(rel_bias, w_in, w_out):
    perm = _swa_head_perm()
    scale = np.ones((IN_COLS,), np.float32)
    scale[:DIFF_WIDTH] = HEAD_DIM ** -0.5
    scale[3 * DIFF_WIDTH:3 * DIFF_WIDTH + SWA_WIDTH] = HEAD_DIM ** -0.5
    cols = np.arange(IN_COLS)
    cols[3 * DIFF_WIDTH:3 * DIFF_WIDTH + SWA_WIDTH] = 3 * DIFF_WIDTH + perm
    w_in_p = (w_in * scale)[:, cols].astype(BF16)
    rows = np.arange(MIX_WIDTH)
    rows[DIFF_WIDTH:] = DIFF_WIDTH + perm
    w_out_p = w_out[rows, :].astype(BF16)

    T = SEQ_TILE
    table = rel_bias.astype(F32)

    def lookup(idx, cols):
        onehot = (idx[..., None] == jnp.arange(N_BUCKETS)).astype(F32)
        return jnp.einsum("...b,bc->...c", onehot, table[:, cols], precision=lax.Precision.HIGHEST)

    kk = jnp.arange(T)[:, None]
    qq = jnp.arange(T)[None, :]
    thresholds = _bucket_thresholds(4 * T)
    idx = jnp.stack([_rel_bucket(d * T + kk - qq, thresholds) for d in (-2, -1, 0, 1, 2)])
    diff_tiles = lookup(idx, slice(0, 2 * DIFF_HEADS))
    diff_tiles = diff_tiles.transpose(3, 0, 1, 2).reshape(DIFF_HEADS, 2, 5, T, T) * LOG2E
    half = N_BUCKETS // 2
    diff_consts = jnp.stack([table[half - 1, 0:2 * DIFF_HEADS:2], table[N_BUCKETS - 1, 0:2 * DIFF_HEADS:2],
                             table[half - 1, 1:2 * DIFF_HEADS:2], table[N_BUCKETS - 1, 1:2 * DIFF_HEADS:2]],
                            axis=1) * LOG2E
    rel = jnp.arange(SWA_KEYS + T)[None, :] - T - jnp.arange(T)[:, None]
    swa_tiles = lookup(_rel_bucket(rel, thresholds), slice(2 * DIFF_HEADS, None))
    swa_tiles = jnp.where((jnp.abs(rel) <= WINDOW)[:, :, None], swa_tiles * LOG2E, NEG).transpose(2, 0, 1)
    return w_in_p, w_out_p, diff_tiles, diff_consts, swa_tiles


def _plan_rows(seq_len):
    nsub = min((4, 5, 6), key=lambda n: (-(-seq_len // (n * SEQ_TILE)) * n, -n))
    return nsub, -(-seq_len // (nsub * SEQ_TILE)) * nsub * SEQ_TILE


def _trunk(x, meta_tokens, prep, attn_norm_g, lamv, subln_g, swa_sink, ffn_norm_g, w_router,
           w_gate, w_up, w_down, final_norm_g):
    w_in_p, w_out_p, diff_tiles, diff_consts, swa_tiles = prep
    B, S, _ = x.shape
    L = S + N_META
    nsub, P = _plan_rows(L)
    rows = B * P
    capacity = EC_FACTOR * (B * L) // N_EXPERTS
    ffn_rows = -(-capacity // BF16_ROWS) * BF16_ROWS
    cp = -(-(capacity + WIN_ROWS) // BF16_ROWS) * BF16_ROWS

    h0 = jnp.concatenate([jnp.broadcast_to(meta_tokens.astype(x.dtype)[None], (B, N_META, D_MODEL)), x,
                          jnp.zeros((B, P - L, D_MODEL), x.dtype)], axis=1).reshape(rows, D_MODEL)

    tm = SEQ_TILE * _largest_divisor(rows // SEQ_TILE, 2)
    proj, vt = _proj(h0, attn_norm_g.reshape(1, D_MODEL), w_in_p, tm)
    diff_o = _diff_attention(proj, vt, diff_consts, lamv, diff_tiles, subln_g.reshape(LANES, 1), B, P, L, nsub)
    swa_o = _swa_attention(proj, swa_sink * LOG2E, swa_tiles, B, P, L)
    h2, m, probs3 = _out_router(diff_o, swa_o, h0, w_out_p, ffn_norm_g.reshape(1, D_MODEL), w_router, B, P, L)
    pos, post, gate, base3 = _topk(probs3, capacity)
    base_ec = base3[:, :, 0].T
    nxt = jnp.concatenate([base_ec[:, 1:], jnp.full((N_EXPERTS, 1), capacity, jnp.int32)], axis=1)
    cnt_ec = nxt - base_ec
    fast = jnp.all(cnt_ec <= FAST_CNT, axis=0).astype(jnp.int32)
    base_flat = base_ec.reshape(-1)
    xc, gc = _gather(base_flat, cnt_ec.reshape(-1), pos, gate, m, cp)
    yc = _ffn(xc, gc, w_gate, w_up, w_down, ffn_rows)
    return _combine(base_flat, fast, h2, post, final_norm_g.reshape(1, D_MODEL), yc, B, P, S)


def kernel(x_prompt, x_sample, meta_tokens, rel_bias, attn_norm_g, w_in, diff_lambda_q1, diff_lambda_k1,
           diff_lambda_q2, diff_lambda_k2, diff_subln_g, swa_sink, w_out, ffn_norm_g, w_router, w_gate, w_up,
           w_down, final_norm_g):
    prep = _prep_params(rel_bias, w_in[0], w_out[0])
    lamv = jnp.stack([diff_lambda_q1[0], diff_lambda_k1[0], diff_lambda_q2[0], diff_lambda_k2[0]]).astype(F32)
    args = (meta_tokens, prep, attn_norm_g[0], lamv, diff_subln_g[0], swa_sink[0].astype(F32), ffn_norm_g[0],
            w_router[0], jnp.swapaxes(w_gate[0], 1, 2), jnp.swapaxes(w_up[0], 1, 2), w_down[0], final_norm_g)
    return (_trunk(x_prompt, *args), _trunk(x_sample, *args))
```
